```python
import math
import jax, jax.numpy as jnp
from jax import lax
import numpy as np

D_MODEL = 1024
BATCH = 16
SEQ = 4096
DEPTH = 4

CHUNK = 64
HEAD_DIM = 64
H_A = 8
LEFT_CHUNKS = 8
REL_MAX = 256
H_B = 4
H_C = 8
H_IDX = 4
D_IDX = 64
TOPK_MAX = 256
H_X = 4
MEM_LEN = 256
D_FF = -(-8 * D_MODEL // (3 * 256)) * 256
QB = 128
ROPE_THETA = 10000.0
EPS = 1e-6

W_A = H_A * HEAD_DIM
W_B = H_B * 2 * HEAD_DIM
W_C = H_C * HEAD_DIM
IN_SPLITS = (W_A, W_A, W_A, W_B, W_B, W_B, W_C, W_C, W_C, H_IDX * D_IDX, D_IDX, H_IDX, 3 * D_MODEL)
N_IN = 3 * W_A + 3 * W_B + 3 * W_C + H_IDX * D_IDX + D_IDX + H_IDX + 3 * D_MODEL
REL_TABLE = CHUNK + REL_MAX

kernel_name = 'hybrid_chunk_causal_encoder'


def rmsnorm(x, g):
    x32 = x.astype(jnp.float32)
    y = x32 * lax.rsqrt(jnp.mean(x32 * x32, axis=-1, keepdims=True) + EPS)
    return y.astype(x.dtype) * g


def rope_tables(seq):
    pos = jnp.arange(seq, dtype=jnp.float32)
    inv = ROPE_THETA ** (-jnp.arange(0, HEAD_DIM, 2, dtype=jnp.float32) / HEAD_DIM)
    ang = pos[:, None] * inv[None, :]
    return jnp.cos(ang), jnp.sin(ang)


def apply_rope(x, cos, sin):
    extra = x.ndim - 3
    shp = (cos.shape[0],) + (1,) * extra + (cos.shape[1],)
    c = cos.reshape(shp).astype(x.dtype)
    s = sin.reshape(shp).astype(x.dtype)
    x1, x2 = jnp.split(x, 2, axis=-1)
    return jnp.concatenate([x1 * c - x2 * s, x2 * c + x1 * s], axis=-1)


def masked_softmax(scores, mask):
    return jax.nn.softmax(jnp.where(mask, scores.astype(jnp.float32), -jnp.inf), axis=-1)


def chunk_band_attention(q, k, v, rel_bias):
    b, s, h, d = q.shape
    nc = s // CHUNK
    band = LEFT_CHUNKS + 1
    pad = ((0, 0), (LEFT_CHUNKS * CHUNK, 0), (0, 0), (0, 0))
    kp = jnp.pad(k, pad)
    vp = jnp.pad(v, pad)
    i = jnp.arange(CHUNK)[:, None]
    j = jnp.arange(band * CHUNK)[None, :]
    rel = jnp.clip(LEFT_CHUNKS * CHUNK + i - j, -(CHUNK - 1), REL_MAX) + (CHUNK - 1)
    bias = rel_bias[:, rel].astype(jnp.float32)
    key_off = jnp.arange(band * CHUNK) // CHUNK - LEFT_CHUNKS
    scale = d ** -0.5

    def one_chunk(n):
        start = n * CHUNK
        qn = lax.dynamic_slice_in_dim(q, start, CHUNK, axis=1)
        kn = lax.dynamic_slice_in_dim(kp, start, band * CHUNK, axis=1)
        vn = lax.dynamic_slice_in_dim(vp, start, band * CHUNK, axis=1)
        sc = jnp.einsum('bqhd,bkhd->bhqk', qn, kn).astype(jnp.float32) * scale + bias[None]
        valid = (n + key_off >= 0)[None, None, None, :]
        p = masked_softmax(sc, valid).astype(v.dtype)
        return jnp.einsum('bhqk,bkhd->bqhd', p, vn)

    o = lax.map(one_chunk, jnp.arange(nc))
    return o.transpose(1, 0, 2, 3, 4).reshape(b, s, h * d)


def diff_attention(q, k, v, lam, lambda_init, subln):
    b, s, h, _, d = q.shape
    key_chunk = jnp.arange(s) // CHUNK
    scale = d ** -0.5

    def one_block(n):
        qn = lax.dynamic_slice_in_dim(q, n * QB, QB, axis=1)
        sc = jnp.einsum('bqhcd,bkhcd->bhcqk', qn, k).astype(jnp.float32) * scale
        q_chunk = (n * QB + jnp.arange(QB)) // CHUNK
        valid = key_chunk[None, :] <= q_chunk[:, None]
        p = masked_softmax(sc, valid)
        a = p[:, :, 0] - lam * p[:, :, 1]
        return jnp.einsum('bhqk,bkhe->bqhe', a.astype(v.dtype), v)

    o = lax.map(one_block, jnp.arange(s // QB))
    o = o.transpose(1, 0, 2, 3, 4).reshape(b, s, h, 2 * d)
    o = rmsnorm(o, subln) * (1.0 - lambda_init)
    return o.reshape(b, s, h * 2 * d)


def dsa_attention(q, k, v, q_idx, k_idx, w_idx, topk):
    b, s, h, d = q.shape
    key_pos = jnp.arange(s)
    gather = jax.vmap(lambda t, idx: t[idx])
    scale = d ** -0.5

    def one_chunk(n):
        start = n * CHUNK
        qn = lax.dynamic_slice_in_dim(q, start, CHUNK, axis=1)
        qi = lax.dynamic_slice_in_dim(q_idx, start, CHUNK, axis=1)
        wi = lax.dynamic_slice_in_dim(w_idx, start, CHUNK, axis=1).astype(jnp.float32)
        logits = jnp.einsum('bqhd,bsd->bqhs', qi, k_idx).astype(jnp.float32) * D_IDX ** -0.5
        score = jnp.einsum('bqhs,bqh->bqs', jax.nn.relu(logits), wi)
        admissible = key_pos < start + CHUNK
        score = jnp.where(admissible[None, None, :], score, -jnp.inf)
        vals, idx = lax.top_k(score, topk)
        valid = jnp.isfinite(vals)
        kg = gather(k, idx)
        vg = gather(v, idx)
        sc = jnp.einsum('bqhd,bqkhd->bhqk', qn, kg).astype(jnp.float32) * scale
        p = masked_softmax(sc, valid[:, None]).astype(v.dtype)
        return jnp.einsum('bhqk,bqkhd->bqhd', p, vg)

    o = lax.map(one_chunk, jnp.arange(s // CHUNK))
    return o.transpose(1, 0, 2, 3, 4).reshape(b, s, h * d)


def memory_cross_attention(hn, mem_n, w_q, w_kv, w_o):
    b, s, _ = hn.shape
    m = mem_n.shape[1]
    q = (hn @ w_q).reshape(b, s, H_X, HEAD_DIM)
    kv = (mem_n @ w_kv).reshape(b, m, 2, H_X, HEAD_DIM)
    sc = jnp.einsum('bqhd,bmhd->bhqm', q, kv[:, :, 0]).astype(jnp.float32) * HEAD_DIM ** -0.5
    p = jax.nn.softmax(sc, axis=-1).astype(hn.dtype)
    o = jnp.einsum('bhqm,bmhd->bqhd', p, kv[:, :, 1]).reshape(b, s, H_X * HEAD_DIM)
    return o @ w_o


def swiglu(hn, w_gu, w_down):
    g, u = jnp.split(hn @ w_gu, 2, axis=-1)
    return (jax.nn.silu(g) * u) @ w_down


def setup_inputs(seed: int = 0) -> dict:
    key = jax.random.key(seed)
    ks = jax.random.split(key, 20)

    def normal(k, shape, scale):
        return jax.random.normal(k, shape, jnp.float32) * scale

    def gain(k, shape):
        return 1.0 + normal(k, shape, 0.05)

    return {
        'x': normal(ks[0], (BATCH, SEQ, D_MODEL), 1.0),
        'mem': normal(ks[1], (BATCH, MEM_LEN, D_MODEL), 1.0),
        'norm_mix': gain(ks[2], (DEPTH, D_MODEL)),
        'w_in': normal(ks[3], (DEPTH, D_MODEL, N_IN), D_MODEL ** -0.5),
        'rel_bias_a': normal(ks[4], (DEPTH, H_A, REL_TABLE), 0.5),
        'lambda_vecs': normal(ks[5], (DEPTH, 4, HEAD_DIM), 0.1),
        'subln_b': gain(ks[6], (DEPTH, 2 * HEAD_DIM)),
        'w_up_a': normal(ks[7], (DEPTH, W_A, D_MODEL), W_A ** -0.5),
        'w_up_b': normal(ks[8], (DEPTH, W_B, D_MODEL), W_B ** -0.5),
        'w_up_c': normal(ks[9], (DEPTH, W_C, D_MODEL), W_C ** -0.5),
        'w_out': normal(ks[10], (DEPTH, D_MODEL, D_MODEL), D_MODEL ** -0.5),
        'norm_cross': gain(ks[11], (DEPTH, D_MODEL)),
        'w_q_x': normal(ks[12], (DEPTH, D_MODEL, H_X * HEAD_DIM), D_MODEL ** -0.5),
        'w_kv_x': normal(ks[13], (DEPTH, D_MODEL, 2 * H_X * HEAD_DIM), D_MODEL ** -0.5),
        'w_o_x': normal(ks[14], (DEPTH, H_X * HEAD_DIM, D_MODEL), (H_X * HEAD_DIM) ** -0.5),
        'norm_ffn': gain(ks[15], (DEPTH, D_MODEL)),
        'w_gu': normal(ks[16], (DEPTH, D_MODEL, 2 * D_FF), D_MODEL ** -0.5),
        'w_down': normal(ks[17], (DEPTH, D_FF, D_MODEL), D_FF ** -0.5),
        'mem_norm': gain(ks[18], (D_MODEL,)),
        'final_norm': gain(ks[19], (D_MODEL,)),
    }


def reference(x, mem, norm_mix, w_in, rel_bias_a, lambda_vecs, subln_b, w_up_a, w_up_b, w_up_c,
              w_out, norm_cross, w_q_x, w_kv_x, w_o_x, norm_ffn, w_gu, w_down, mem_norm, final_norm):
    b, s, _ = x.shape
    topk = min(TOPK_MAX, s // 4)
    cos, sin = rope_tables(s)
    offsets = [int(o) for o in np.cumsum(IN_SPLITS)[:-1]]
    mem_n = rmsnorm(mem, mem_norm)
    shp_a = (b, s, H_A, HEAD_DIM)
    shp_b = (b, s, H_B, 2, HEAD_DIM)
    shp_c = (b, s, H_C, HEAD_DIM)
    for l in range(DEPTH):
        xn = rmsnorm(x, norm_mix[l])
        (qa, ka, va, qb, kb, vb, qc, kc, vc, qi, ki, wi, gates) = jnp.split(xn @ w_in[l], offsets, axis=-1)
        o_a = chunk_band_attention(qa.reshape(shp_a), ka.reshape(shp_a), va.reshape(shp_a), rel_bias_a[l])
        lam_init = 0.8 - 0.6 * math.exp(-0.3 * l)
        lv = lambda_vecs[l].astype(jnp.float32)
        lam = jnp.exp(jnp.sum(lv[0] * lv[1])) - jnp.exp(jnp.sum(lv[2] * lv[3])) + lam_init
        o_b = diff_attention(apply_rope(qb.reshape(shp_b), cos, sin), apply_rope(kb.reshape(shp_b), cos, sin),
                             vb.reshape(b, s, H_B, 2 * HEAD_DIM), lam, lam_init, subln_b[l])
        o_c = dsa_attention(apply_rope(qc.reshape(shp_c), cos, sin), apply_rope(kc.reshape(shp_c), cos, sin),
                            vc.reshape(shp_c),
                            apply_rope(qi.reshape(b, s, H_IDX, D_IDX), cos, sin),
                            apply_rope(ki[:, :, None, :], cos, sin)[:, :, 0],
                            wi * H_IDX ** -0.5, topk)
        g = jax.nn.sigmoid(gates).reshape(b, s, 3, D_MODEL)
        y = (g[:, :, 0] * (o_a @ w_up_a[l]) + g[:, :, 1] * (o_b @ w_up_b[l])
             + g[:, :, 2] * (o_c @ w_up_c[l]))
        x = x + y @ w_out[l]
        x = x + memory_cross_attention(rmsnorm(x, norm_cross[l]), mem_n, w_q_x[l], w_kv_x[l], w_o_x[l])
        x = x + swiglu(rmsnorm(x, norm_ffn[l]), w_gu[l], w_down[l])
    return rmsnorm(x, final_norm)
```

```python
import functools
import math

import jax
import jax.numpy as jnp
import numpy as np
from jax import lax
from jax.experimental import pallas as pl
from jax.experimental.pallas import tpu as pltpu

F32 = jnp.float32
BF16 = jnp.bfloat16
I32 = jnp.int32

CHUNK = 64
HEAD_DIM = 64
H_A = 8
LEFT_CHUNKS = 8
REL_MAX = 256
H_B = 4
H_C = 8
H_IDX = 4
D_IDX = 64
TOPK_MAX = 256
H_X = 4
ROPE_THETA = 10000.0
EPS = 1e-6

LANES = 128
TQ = 256
TK = 256
PROJ_TN = 512
VMEM_LIMIT = 56 * 1024 * 1024

NEG_INIT = -1e30
NEG_MASK = -2e30
INT_MIN = -(2 ** 31)

CB_QB, CB_KB, CB_QC, CB_KC = 0, 4, 8, 12
CB_QI, CB_KI, CB_WI = 16, 18, 19
CB_QA, CB_KA, CB_VA, CB_VB, CB_VC = 20, 24, 28, 32, 36
CB_GATES = 40
N_ROPE_COLS = 2560
N_PROJ_COLS = 8192


def _dot(a, b):
    return jnp.dot(a, b, preferred_element_type=F32)


def _dot_nt(a, b):
    return lax.dot_general(a, b, (((1,), (1,)), ((), ())), preferred_element_type=F32)


def _rms(x, g):
    ms = jnp.mean(x * x, axis=-1, keepdims=True)
    return (x * lax.rsqrt(ms + EPS)) * g


def _params(sem, vmem=None):
    return pltpu.CompilerParams(dimension_semantics=sem, vmem_limit_bytes=vmem)


def _in_proj_kernel(x_ref, g_ref, w_ref, wr_ref, cos_ref, sin_ref, o_ref, xn_ref, *,
                    n_rope, gate_start):
    j = pl.program_id(1)

    @pl.when(j == 0)
    def _():
        xn_ref[...] = _rms(x_ref[...], g_ref[...]).astype(BF16)

    xn = xn_ref[...]
    tn = o_ref.shape[1]

    @pl.when(j < n_rope)
    def _():
        a = _dot(xn, w_ref[...])
        r = _dot(xn, wr_ref[...])
        cos = cos_ref[...]
        sin = sin_ref[...]
        plain_tail = j == n_rope - 1
        for c in range(tn // LANES):
            sl = slice(c * LANES, (c + 1) * LANES)
            cc, ss = cos, sin
            if c == tn // LANES - 1:
                cc = jnp.where(plain_tail, 1.0, cos)
                ss = jnp.where(plain_tail, 0.0, sin)
            o_ref[:, sl] = (a[:, sl] * cc + r[:, sl] * ss).astype(o_ref.dtype)

    @pl.when((j >= n_rope) & (j < gate_start))
    def _():
        o_ref[...] = _dot(xn, w_ref[...]).astype(o_ref.dtype)

    @pl.when(j >= gate_start)
    def _():
        a = _dot(xn, w_ref[...])
        o_ref[...] = (1.0 / (1.0 + jnp.exp(-a))).astype(o_ref.dtype)


def _in_proj(x2, gain, w_main, w_rot, cos_t, sin_t, seq):
    t, d = x2.shape
    tm = min(1024, seq)
    tn = PROJ_TN
    n_rope = N_ROPE_COLS // tn
    gate_start = CB_GATES * LANES // tn
    pos_blocks = seq // tm
    kern = functools.partial(_in_proj_kernel, n_rope=n_rope, gate_start=gate_start)
    return pl.pallas_call(
        kern,
        out_shape=jax.ShapeDtypeStruct((t, N_PROJ_COLS), BF16),
        grid=(t // tm, N_PROJ_COLS // tn),
        in_specs=[
            pl.BlockSpec((tm, d), lambda i, j: (i, 0)),
            pl.BlockSpec((1, d), lambda i, j: (0, 0)),
            pl.BlockSpec((d, tn), lambda i, j: (0, j)),
            pl.BlockSpec((d, tn), lambda i, j: (0, jnp.minimum(j, n_rope - 1))),
            pl.BlockSpec((tm, LANES), lambda i, j: (i % pos_blocks, 0)),
            pl.BlockSpec((tm, LANES), lambda i, j: (i % pos_blocks, 0)),
        ],
        out_specs=pl.BlockSpec((tm, tn), lambda i, j: (i, j)),
        scratch_shapes=[pltpu.VMEM((tm, d), BF16)],
        compiler_params=_params(("arbitrary", "arbitrary"), VMEM_LIMIT),
        name="in_proj",
    )(x2, gain, w_main, w_rot, cos_t, sin_t)


def _band_kernel(q_ref, k0_ref, k1_ref, k2_ref, v0_ref, v1_ref, v2_ref, bias_ref, o_ref):
    i = pl.program_id(2)
    tq = q_ref.shape[0]
    q = q_ref[...]
    lo = lax.broadcasted_iota(I32, (tq, LANES), 1) < HEAD_DIM
    kw = jnp.concatenate([k0_ref[...], k1_ref[...], k2_ref[...]], axis=0)
    vw = jnp.concatenate([v0_ref[...], v1_ref[...], v2_ref[...]], axis=0)
    kj = lax.broadcasted_iota(I32, (tq, 3 * tq), 1)
    kvalid = kj >= (2 - i) * tq
    outs = []
    for hh in range(2):
        qm = jnp.where(lo if hh == 0 else jnp.logical_not(lo), q, jnp.zeros_like(q))
        s = _dot_nt(qm, kw) + bias_ref[hh]
        s = jnp.where(kvalid, s, NEG_MASK)
        m = jnp.max(s, axis=-1, keepdims=True)
        p = jnp.exp(s - m)
        l = jnp.sum(p, axis=-1, keepdims=True)
        outs.append(_dot(p.astype(BF16), vw) / l)
    o_ref[...] = jnp.where(lo, outs[0], outs[1]).astype(o_ref.dtype)


def _band_bias(rel_bias, tq):
    back = LEFT_CHUNKS * CHUNK
    assert back == 2 * tq
    qi = np.arange(tq)[:, None] + back
    kj = np.arange(3 * tq)[None, :]
    d = qi - kj
    rel = np.clip(d, -(CHUNK - 1), REL_MAX) + (CHUNK - 1)
    dc = qi // CHUNK - kj // CHUNK
    valid = (dc >= 0) & (dc <= LEFT_CHUNKS)
    bias = rel_bias[:, rel].astype(F32)
    return jnp.where(jnp.asarray(valid)[None], bias, NEG_MASK)


def _band_attention(proj3, bias):
    b, s, _ = proj3.shape
    tq = TQ
    nq = s // tq

    def kspec(base, off):
        return pl.BlockSpec((None, tq, LANES),
                            lambda hp, bb, i: (bb, jnp.maximum(i + off, 0), base + hp))

    return pl.pallas_call(
        _band_kernel,
        out_shape=jax.ShapeDtypeStruct((b, s, H_A * HEAD_DIM), BF16),
        grid=(H_A // 2, b, nq),
        in_specs=[
            pl.BlockSpec((None, tq, LANES), lambda hp, bb, i: (bb, i, CB_QA + hp)),
            kspec(CB_KA, -2), kspec(CB_KA, -1), kspec(CB_KA, 0),
            kspec(CB_VA, -2), kspec(CB_VA, -1), kspec(CB_VA, 0),
            pl.BlockSpec((2, tq, 3 * tq), lambda hp, bb, i: (hp, 0, 0)),
        ],
        out_specs=pl.BlockSpec((None, tq, LANES), lambda hp, bb, i: (bb, i, hp)),
        compiler_params=_params(("arbitrary", "arbitrary", "arbitrary"), VMEM_LIMIT),
        name="band_attn",
    )(proj3, proj3, proj3, proj3, proj3, proj3, proj3, bias)


def _diff_kernel(q_ref, k_ref, v_ref, lv_ref, sub_ref, o_ref, m_ref, l_ref, acc_ref, *,
                 lam_init):
    i = pl.program_id(2)
    tq = q_ref.shape[0]
    tk = TK
    q = q_ref[...]
    lo = lax.broadcasted_iota(I32, (tq, LANES), 1) < HEAD_DIM
    zero = jnp.zeros_like(q)
    qms = (jnp.where(lo, q, zero), jnp.where(lo, zero, q))

    m_ref[...] = jnp.full(m_ref.shape, NEG_INIT, F32)
    l_ref[...] = jnp.zeros(l_ref.shape, F32)
    acc_ref[...] = jnp.zeros(acc_ref.shape, F32)

    def block(j, mask):
        start = pl.multiple_of(j * tk, tk)
        kb = k_ref[pl.ds(start, tk), :]
        vb = v_ref[pl.ds(start, tk), :]
        for c in range(2):
            s = _dot_nt(qms[c], kb)
            if mask is not None:
                s = jnp.where(mask, s, NEG_MASK)
            m_old = m_ref[c]
            m_new = jnp.maximum(m_old, jnp.max(s, axis=-1, keepdims=True))
            p = jnp.exp(s - m_new)
            alpha = jnp.exp(m_old - m_new)
            l_ref[c] = alpha * l_ref[c] + jnp.sum(p, axis=-1, keepdims=True)
            acc_ref[c] = alpha * acc_ref[c] + _dot(p.astype(BF16), vb)
            m_ref[c] = m_new

    def body(j, carry):
        block(j, None)
        return carry

    lax.fori_loop(0, i, body, 0)
    rq = lax.broadcasted_iota(I32, (tq, tk), 0) // CHUNK
    ck = lax.broadcasted_iota(I32, (tq, tk), 1) // CHUNK
    block(i, ck <= rq)

    lv = lv_ref[...]
    lam = (jnp.exp(jnp.sum(lv[0:1] * lv[1:2], axis=-1, keepdims=True))
           - jnp.exp(jnp.sum(lv[2:3] * lv[3:4], axis=-1, keepdims=True)) + lam_init)
    o = acc_ref[0] / l_ref[0] - lam * (acc_ref[1] / l_ref[1])
    o = _rms(o, sub_ref[...]) * (1.0 - lam_init)
    o_ref[...] = o.astype(o_ref.dtype)


def _diff_attention(proj3, lambda_vec, subln, lam_init):
    b, s, _ = proj3.shape
    tq = TQ
    kern = functools.partial(_diff_kernel, lam_init=lam_init)
    return pl.pallas_call(
        kern,
        out_shape=jax.ShapeDtypeStruct((b, s, H_B * 2 * HEAD_DIM), BF16),
        grid=(b, H_B, s // tq),
        in_specs=[
            pl.BlockSpec((None, tq, LANES), lambda bb, h, i: (bb, i, CB_QB + h)),
            pl.BlockSpec((None, s, LANES), lambda bb, h, i: (bb, 0, CB_KB + h)),
            pl.BlockSpec((None, s, LANES), lambda bb, h, i: (bb, 0, CB_VB + h)),
            pl.BlockSpec((4, HEAD_DIM), lambda bb, h, i: (0, 0)),
            pl.BlockSpec((1, LANES), lambda bb, h, i: (0, 0)),
        ],
        out_specs=pl.BlockSpec((None, tq, LANES), lambda bb, h, i: (bb, i, h)),
        scratch_shapes=[
            pltpu.VMEM((2, tq, 1), F32),
            pltpu.VMEM((2, tq, 1), F32),
            pltpu.VMEM((2, tq, LANES), F32),
        ],
        compiler_params=_params(("arbitrary", "arbitrary", "arbitrary"), VMEM_LIMIT),
        name="diff_attn",
    )(proj3, proj3, proj3, lambda_vec, subln)


def _sparse_kernel(q_ref, k_ref, v_ref, qi_ref, ki_ref, wi_ref, o_ref,
                   keys_ref, m_ref, l_ref, acc_ref, *, topk):
    i = pl.program_id(1)
    tq = q_ref.shape[0]
    tk = TK
    lane = lax.broadcasted_iota(I32, (tq, LANES), 1)
    lo = lane < HEAD_DIM

    wi = wi_ref[...].astype(F32) * (H_IDX ** -0.5)
    w_cols = [wi[:, h:h + 1] for h in range(H_IDX)]
    qi = qi_ref[...]
    qi_heads = []
    for h in range(H_IDX):
        pair = qi[:, (h // 2) * LANES:(h // 2 + 1) * LANES]
        zero = jnp.zeros_like(pair)
        qi_heads.append(jnp.where(lo, pair, zero) if h % 2 == 0 else jnp.where(lo, zero, pair))

    rq = lax.broadcasted_iota(I32, (tq, tk), 0) // CHUNK
    ck = lax.broadcasted_iota(I32, (tq, tk), 1) // CHUNK
    diag_ok = ck <= rq

    def score_block(j, admissible):
        start = pl.multiple_of(j * tk, tk)
        kk = ki_ref[pl.ds(start, tk), :]
        sc = jnp.zeros((tq, tk), F32)
        for h in range(H_IDX):
            sc = sc + jnp.maximum(_dot_nt(qi_heads[h], kk), 0.0) * w_cols[h]
        bits = pltpu.bitcast(sc, I32)
        key = bits ^ ((bits >> 31) & 0x7FFFFFFF)
        key = jnp.where(sc == 0.0, 0, key)
        if admissible is not None:
            key = jnp.where(admissible, key, INT_MIN)
        keys_ref[j] = key

    def score_body(j, carry):
        score_block(j, None)
        return carry

    lax.fori_loop(0, i, score_body, 0)
    score_block(i, diag_ok)

    def count(pred_fn):
        def inner(j, acc):
            blk = keys_ref[j]
            for c in range(tk // LANES):
                acc = acc + jnp.where(pred_fn(blk[:, c * LANES:(c + 1) * LANES]), 1.0, 0.0)
            return acc
        acc = lax.fori_loop(0, i + 1, inner, jnp.zeros((tq, LANES), F32))
        return jnp.sum(acc, axis=-1, keepdims=True)

    def radix_body(it, prefix):
        bit = lax.shift_left(jnp.int32(1), 31 - it)
        cand_u = prefix | bit
        cand = jnp.broadcast_to(cand_u ^ INT_MIN, (tq, LANES))
        cnt = count(lambda blk: blk >= cand)
        return jnp.where(cnt >= topk, cand_u, prefix)

    prefix = lax.fori_loop(0, 32, radix_body, jnp.zeros((tq, 1), I32))
    thr = prefix ^ INT_MIN
    thr_b = jnp.broadcast_to(thr, (tq, LANES))
    n_gt = count(lambda blk: blk > thr_b)
    need = jnp.where(prefix == 0, 0.0, topk - n_gt)

    m_ref[...] = jnp.full(m_ref.shape, NEG_INIT, F32)
    l_ref[...] = jnp.zeros(l_ref.shape, F32)
    acc_ref[...] = jnp.zeros(acc_ref.shape, F32)
    q = q_ref[...]
    q_heads = []
    for h in range(H_C):
        pair = q[:, (h // 2) * LANES:(h // 2 + 1) * LANES]
        zero = jnp.zeros_like(pair)
        q_heads.append(jnp.where(lo, pair, zero) if h % 2 == 0 else jnp.where(lo, zero, pair))
    before = (lax.broadcasted_iota(I32, (tk, tk), 0)
              < lax.broadcasted_iota(I32, (tk, tk), 1))
    before = jnp.where(before, 1.0, 0.0).astype(BF16)
    thr_k = jnp.broadcast_to(thr, (tq, tk))

    def attn_body(j, tie_seen):
        start = pl.multiple_of(j * tk, tk)
        kblk = keys_ref[j]
        tie = kblk == thr_k
        tie_f = jnp.where(tie, 1.0, 0.0)
        rank = _dot(tie_f.astype(BF16), before) + tie_seen
        sel = (kblk > thr_k) | (tie & (rank < need))
        bias = jnp.where(sel, 0.0, NEG_MASK)
        for hp in range(H_C // 2):
            sl = slice(hp * LANES, (hp + 1) * LANES)
            kb = k_ref[pl.ds(start, tk), sl]
            vb = v_ref[pl.ds(start, tk), sl]
            pv = []
            alphas = []
            for hh in range(2):
                h = 2 * hp + hh
                s = _dot_nt(q_heads[h], kb) + bias
                m_old = m_ref[h]
                m_new = jnp.maximum(m_old, jnp.max(s, axis=-1, keepdims=True))
                p = jnp.exp(s - m_new)
                alpha = jnp.exp(m_old - m_new)
                l_ref[h] = alpha * l_ref[h] + jnp.sum(p, axis=-1, keepdims=True)
                m_ref[h] = m_new
                pv.append(_dot(p.astype(BF16), vb))
                alphas.append(alpha)
            acc_ref[:, sl] = (jnp.where(lo, alphas[0], alphas[1]) * acc_ref[:, sl]
                              + jnp.where(lo, pv[0], pv[1]))
        return tie_seen + jnp.sum(tie_f, axis=-1, keepdims=True)

    lax.fori_loop(0, i + 1, attn_body, jnp.zeros((tq, 1), F32))

    for hp in range(H_C // 2):
        sl = slice(hp * LANES, (hp + 1) * LANES)
        inv = jnp.where(lo, 1.0 / l_ref[2 * hp], 1.0 / l_ref[2 * hp + 1])
        o_ref[:, sl] = (acc_ref[:, sl] * inv).astype(o_ref.dtype)


def _sparse_attention(proj3, topk):
    b, s, _ = proj3.shape
    tq = TQ
    wc = H_C * HEAD_DIM
    kern = functools.partial(_sparse_kernel, topk=float(topk))
    return pl.pallas_call(
        kern,
        out_shape=jax.ShapeDtypeStruct((b, s, wc), BF16),
        grid=(b, s // tq),
        in_specs=[
            pl.BlockSpec((None, tq, wc), lambda bb, i: (bb, i, CB_QC * LANES // wc)),
            pl.BlockSpec((None, s, wc), lambda bb, i: (bb, 0, CB_KC * LANES // wc)),
            pl.BlockSpec((None, s, wc), lambda bb, i: (bb, 0, CB_VC * LANES // wc)),
            pl.BlockSpec((None, tq, 2 * LANES), lambda bb, i: (bb, i, CB_QI // 2)),
            pl.BlockSpec((None, s, LANES), lambda bb, i: (bb, 0, CB_KI)),
            pl.BlockSpec((None, tq, LANES), lambda bb, i: (bb, i, CB_WI)),
        ],
        out_specs=pl.BlockSpec((None, tq, wc), lambda bb, i: (bb, i, 0)),
        scratch_shapes=[
            pltpu.VMEM((s // TK, tq, TK), I32),
            pltpu.VMEM((H_C, tq, 1), F32),
            pltpu.VMEM((H_C, tq, 1), F32),
            pltpu.VMEM((tq, wc), F32),
        ],
        compiler_params=_params(("arbitrary", "arbitrary"), VMEM_LIMIT),
        name="sparse_attn",
    )(proj3, proj3, proj3, proj3, proj3, proj3)


def _mix_kernel(x_ref, oa_ref, ob_ref, oc_ref, ga_ref, gb_ref, gc_ref,
                ua_ref, ub_ref, uc_ref, wo_ref, o_ref):
    y = ga_ref[...].astype(F32) * _dot(oa_ref[...], ua_ref[...])
    y = y + gb_ref[...].astype(F32) * _dot(ob_ref[...], ub_ref[...])
    y = y + gc_ref[...].astype(F32) * _dot(oc_ref[...], uc_ref[...])
    o_ref[...] = x_ref[...] + _dot(y.astype(BF16), wo_ref[...])


def _mix_out(x2, oa, ob, oc, proj2, ua, ub, uc, wo):
    t, d = x2.shape
    tm = 512
    gate_blk = CB_GATES * LANES // d
    w = oa.shape[1]

    def full(shape):
        return pl.BlockSpec(shape, lambda i: (0, 0))

    return pl.pallas_call(
        _mix_kernel,
        out_shape=jax.ShapeDtypeStruct((t, d), F32),
        grid=(t // tm,),
        in_specs=[
            pl.BlockSpec((tm, d), lambda i: (i, 0)),
            pl.BlockSpec((tm, w), lambda i: (i, 0)),
            pl.BlockSpec((tm, w), lambda i: (i, 0)),
            pl.BlockSpec((tm, w), lambda i: (i, 0)),
            pl.BlockSpec((tm, d), lambda i: (i, gate_blk)),
            pl.BlockSpec((tm, d), lambda i: (i, gate_blk + 1)),
            pl.BlockSpec((tm, d), lambda i: (i, gate_blk + 2)),
            full((w, d)), full((w, d)), full((w, d)), full((d, d)),
        ],
        out_specs=pl.BlockSpec((tm, d), lambda i: (i, 0)),
        compiler_params=_params(("arbitrary",), VMEM_LIMIT),
        name="mix_out",
    )(x2, oa, ob, oc, proj2, proj2, proj2, ua, ub, uc, wo)


def _mem_kv_kernel(mem_ref, g_ref, w_ref, o_ref):
    o_ref[...] = _dot(_rms(mem_ref[...], g_ref[...]).astype(BF16), w_ref[...]).astype(o_ref.dtype)


def _mem_kv(mem2, mem_norm, w_kv):
    r, d = mem2.shape
    depth, _, n = w_kv.shape
    tm = min(1024, r)
    return pl.pallas_call(
        _mem_kv_kernel,
        out_shape=jax.ShapeDtypeStruct((depth, r, n), BF16),
        grid=(depth, r // tm),
        in_specs=[
            pl.BlockSpec((tm, d), lambda l, i: (i, 0)),
            pl.BlockSpec((1, d), lambda l, i: (0, 0)),
            pl.BlockSpec((None, d, n), lambda l, i: (l, 0, 0)),
        ],
        out_specs=pl.BlockSpec((None, tm, n), lambda l, i: (l, i, 0)),
        compiler_params=_params(("arbitrary", "arbitrary"), VMEM_LIMIT),
        name="mem_kv",
    )(mem2, mem_norm, w_kv)


def _cross_kernel(x_ref, g_ref, wq_ref, kv_ref, wo_ref, o_ref):
    x = x_ref[...]
    tm = x.shape[0]
    hn = _rms(x, g_ref[...]).astype(BF16)
    q = _dot(hn, wq_ref[...]).astype(BF16)
    kv = kv_ref[...]
    wkv = H_X * HEAD_DIM
    lo = lax.broadcasted_iota(I32, (tm, LANES), 1) < HEAD_DIM
    pairs = []
    for hp in range(H_X // 2):
        qp = q[:, hp * LANES:(hp + 1) * LANES]
        kp = kv[:, hp * LANES:(hp + 1) * LANES]
        vp = kv[:, wkv + hp * LANES:wkv + (hp + 1) * LANES]
        zero = jnp.zeros_like(qp)
        outs = []
        for hh in range(2):
            qm = jnp.where(lo, qp, zero) if hh == 0 else jnp.where(lo, zero, qp)
            s = _dot_nt(qm, kp)
            m = jnp.max(s, axis=-1, keepdims=True)
            p = jnp.exp(s - m)
            l = jnp.sum(p, axis=-1, keepdims=True)
            outs.append(_dot(p.astype(BF16), vp) / l)
        pairs.append(jnp.where(lo, outs[0], outs[1]).astype(BF16))
    o = jnp.concatenate(pairs, axis=1)
    o_ref[...] = x + _dot(o, wo_ref[...])


def _cross_attention(x2, gain, wq, kv, wo, seq):
    t, d = x2.shape
    tm = min(512, seq)
    per_batch = seq // tm
    n_mem, wkv2 = kv.shape[1], kv.shape[2]
    wq_cols = wq.shape[1]
    return pl.pallas_call(
        _cross_kernel,
        out_shape=jax.ShapeDtypeStruct((t, d), F32),
        grid=(t // tm,),
        in_specs=[
            pl.BlockSpec((tm, d), lambda i: (i, 0)),
            pl.BlockSpec((1, d), lambda i: (0, 0)),
            pl.BlockSpec((d, wq_cols), lambda i: (0, 0)),
            pl.BlockSpec((None, n_mem, wkv2), lambda i: (i // per_batch, 0, 0)),
            pl.BlockSpec((wq_cols, d), lambda i: (0, 0)),
        ],
        out_specs=pl.BlockSpec((tm, d), lambda i: (i, 0)),
        compiler_params=_params(("arbitrary",), VMEM_LIMIT),
        name="cross_attn",
    )(x2, gain, wq, kv, wo)


def _ffn_kernel(x_ref, g_ref, wg_ref, wu_ref, wd_ref, fg_ref, o_ref, hn_ref, acc_ref, *,
                final_norm):
    c = pl.program_id(1)

    @pl.when(c == 0)
    def _():
        hn_ref[...] = _rms(x_ref[...], g_ref[...]).astype(BF16)
        acc_ref[...] = jnp.zeros(acc_ref.shape, F32)

    hn = hn_ref[...]
    gate = _dot(hn, wg_ref[...])
    up = _dot(hn, wu_ref[...])
    h = (gate / (1.0 + jnp.exp(-gate))) * up
    acc_ref[...] += _dot(h.astype(BF16), wd_ref[...])

    @pl.when(c == pl.num_programs(1) - 1)
    def _():
        y = x_ref[...] + acc_ref[...]
        if final_norm:
            y = _rms(y, fg_ref[...])
        o_ref[...] = y


def _ffn(x2, gain, w_gu, w_down, final_gain, final_norm):
    t, d = x2.shape
    d_ff = w_down.shape[0]
    tm = min(1024, t)
    tf = 256
    nf = d_ff // tf
    kern = functools.partial(_ffn_kernel, final_norm=final_norm)
    return pl.pallas_call(
        kern,
        out_shape=jax.ShapeDtypeStruct((t, d), F32),
        grid=(t // tm, nf),
        in_specs=[
            pl.BlockSpec((tm, d), lambda i, c: (i, 0)),
            pl.BlockSpec((1, d), lambda i, c: (0, 0)),
            pl.BlockSpec((d, tf), lambda i, c: (0, c)),
            pl.BlockSpec((d, tf), lambda i, c: (0, c + nf)),
            pl.BlockSpec((tf, d), lambda i, c: (c, 0)),
            pl.BlockSpec((1, d), lambda i, c: (0, 0)),
        ],
        out_specs=pl.BlockSpec((tm, d), lambda i, c: (i, 0)),
        scratch_shapes=[pltpu.VMEM((tm, d), BF16), pltpu.VMEM((tm, d), F32)],
        compiler_params=_params(("arbitrary", "arbitrary"), VMEM_LIMIT),
        name="ffn",
    )(x2, gain, w_gu, w_gu, w_down, final_gain)


def _rotate_half_cols(w):
    d = w.shape[0]
    wr = w.reshape(d, -1, 2, HEAD_DIM // 2)
    return jnp.concatenate([-wr[:, :, 1], wr[:, :, 0]], axis=2).reshape(d, -1)


def _layout_in_weights(w_in_l):
    d = w_in_l.shape[0]
    w_a, w_b, w_c = H_A * HEAD_DIM, H_B * 2 * HEAD_DIM, H_C * HEAD_DIM
    splits = (w_a, w_a, w_a, w_b, w_b, w_b, w_c, w_c, w_c, H_IDX * D_IDX, D_IDX, H_IDX, 3 * d)
    offs = [int(o) for o in np.cumsum(splits)[:-1]]
    qa, ka, va, qb, kb, vb, qc, kc, vc, qi, ki, wi, gates = jnp.split(w_in_l, offs, axis=1)
    scale = HEAD_DIM ** -0.5
    qa, qb, qc, qi = qa * scale, qb * scale, qc * scale, qi * (D_IDX ** -0.5)
    pad = jnp.zeros((d, LANES - H_IDX), w_in_l.dtype)
    tail = jnp.concatenate([qi, ki, ki, wi, pad], axis=1)
    tail_rot = jnp.concatenate(
        [_rotate_half_cols(qi), _rotate_half_cols(ki), _rotate_half_cols(ki),
         jnp.zeros((d, LANES), w_in_l.dtype)], axis=1)
    w_main = jnp.concatenate([qb, kb, qc, kc, tail, qa, ka, va, vb, vc, gates], axis=1)
    w_rot = jnp.concatenate([_rotate_half_cols(qb), _rotate_half_cols(kb),
                             _rotate_half_cols(qc), _rotate_half_cols(kc), tail_rot], axis=1)
    assert w_main.shape[1] == N_PROJ_COLS and w_rot.shape[1] == N_ROPE_COLS
    return w_main.astype(BF16), w_rot.astype(BF16)


def _rope_tables(seq):
    pos = jnp.arange(seq, dtype=F32)
    inv = ROPE_THETA ** (-jnp.arange(0, HEAD_DIM, 2, dtype=F32) / HEAD_DIM)
    ang = pos[:, None] * inv[None, :]
    reps = LANES // (HEAD_DIM // 2)
    return jnp.tile(jnp.cos(ang), (1, reps)), jnp.tile(jnp.sin(ang), (1, reps))


def kernel(x, mem, norm_mix, w_in, rel_bias_a, lambda_vecs, subln_b, w_up_a, w_up_b, w_up_c,
           w_out, norm_cross, w_q_x, w_kv_x, w_o_x, norm_ffn, w_gu, w_down, mem_norm, final_norm):
    b, s, d = x.shape
    depth = w_in.shape[0]
    assert s % TQ == 0 and LEFT_CHUNKS * CHUNK == 2 * TQ and TQ == TK
    topk = min(TOPK_MAX, s // 4)
    cos_t, sin_t = _rope_tables(s)
    x2 = x.reshape(b * s, d)
    mem2 = mem.reshape(b * mem.shape[1], d)
    kv_all = _mem_kv(mem2, mem_norm.reshape(1, d), w_kv_x.astype(BF16))
    kv_all = kv_all.reshape(depth, b, mem.shape[1], -1)
    for l in range(depth):
        w_main, w_rot = _layout_in_weights(w_in[l])
        proj2 = _in_proj(x2, norm_mix[l].reshape(1, d), w_main, w_rot, cos_t, sin_t, s)
        proj3 = proj2.reshape(b, s, N_PROJ_COLS)
        lam_init = 0.8 - 0.6 * math.exp(-0.3 * l)
        o_a = _band_attention(proj3, _band_bias(rel_bias_a[l], TQ))
        o_b = _diff_attention(proj3, lambda_vecs[l].astype(F32),
                              subln_b[l].reshape(1, LANES), lam_init)
        o_c = _sparse_attention(proj3, topk)
        x2 = _mix_out(x2, o_a.reshape(b * s, -1), o_b.reshape(b * s, -1), o_c.reshape(b * s, -1),
                      proj2, w_up_a[l].astype(BF16), w_up_b[l].astype(BF16),
                      w_up_c[l].astype(BF16), w_out[l].astype(BF16))
        x2 = _cross_attention(x2, norm_cross[l].reshape(1, d),
                              (w_q_x[l] * HEAD_DIM ** -0.5).astype(BF16), kv_all[l],
                              w_o_x[l].astype(BF16), s)
        x2 = _ffn(x2, norm_ffn[l].reshape(1, d), w_gu[l].astype(BF16), w_down[l].astype(BF16),
                  final_norm.reshape(1, d), final_norm=(l == depth - 1))
    return x2.reshape(b, s, d)
```

```python
import functools
import math

import jax
import jax.numpy as jnp
import numpy as np
from jax import lax
from jax.experimental import pallas as pl
from jax.experimental.pallas import tpu as pltpu

F32 = jnp.float32
BF16 = jnp.bfloat16
I32 = jnp.int32

CHUNK = 64
HEAD_DIM = 64
H_A = 8
LEFT_CHUNKS = 8
REL_MAX = 256
H_B = 4
H_C = 8
H_IDX = 4
D_IDX = 64
TOPK_MAX = 256
H_X = 4
ROPE_THETA = 10000.0
EPS = 1e-6

LANES = 128
SUBLANES = 8
TQ = 256
TK = 256
TQ_B = 512
PROJ_TN = 512
VMEM_LIMIT = 56 * 1024 * 1024

NEG_INIT = -1e30
NEG_MASK = -2e30
INT_MIN = -(2 ** 31)

CB_QB, CB_KB, CB_QC, CB_KC = 0, 4, 8, 12
CB_QI, CB_KI, CB_WI = 16, 18, 19
CB_QA, CB_KA, CB_VA = 20, 24, 28
CB_GATES = 32
N_ROPE_COLS = 2560
N_PROJ_COLS = 7168


def _dot(a, b):
    return jnp.dot(a, b, preferred_element_type=F32)


def _dot_nt(a, b):
    return lax.dot_general(a, b, (((1,), (1,)), ((), ())), preferred_element_type=F32)


def _rms(x, g):
    ms = jnp.mean(x * x, axis=-1, keepdims=True)
    return (x * lax.rsqrt(ms + EPS)) * g


def _params(sem, vmem=None):
    return pltpu.CompilerParams(dimension_semantics=sem, vmem_limit_bytes=vmem)


def _head_halves(pair):
    lo = lax.broadcasted_iota(I32, pair.shape, 1) < HEAD_DIM
    zero = jnp.zeros_like(pair)
    return jnp.where(lo, pair, zero), jnp.where(lo, zero, pair)


def _in_proj_kernel(x_ref, g_ref, w_ref, wr_ref, cos_ref, sin_ref, wvb_ref, wvc_ref,
                    o_ref, vbt_ref, vct_ref, xn_ref, *, n_rope, gate_start):
    j = pl.program_id(1)

    @pl.when(j == 0)
    def _():
        xn = _rms(x_ref[...], g_ref[...]).astype(BF16)
        xn_ref[...] = xn
        vbt = _dot_nt(wvb_ref[...], xn).astype(BF16)
        for c in range(vbt_ref.shape[0]):
            w = vbt_ref.shape[2]
            vbt_ref[c] = vbt[:, c * w:(c + 1) * w]
        vct = _dot_nt(wvc_ref[...], xn).astype(BF16)
        for c in range(vct_ref.shape[0]):
            w = vct_ref.shape[2]
            vct_ref[c] = vct[:, c * w:(c + 1) * w]

    xn = xn_ref[...]
    tn = o_ref.shape[1]

    @pl.when(j < n_rope)
    def _():
        a = _dot(xn, w_ref[...])
        r = _dot(xn, wr_ref[...])
        cos = cos_ref[...]
        sin = sin_ref[...]
        plain_tail = j == n_rope - 1
        for c in range(tn // LANES):
            sl = slice(c * LANES, (c + 1) * LANES)
            cc, ss = cos, sin
            if c == tn // LANES - 1:
                cc = jnp.where(plain_tail, 1.0, cos)
                ss = jnp.where(plain_tail, 0.0, sin)
            o_ref[:, sl] = (a[:, sl] * cc + r[:, sl] * ss).astype(o_ref.dtype)

    @pl.when((j >= n_rope) & (j < gate_start))
    def _():
        o_ref[...] = _dot(xn, w_ref[...]).astype(o_ref.dtype)

    @pl.when(j >= gate_start)
    def _():
        a = _dot(xn, w_ref[...])
        o_ref[...] = (1.0 / (1.0 + jnp.exp(-a))).astype(o_ref.dtype)


def _in_proj(x2, gain, w_main, w_rot, cos_t, sin_t, w_vbt, w_vct, seq):
    t, d = x2.shape
    tm = min(1024, seq)
    tn = PROJ_TN
    n_rope = N_ROPE_COLS // tn
    gate_start = CB_GATES * LANES // tn
    pos_blocks = seq // tm
    wb, wc = w_vbt.shape[0], w_vct.shape[0]
    kern = functools.partial(_in_proj_kernel, n_rope=n_rope, gate_start=gate_start)
    return pl.pallas_call(
        kern,
        out_shape=(
            jax.ShapeDtypeStruct((t, N_PROJ_COLS), BF16),
            jax.ShapeDtypeStruct((t // TQ_B, wb, TQ_B), BF16),
            jax.ShapeDtypeStruct((t // TK, wc, TK), BF16),
        ),
        grid=(t // tm, N_PROJ_COLS // tn),
        in_specs=[
            pl.BlockSpec((tm, d), lambda i, j: (i, 0)),
            pl.BlockSpec((1, d), lambda i, j: (0, 0)),
            pl.BlockSpec((d, tn), lambda i, j: (0, j)),
            pl.BlockSpec((d, tn), lambda i, j: (0, jnp.minimum(j, n_rope - 1))),
            pl.BlockSpec((tm, LANES), lambda i, j: (i % pos_blocks, 0)),
            pl.BlockSpec((tm, LANES), lambda i, j: (i % pos_blocks, 0)),
            pl.BlockSpec((wb, d), lambda i, j: (0, 0)),
            pl.BlockSpec((wc, d), lambda i, j: (0, 0)),
        ],
        out_specs=(
            pl.BlockSpec((tm, tn), lambda i, j: (i, j)),
            pl.BlockSpec((tm // TQ_B, wb, TQ_B), lambda i, j: (i, 0, 0)),
            pl.BlockSpec((tm // TK, wc, TK), lambda i, j: (i, 0, 0)),
        ),
        scratch_shapes=[pltpu.VMEM((tm, d), BF16)],
        compiler_params=_params(("arbitrary", "arbitrary"), VMEM_LIMIT),
        name="in_proj",
    )(x2, gain, w_main, w_rot, cos_t, sin_t, w_vbt, w_vct)


def _band_kernel(q_ref, k0_ref, k1_ref, k2_ref, v0_ref, v1_ref, v2_ref, bias_ref, o_ref):
    i = pl.program_id(2)
    tq = q_ref.shape[0]
    q = q_ref[...]
    lo = lax.broadcasted_iota(I32, (tq, LANES), 1) < HEAD_DIM
    kw = jnp.concatenate([k0_ref[...], k1_ref[...], k2_ref[...]], axis=0)
    vw = jnp.concatenate([v0_ref[...], v1_ref[...], v2_ref[...]], axis=0)
    kj = lax.broadcasted_iota(I32, (tq, 3 * tq), 1)
    kvalid = kj >= (2 - i) * tq
    outs = []
    for hh, qm in enumerate(_head_halves(q)):
        s = _dot_nt(qm, kw) + bias_ref[hh]
        s = jnp.where(kvalid, s, NEG_MASK)
        m = jnp.max(s, axis=-1, keepdims=True)
        p = jnp.exp(s - m)
        l = jnp.sum(p, axis=-1, keepdims=True)
        outs.append(_dot(p.astype(BF16), vw) / l)
    o_ref[...] = jnp.where(lo, outs[0], outs[1]).astype(o_ref.dtype)


def _band_bias(rel_bias, tq):
    back = LEFT_CHUNKS * CHUNK
    assert back == 2 * tq
    width = 4 * tq
    dist = back + tq - 1 - np.arange(width - 1)
    g = rel_bias[:, np.clip(dist, -(CHUNK - 1), REL_MAX) + (CHUNK - 1)].astype(F32)
    g = jnp.concatenate([g, jnp.zeros((g.shape[0], 1), F32)], axis=1)
    g = jnp.roll(g, -(tq - 1), axis=1)
    flat = jnp.tile(g, (1, tq))[:, :tq * (width - 1)]
    bias = flat.reshape(-1, tq, width - 1)[:, :, :3 * tq]
    qi = np.arange(tq)[:, None] + back
    kj = np.arange(3 * tq)[None, :]
    dc = qi // CHUNK - kj // CHUNK
    valid = (dc >= 0) & (dc <= LEFT_CHUNKS)
    return jnp.where(jnp.asarray(valid)[None], bias, NEG_MASK)


def _band_attention(proj3, bias):
    b, s, _ = proj3.shape
    tq = TQ
    nq = s // tq

    def kspec(base, off):
        return pl.BlockSpec((None, tq, LANES),
                            lambda hp, bb, i: (bb, jnp.maximum(i + off, 0), base + hp))

    return pl.pallas_call(
        _band_kernel,
        out_shape=jax.ShapeDtypeStruct((b, s, H_A * HEAD_DIM), BF16),
        grid=(H_A // 2, b, nq),
        in_specs=[
            pl.BlockSpec((None, tq, LANES), lambda hp, bb, i: (bb, i, CB_QA + hp)),
            kspec(CB_KA, -2), kspec(CB_KA, -1), kspec(CB_KA, 0),
            kspec(CB_VA, -2), kspec(CB_VA, -1), kspec(CB_VA, 0),
            pl.BlockSpec((2, tq, 3 * tq), lambda hp, bb, i: (hp, 0, 0)),
        ],
        out_specs=pl.BlockSpec((None, tq, LANES), lambda hp, bb, i: (bb, i, hp)),
        compiler_params=_params(("arbitrary", "arbitrary", "arbitrary"), VMEM_LIMIT),
        name="band_attn",
    )(proj3, proj3, proj3, proj3, proj3, proj3, proj3, bias)


def _diff_kernel(q_ref, k_ref, vt_ref, lv_ref, sub_ref, o_ref, acc_ref, *, lam_init):
    i = pl.program_id(2)
    tq = q_ref.shape[0]
    tk = vt_ref.shape[2]
    qms = _head_halves(q_ref[...])
    acc_ref[...] = jnp.zeros(acc_ref.shape, F32)
    ck = lax.broadcasted_iota(I32, (tk, tq), 0) // CHUNK
    cq = lax.broadcasted_iota(I32, (tk, tq), 1) // CHUNK
    diag_ok = ck <= cq

    def scores(j, c, mask):
        start = pl.multiple_of(j * tk, tk)
        s = _dot_nt(k_ref[pl.ds(start, tk), :], qms[c])
        return s if mask is None else jnp.where(mask, s, NEG_MASK)

    def fold(x):
        return x.reshape(tk // SUBLANES, SUBLANES, tq)

    def max_block(j, ms, mask):
        return tuple(jnp.maximum(ms[c], jnp.max(fold(scores(j, c, mask)), axis=0))
                     for c in range(2))

    m0 = jnp.full((SUBLANES, tq), NEG_INIT, F32)
    ms = lax.fori_loop(0, i, lambda j, ms: max_block(j, ms, None), (m0, m0))
    ms = max_block(i, ms, diag_ok)
    ms = tuple(jnp.max(m, axis=0, keepdims=True) for m in ms)

    def pv_block(j, ls, mask):
        vt = vt_ref[j]
        new = []
        for c in range(2):
            p = jnp.exp(scores(j, c, mask) - ms[c])
            new.append(ls[c] + jnp.sum(fold(p), axis=0))
            acc_ref[c] += _dot(vt, p.astype(BF16))
        return tuple(new)

    l0 = jnp.zeros((SUBLANES, tq), F32)
    ls = lax.fori_loop(0, i, lambda j, ls: pv_block(j, ls, None), (l0, l0))
    ls = pv_block(i, ls, diag_ok)
    l_a, l_b = (jnp.sum(l, axis=0, keepdims=True) for l in ls)

    lv = lv_ref[...]
    lam = (jnp.exp(jnp.sum(lv[0:1] * lv[1:2], axis=-1, keepdims=True))
           - jnp.exp(jnp.sum(lv[2:3] * lv[3:4], axis=-1, keepdims=True)) + lam_init)
    o = acc_ref[0] / l_a - lam * (acc_ref[1] / l_b)
    ms = jnp.mean(o * o, axis=0, keepdims=True)
    o = (o * lax.rsqrt(ms + EPS)) * sub_ref[...] * (1.0 - lam_init)
    o_ref[...] = o.T.astype(o_ref.dtype)


def _diff_attention(proj3, vbt, lambda_vec, subln_col, lam_init):
    b, s, _ = proj3.shape
    tq = TQ_B
    nkb = s // tq
    kern = functools.partial(_diff_kernel, lam_init=lam_init)
    return pl.pallas_call(
        kern,
        out_shape=jax.ShapeDtypeStruct((b, s, H_B * 2 * HEAD_DIM), BF16),
        grid=(b, H_B, s // tq),
        in_specs=[
            pl.BlockSpec((None, tq, LANES), lambda bb, h, i: (bb, i, CB_QB + h)),
            pl.BlockSpec((None, s, LANES), lambda bb, h, i: (bb, 0, CB_KB + h)),
            pl.BlockSpec((nkb, LANES, tq), lambda bb, h, i: (bb, h, 0)),
            pl.BlockSpec((4, HEAD_DIM), lambda bb, h, i: (0, 0)),
            pl.BlockSpec((LANES, 1), lambda bb, h, i: (0, 0)),
        ],
        out_specs=pl.BlockSpec((None, tq, LANES), lambda bb, h, i: (bb, i, h)),
        scratch_shapes=[pltpu.VMEM((2, LANES, tq), F32)],
        compiler_params=_params(("arbitrary", "arbitrary", "arbitrary"), VMEM_LIMIT),
        name="diff_attn",
    )(proj3, proj3, vbt, lambda_vec, subln_col)


def _sparse_kernel(q_ref, k_ref, vt_ref, qi_ref, ki_ref, wi_ref, o_ref,
                   keys_ref, acc_ref, *, topk):
    i = pl.program_id(1)
    tq = q_ref.shape[0]
    tk = vt_ref.shape[2]

    wt = wi_ref[...].astype(F32).T * (H_IDX ** -0.5)
    w_rows = [wt[h:h + 1, :] for h in range(H_IDX)]
    qi = qi_ref[...]
    qi_heads = (_head_halves(qi[:, :LANES]) + _head_halves(qi[:, LANES:]))
    ck = lax.broadcasted_iota(I32, (tk, tq), 0) // CHUNK
    cq = lax.broadcasted_iota(I32, (tk, tq), 1) // CHUNK
    diag_ok = ck <= cq

    def score_block(j, admissible):
        start = pl.multiple_of(j * tk, tk)
        kk = ki_ref[pl.ds(start, tk), :]
        sc = jnp.zeros((tk, tq), F32)
        for h in range(H_IDX):
            sc = sc + jnp.maximum(_dot_nt(kk, qi_heads[h]), 0.0) * w_rows[h]
        bits = pltpu.bitcast(sc, I32)
        key = bits ^ ((bits >> 31) & 0x7FFFFFFF)
        key = jnp.where(sc == 0.0, 0, key)
        if admissible is not None:
            key = jnp.where(admissible, key, INT_MIN)
        keys_ref[j] = key

    def score_body(j, carry):
        score_block(j, None)
        return carry

    lax.fori_loop(0, i, score_body, 0)
    score_block(i, diag_ok)

    def count(pred_fn):
        def inner(j, acc):
            blk = keys_ref[j].reshape(tk // SUBLANES, SUBLANES, tq)
            return acc + jnp.sum(jnp.where(pred_fn(blk), 1.0, 0.0), axis=0)
        acc = lax.fori_loop(0, i + 1, inner, jnp.zeros((SUBLANES, tq), F32))
        return jnp.sum(acc, axis=0, keepdims=True)

    def radix_body(it, prefix):
        bit = lax.shift_left(jnp.int32(1), 31 - it)
        cand_u = prefix | bit
        cand = cand_u ^ INT_MIN
        cnt = count(lambda blk: blk >= cand)
        return jnp.where(cnt >= topk, cand_u, prefix)

    prefix = lax.fori_loop(0, 32, radix_body, jnp.zeros((1, tq), I32))
    thr = prefix ^ INT_MIN
    n_gt = count(lambda blk: blk > thr)
    need = jnp.where(prefix == 0, 0.0, topk - n_gt)

    acc_ref[...] = jnp.zeros(acc_ref.shape, F32)
    q = q_ref[...]
    q_heads = ()
    for hp in range(H_C // 2):
        q_heads += _head_halves(q[:, hp * LANES:(hp + 1) * LANES])
    earlier = (lax.broadcasted_iota(I32, (tk, tk), 0)
               > lax.broadcasted_iota(I32, (tk, tk), 1))
    earlier = jnp.where(earlier, 1.0, 0.0).astype(BF16)

    def fold(x):
        return x.reshape(tk // SUBLANES, SUBLANES, tq)

    def scores(j, h, bias):
        start = pl.multiple_of(j * tk, tk)
        hp = h // 2
        kb = k_ref[pl.ds(start, tk), hp * LANES:(hp + 1) * LANES]
        return _dot_nt(kb, q_heads[h]) + bias

    def max_body(j, carry):
        tie_seen, ms = carry
        kblk = keys_ref[j]
        tie = kblk == thr
        tie_f = jnp.where(tie, 1.0, 0.0)
        rank = _dot(earlier, tie_f.astype(BF16)) + tie_seen
        sel = (kblk > thr) | (tie & (rank < need))
        bias = jnp.where(sel, 0.0, NEG_MASK)
        keys_ref[j] = pltpu.bitcast(bias, I32)
        ms = tuple(jnp.maximum(ms[h], jnp.max(fold(scores(j, h, bias)), axis=0))
                   for h in range(H_C))
        return tie_seen + jnp.sum(tie_f, axis=0, keepdims=True), ms

    m0 = jnp.full((SUBLANES, tq), NEG_INIT, F32)
    _, ms = lax.fori_loop(0, i + 1, max_body, (jnp.zeros((1, tq), F32), (m0,) * H_C))
    ms = tuple(jnp.max(m, axis=0, keepdims=True) for m in ms)

    def pv_body(j, ls):
        bias = pltpu.bitcast(keys_ref[j], F32)
        new = []
        for h in range(H_C):
            rows = slice(h * HEAD_DIM, (h + 1) * HEAD_DIM)
            p = jnp.exp(scores(j, h, bias) - ms[h])
            new.append(ls[h] + jnp.sum(fold(p), axis=0))
            acc_ref[rows, :] += _dot(vt_ref[j, rows, :], p.astype(BF16))
        return tuple(new)

    ls = lax.fori_loop(0, i + 1, pv_body, (jnp.zeros((SUBLANES, tq), F32),) * H_C)

    for h in range(H_C):
        rows = slice(h * HEAD_DIM, (h + 1) * HEAD_DIM)
        acc_ref[rows, :] = acc_ref[rows, :] / jnp.sum(ls[h], axis=0, keepdims=True)
    o_ref[...] = acc_ref[...].T.astype(o_ref.dtype)


def _sparse_attention(proj3, vct, topk):
    b, s, _ = proj3.shape
    tq = TQ
    wc = H_C * HEAD_DIM
    nkb = s // TK
    kern = functools.partial(_sparse_kernel, topk=float(topk))
    return pl.pallas_call(
        kern,
        out_shape=jax.ShapeDtypeStruct((b, s, wc), BF16),
        grid=(b, s // tq),
        in_specs=[
            pl.BlockSpec((None, tq, wc), lambda bb, i: (bb, i, CB_QC * LANES // wc)),
            pl.BlockSpec((None, s, wc), lambda bb, i: (bb, 0, CB_KC * LANES // wc)),
            pl.BlockSpec((nkb, wc, TK), lambda bb, i: (bb, 0, 0)),
            pl.BlockSpec((None, tq, 2 * LANES), lambda bb, i: (bb, i, CB_QI // 2)),
            pl.BlockSpec((None, s, LANES), lambda bb, i: (bb, 0, CB_KI)),
            pl.BlockSpec((None, tq, LANES), lambda bb, i: (bb, i, CB_WI)),
        ],
        out_specs=pl.BlockSpec((None, tq, wc), lambda bb, i: (bb, i, 0)),
        scratch_shapes=[
            pltpu.VMEM((nkb, TK, tq), I32),
            pltpu.VMEM((wc, tq), F32),
        ],
        compiler_params=_params(("arbitrary", "arbitrary"), VMEM_LIMIT),
        name="sparse_attn",
    )(proj3, proj3, vct, proj3, proj3, proj3)


def _mix_kernel(x_ref, oa_ref, ob_ref, oc_ref, ga_ref, gb_ref, gc_ref,
                ua_ref, ub_ref, uc_ref, wo_ref, o_ref):
    y = ga_ref[...].astype(F32) * _dot(oa_ref[...], ua_ref[...])
    y = y + gb_ref[...].astype(F32) * _dot(ob_ref[...], ub_ref[...])
    y = y + gc_ref[...].astype(F32) * _dot(oc_ref[...], uc_ref[...])
    o_ref[...] = x_ref[...] + _dot(y.astype(BF16), wo_ref[...])


def _mix_out(x2, oa, ob, oc, proj2, ua, ub, uc, wo):
    t, d = x2.shape
    tm = 512
    gate_blk = CB_GATES * LANES // d
    w = oa.shape[1]

    def full(shape):
        return pl.BlockSpec(shape, lambda i: (0, 0))

    return pl.pallas_call(
        _mix_kernel,
        out_shape=jax.ShapeDtypeStruct((t, d), F32),
        grid=(t // tm,),
        in_specs=[
            pl.BlockSpec((tm, d), lambda i: (i, 0)),
            pl.BlockSpec((tm, w), lambda i: (i, 0)),
            pl.BlockSpec((tm, w), lambda i: (i, 0)),
            pl.BlockSpec((tm, w), lambda i: (i, 0)),
            pl.BlockSpec((tm, d), lambda i: (i, gate_blk)),
            pl.BlockSpec((tm, d), lambda i: (i, gate_blk + 1)),
            pl.BlockSpec((tm, d), lambda i: (i, gate_blk + 2)),
            full((w, d)), full((w, d)), full((w, d)), full((d, d)),
        ],
        out_specs=pl.BlockSpec((tm, d), lambda i: (i, 0)),
        compiler_params=_params(("arbitrary",), VMEM_LIMIT),
        name="mix_out",
    )(x2, oa, ob, oc, proj2, proj2, proj2, ua, ub, uc, wo)


def _mem_kv_kernel(mem_ref, g_ref, w_ref, o_ref):
    o_ref[...] = _dot(_rms(mem_ref[...], g_ref[...]).astype(BF16), w_ref[...]).astype(o_ref.dtype)


def _mem_kv(mem2, mem_norm, w_kv):
    r, d = mem2.shape
    depth, _, n = w_kv.shape
    tm = min(1024, r)
    return pl.pallas_call(
        _mem_kv_kernel,
        out_shape=jax.ShapeDtypeStruct((depth, r, n), BF16),
        grid=(depth, r // tm),
        in_specs=[
            pl.BlockSpec((tm, d), lambda l, i: (i, 0)),
            pl.BlockSpec((1, d), lambda l, i: (0, 0)),
            pl.BlockSpec((None, d, n), lambda l, i: (l, 0, 0)),
        ],
        out_specs=pl.BlockSpec((None, tm, n), lambda l, i: (l, i, 0)),
        compiler_params=_params(("arbitrary", "arbitrary"), VMEM_LIMIT),
        name="mem_kv",
    )(mem2, mem_norm, w_kv)


def _cross_kernel(x_ref, g_ref, wq_ref, kv_ref, wo_ref, o_ref):
    x = x_ref[...]
    tm = x.shape[0]
    hn = _rms(x, g_ref[...]).astype(BF16)
    q = _dot(hn, wq_ref[...]).astype(BF16)
    kv = kv_ref[...]
    wkv = H_X * HEAD_DIM
    lo = lax.broadcasted_iota(I32, (tm, LANES), 1) < HEAD_DIM
    pairs = []
    for hp in range(H_X // 2):
        kp = kv[:, hp * LANES:(hp + 1) * LANES]
        vp = kv[:, wkv + hp * LANES:wkv + (hp + 1) * LANES]
        outs = []
        for qm in _head_halves(q[:, hp * LANES:(hp + 1) * LANES]):
            s = _dot_nt(qm, kp)
            m = jnp.max(s, axis=-1, keepdims=True)
            p = jnp.exp(s - m)
            l = jnp.sum(p, axis=-1, keepdims=True)
            outs.append(_dot(p.astype(BF16), vp) / l)
        pairs.append(jnp.where(lo, outs[0], outs[1]).astype(BF16))
    o = jnp.concatenate(pairs, axis=1)
    o_ref[...] = x + _dot(o, wo_ref[...])


def _cross_attention(x2, gain, wq, kv, wo, seq):
    t, d = x2.shape
    tm = min(512, seq)
    per_batch = seq // tm
    n_mem, wkv2 = kv.shape[1], kv.shape[2]
    wq_cols = wq.shape[1]
    return pl.pallas_call(
        _cross_kernel,
        out_shape=jax.ShapeDtypeStruct((t, d), F32),
        grid=(t // tm,),
        in_specs=[
            pl.BlockSpec((tm, d), lambda i: (i, 0)),
            pl.BlockSpec((1, d), lambda i: (0, 0)),
            pl.BlockSpec((d, wq_cols), lambda i: (0, 0)),
            pl.BlockSpec((None, n_mem, wkv2), lambda i: (i // per_batch, 0, 0)),
            pl.BlockSpec((wq_cols, d), lambda i: (0, 0)),
        ],
        out_specs=pl.BlockSpec((tm, d), lambda i: (i, 0)),
        compiler_params=_params(("arbitrary",), VMEM_LIMIT),
        name="cross_attn",
    )(x2, gain, wq, kv, wo)


def _ffn_kernel(x_ref, g_ref, wg_ref, wu_ref, wd_ref, fg_ref, o_ref, hn_ref, acc_ref, *,
                final_norm):
    c = pl.program_id(1)

    @pl.when(c == 0)
    def _():
        hn_ref[...] = _rms(x_ref[...], g_ref[...]).astype(BF16)
        acc_ref[...] = jnp.zeros(acc_ref.shape, F32)

    hn = hn_ref[...]
    gate = _dot(hn, wg_ref[...])
    up = _dot(hn, wu_ref[...])
    h = (gate / (1.0 + jnp.exp(-gate))) * up
    acc_ref[...] += _dot(h.astype(BF16), wd_ref[...])

    @pl.when(c == pl.num_programs(1) - 1)
    def _():
        y = x_ref[...] + acc_ref[...]
        if final_norm:
            y = _rms(y, fg_ref[...])
        o_ref[...] = y


def _ffn(x2, gain, w_gu, w_down, final_gain, final_norm):
    t, d = x2.shape
    d_ff = w_down.shape[0]
    tm = min(1024, t)
    tf = 256
    nf = d_ff // tf
    kern = functools.partial(_ffn_kernel, final_norm=final_norm)
    return pl.pallas_call(
        kern,
        out_shape=jax.ShapeDtypeStruct((t, d), F32),
        grid=(t // tm, nf),
        in_specs=[
            pl.BlockSpec((tm, d), lambda i, c: (i, 0)),
            pl.BlockSpec((1, d), lambda i, c: (0, 0)),
            pl.BlockSpec((d, tf), lambda i, c: (0, c)),
            pl.BlockSpec((d, tf), lambda i, c: (0, c + nf)),
            pl.BlockSpec((tf, d), lambda i, c: (c, 0)),
            pl.BlockSpec((1, d), lambda i, c: (0, 0)),
        ],
        out_specs=pl.BlockSpec((tm, d), lambda i, c: (i, 0)),
        scratch_shapes=[pltpu.VMEM((tm, d), BF16), pltpu.VMEM((tm, d), F32)],
        compiler_params=_params(("arbitrary", "arbitrary"), VMEM_LIMIT),
        name="ffn",
    )(x2, gain, w_gu, w_gu, w_down, final_gain)


def _rotate_half_cols(w):
    d = w.shape[0]
    wr = w.reshape(d, -1, 2, HEAD_DIM // 2)
    return jnp.concatenate([-wr[:, :, 1], wr[:, :, 0]], axis=2).reshape(d, -1)


def _layout_in_weights(w_in_l):
    d = w_in_l.shape[0]
    w_a, w_b, w_c = H_A * HEAD_DIM, H_B * 2 * HEAD_DIM, H_C * HEAD_DIM
    splits = (w_a, w_a, w_a, w_b, w_b, w_b, w_c, w_c, w_c, H_IDX * D_IDX, D_IDX, H_IDX, 3 * d)
    offs = [int(o) for o in np.cumsum(splits)[:-1]]
    qa, ka, va, qb, kb, vb, qc, kc, vc, qi, ki, wi, gates = jnp.split(w_in_l, offs, axis=1)
    scale = HEAD_DIM ** -0.5
    qa, qb, qc, qi = qa * scale, qb * scale, qc * scale, qi * (D_IDX ** -0.5)
    pad = jnp.zeros((d, LANES - H_IDX), w_in_l.dtype)
    tail = jnp.concatenate([qi, ki, ki, wi, pad], axis=1)
    tail_rot = jnp.concatenate(
        [_rotate_half_cols(qi), _rotate_half_cols(ki), _rotate_half_cols(ki),
         jnp.zeros((d, LANES), w_in_l.dtype)], axis=1)
    w_main = jnp.concatenate([qb, kb, qc, kc, tail, qa, ka, va, gates], axis=1)
    w_rot = jnp.concatenate([_rotate_half_cols(qb), _rotate_half_cols(kb),
                             _rotate_half_cols(qc), _rotate_half_cols(kc), tail_rot], axis=1)
    assert w_main.shape[1] == N_PROJ_COLS and w_rot.shape[1] == N_ROPE_COLS
    return w_main.astype(BF16), w_rot.astype(BF16), vb.T.astype(BF16), vc.T.astype(BF16)


def _rope_tables(seq):
    pos = jnp.arange(seq, dtype=F32)
    inv = ROPE_THETA ** (-jnp.arange(0, HEAD_DIM, 2, dtype=F32) / HEAD_DIM)
    ang = pos[:, None] * inv[None, :]
    reps = LANES // (HEAD_DIM // 2)
    return jnp.tile(jnp.cos(ang), (1, reps)), jnp.tile(jnp.sin(ang), (1, reps))


def kernel(x, mem, norm_mix, w_in, rel_bias_a, lambda_vecs, subln_b, w_up_a, w_up_b, w_up_c,
           w_out, norm_cross, w_q_x, w_kv_x, w_o_x, norm_ffn, w_gu, w_down, mem_norm, final_norm):
    b, s, d = x.shape
    depth = w_in.shape[0]
    assert s % TQ_B == 0 and LEFT_CHUNKS * CHUNK == 2 * TQ and TQ == TK
    topk = min(TOPK_MAX, s // 4)
    cos_t, sin_t = _rope_tables(s)
    x2 = x.reshape(b * s, d)
    mem2 = mem.reshape(b * mem.shape[1], d)
    kv_all = _mem_kv(mem2, mem_norm.reshape(1, d), w_kv_x.astype(BF16))
    kv_all = kv_all.reshape(depth, b, mem.shape[1], -1)
    for l in range(depth):
        w_main, w_rot, w_vbt, w_vct = _layout_in_weights(w_in[l])
        proj2, vbt, vct = _in_proj(x2, norm_mix[l].reshape(1, d), w_main, w_rot, cos_t, sin_t,
                                   w_vbt, w_vct, s)
        proj3 = proj2.reshape(b, s, N_PROJ_COLS)
        lam_init = 0.8 - 0.6 * math.exp(-0.3 * l)
        o_a = _band_attention(proj3, _band_bias(rel_bias_a[l], TQ))
        o_b = _diff_attention(proj3, vbt, lambda_vecs[l].astype(F32),
                              subln_b[l].reshape(LANES, 1), lam_init)
        o_c = _sparse_attention(proj3, vct, topk)
        x2 = _mix_out(x2, o_a.reshape(b * s, -1), o_b.reshape(b * s, -1), o_c.reshape(b * s, -1),
                      proj2, w_up_a[l].astype(BF16), w_up_b[l].astype(BF16),
                      w_up_c[l].astype(BF16), w_out[l].astype(BF16))
        x2 = _cross_attention(x2, norm_cross[l].reshape(1, d),
                              (w_q_x[l] * HEAD_DIM ** -0.5).astype(BF16), kv_all[l],
                              w_o_x[l].astype(BF16), s)
        x2 = _ffn(x2, norm_ffn[l].reshape(1, d), w_gu[l].astype(BF16), w_down[l].astype(BF16),
                  final_norm.reshape(1, d), final_norm=(l == depth - 1))
    return x2.reshape(b, s, d)
```

```python
import functools
import math

import jax
import jax.numpy as jnp
import numpy as np
from jax import lax
from jax.experimental import pallas as pl
from jax.experimental.pallas import tpu as pltpu

F32 = jnp.float32
BF16 = jnp.bfloat16
I32 = jnp.int32

CHUNK = 64
HEAD_DIM = 64
H_A = 8
LEFT_CHUNKS = 8
REL_MAX = 256
H_B = 4
H_C = 8
H_IDX = 4
D_IDX = 64
TOPK_MAX = 256
H_X = 4
ROPE_THETA = 10000.0
EPS = 1e-6

LANES = 128
SUBLANES = 8
TQ = 256
TK = 256
TQ_B = 512
SLAB = 128
PROJ_TN = 512
VMEM_LIMIT = 56 * 1024 * 1024

NEG_INIT = -1e30
NEG_MASK = -2e30
INT_MIN = -(2 ** 31)

CB_QB, CB_KB, CB_QC, CB_KC = 0, 4, 8, 12
CB_QI, CB_KI, CB_WI = 16, 18, 19
CB_QA, CB_KA, CB_VA = 20, 24, 28
CB_GATES = 32
N_ROPE_COLS = 2560
N_PROJ_COLS = 7168


def _dot(a, b):
    return jnp.dot(a, b, preferred_element_type=F32)


def _dot_nt(a, b):
    return lax.dot_general(a, b, (((1,), (1,)), ((), ())), preferred_element_type=F32)


def _rms(x, g):
    ms = jnp.mean(x * x, axis=-1, keepdims=True)
    return (x * lax.rsqrt(ms + EPS)) * g


def _params(sem, vmem=None):
    return pltpu.CompilerParams(dimension_semantics=sem, vmem_limit_bytes=vmem)


def _head_halves(pair):
    lo = lax.broadcasted_iota(I32, pair.shape, 1) < HEAD_DIM
    zero = jnp.zeros_like(pair)
    return jnp.where(lo, pair, zero), jnp.where(lo, zero, pair)


def _in_proj_kernel(x_ref, g_ref, w_ref, cos_ref, sina_ref, sinb_ref, wvb_ref, wvc_ref,
                    o_ref, vbt_ref, vct_ref, xn_ref, *, n_rope, gate_start):
    j = pl.program_id(1)

    @pl.when(j == 0)
    def _():
        xn = _rms(x_ref[...], g_ref[...]).astype(BF16)
        xn_ref[...] = xn
        vbt = _dot_nt(wvb_ref[...], xn).astype(BF16)
        for c in range(vbt_ref.shape[0]):
            w = vbt_ref.shape[2]
            vbt_ref[c] = vbt[:, c * w:(c + 1) * w]
        vct = _dot_nt(wvc_ref[...], xn).astype(BF16)
        for c in range(vct_ref.shape[0]):
            w = vct_ref.shape[2]
            vct_ref[c] = vct[:, c * w:(c + 1) * w]

    xn = xn_ref[...]
    tn = o_ref.shape[1]

    @pl.when(j < n_rope)
    def _():
        a = _dot(xn, w_ref[...])
        cos = cos_ref[...]
        sina = sina_ref[...]
        sinb = sinb_ref[...]
        plain_tail = j == n_rope - 1
        half = HEAD_DIM // 2
        for c in range(tn // LANES):
            sl = slice(c * LANES, (c + 1) * LANES)
            cc, sa, sb = cos, sina, sinb
            if c == tn // LANES - 1:
                cc = jnp.where(plain_tail, 1.0, cos)
                sa = jnp.where(plain_tail, 0.0, sina)
                sb = jnp.where(plain_tail, 0.0, sinb)
            ac = a[:, sl]
            rot = pltpu.roll(ac, LANES - half, 1) * sa + pltpu.roll(ac, half, 1) * sb
            o_ref[:, sl] = (ac * cc + rot).astype(o_ref.dtype)

    @pl.when((j >= n_rope) & (j < gate_start))
    def _():
        o_ref[...] = _dot(xn, w_ref[...]).astype(o_ref.dtype)

    @pl.when(j >= gate_start)
    def _():
        a = _dot(xn, w_ref[...])
        o_ref[...] = (1.0 / (1.0 + jnp.exp(-a))).astype(o_ref.dtype)


def _in_proj(x2, gain, w_main, cos_t, sina_t, sinb_t, w_vbt, w_vct, seq):
    t, d = x2.shape
    tm = min(1024, seq)
    tn = PROJ_TN
    n_rope = N_ROPE_COLS // tn
    gate_start = CB_GATES * LANES // tn
    pos_blocks = seq // tm
    wb, wc = w_vbt.shape[0], w_vct.shape[0]
    kern = functools.partial(_in_proj_kernel, n_rope=n_rope, gate_start=gate_start)
    return pl.pallas_call(
        kern,
        out_shape=(
            jax.ShapeDtypeStruct((t, N_PROJ_COLS), BF16),
            jax.ShapeDtypeStruct((t // TQ_B, wb, TQ_B), BF16),
            jax.ShapeDtypeStruct((t // TK, wc, TK), BF16),
        ),
        grid=(t // tm, N_PROJ_COLS // tn),
        in_specs=[
            pl.BlockSpec((tm, d), lambda i, j: (i, 0)),
            pl.BlockSpec((1, d), lambda i, j: (0, 0)),
            pl.BlockSpec((d, tn), lambda i, j: (0, j)),
            pl.BlockSpec((tm, LANES), lambda i, j: (i % pos_blocks, 0)),
            pl.BlockSpec((tm, LANES), lambda i, j: (i % pos_blocks, 0)),
            pl.BlockSpec((tm, LANES), lambda i, j: (i % pos_blocks, 0)),
            pl.BlockSpec((wb, d), lambda i, j: (0, 0)),
            pl.BlockSpec((wc, d), lambda i, j: (0, 0)),
        ],
        out_specs=(
            pl.BlockSpec((tm, tn), lambda i, j: (i, j)),
            pl.BlockSpec((tm // TQ_B, wb, TQ_B), lambda i, j: (i, 0, 0)),
            pl.BlockSpec((tm // TK, wc, TK), lambda i, j: (i, 0, 0)),
        ),
        scratch_shapes=[pltpu.VMEM((tm, d), BF16)],
        compiler_params=_params(("arbitrary", "arbitrary"), VMEM_LIMIT),
        name="in_proj",
    )(x2, gain, w_main, cos_t, sina_t, sinb_t, w_vbt, w_vct)


def _band_kernel(q_ref, k0_ref, k1_ref, k2_ref, v0_ref, v1_ref, v2_ref, bias_ref, o_ref):
    i = pl.program_id(2)
    tq = q_ref.shape[0]
    q = q_ref[...]
    lo = lax.broadcasted_iota(I32, (tq, LANES), 1) < HEAD_DIM
    kw = jnp.concatenate([k0_ref[...], k1_ref[...], k2_ref[...]], axis=0)
    vw = jnp.concatenate([v0_ref[...], v1_ref[...], v2_ref[...]], axis=0)
    kj = lax.broadcasted_iota(I32, (tq, 3 * tq), 1)
    kvalid = kj >= (2 - i) * tq
    outs = []
    for hh, qm in enumerate(_head_halves(q)):
        s = _dot_nt(qm, kw) + bias_ref[hh]
        s = jnp.where(kvalid, s, NEG_MASK)
        m = jnp.max(s, axis=-1, keepdims=True)
        p = jnp.exp(s - m)
        l = jnp.sum(p, axis=-1, keepdims=True)
        outs.append(_dot(p.astype(BF16), vw) / l)
    o_ref[...] = jnp.where(lo, outs[0], outs[1]).astype(o_ref.dtype)


def _band_bias(rel_bias, tq):
    back = LEFT_CHUNKS * CHUNK
    assert back == 2 * tq
    width = 4 * tq
    dist = back + tq - 1 - np.arange(width - 1)
    g = rel_bias[:, np.clip(dist, -(CHUNK - 1), REL_MAX) + (CHUNK - 1)].astype(F32)
    g = jnp.concatenate([g, jnp.zeros((g.shape[0], 1), F32)], axis=1)
    g = jnp.roll(g, -(tq - 1), axis=1)
    flat = jnp.tile(g, (1, tq))[:, :tq * (width - 1)]
    bias = flat.reshape(-1, tq, width - 1)[:, :, :3 * tq]
    qi = np.arange(tq)[:, None] + back
    kj = np.arange(3 * tq)[None, :]
    dc = qi // CHUNK - kj // CHUNK
    valid = (dc >= 0) & (dc <= LEFT_CHUNKS)
    return jnp.where(jnp.asarray(valid)[None], bias, NEG_MASK)


def _band_attention(proj3, bias):
    b, s, _ = proj3.shape
    tq = TQ
    nq = s // tq

    def kspec(base, off):
        return pl.BlockSpec((None, tq, LANES),
                            lambda hp, bb, i: (bb, jnp.maximum(i + off, 0), base + hp))

    return pl.pallas_call(
        _band_kernel,
        out_shape=jax.ShapeDtypeStruct((b, s, H_A * HEAD_DIM), BF16),
        grid=(H_A // 2, b, nq),
        in_specs=[
            pl.BlockSpec((None, tq, LANES), lambda hp, bb, i: (bb, i, CB_QA + hp)),
            kspec(CB_KA, -2), kspec(CB_KA, -1), kspec(CB_KA, 0),
            kspec(CB_VA, -2), kspec(CB_VA, -1), kspec(CB_VA, 0),
            pl.BlockSpec((2, tq, 3 * tq), lambda hp, bb, i: (hp, 0, 0)),
        ],
        out_specs=pl.BlockSpec((None, tq, LANES), lambda hp, bb, i: (bb, i, hp)),
        compiler_params=_params(("arbitrary", "arbitrary", "arbitrary"), VMEM_LIMIT),
        name="band_attn",
    )(proj3, proj3, proj3, proj3, proj3, proj3, proj3, bias)


def _diff_kernel(q_ref, k_ref, vt_ref, lv_ref, sub_ref, o_ref, acc_ref, s_ref, p_ref, *,
                 lam_init):
    i = pl.program_id(1)
    tq = q_ref.shape[0]
    tk = vt_ref.shape[2]
    n_maps = 2 * H_B
    q = q_ref[...]
    qms = ()
    for h in range(H_B):
        qms += _head_halves(q[:, h * LANES:(h + 1) * LANES])
    acc_ref[...] = jnp.zeros(acc_ref.shape, F32)
    ck = lax.broadcasted_iota(I32, (tk, tq), 0) // CHUNK
    cq = lax.broadcasted_iota(I32, (tk, tq), 1) // CHUNK
    diag_ok = ck <= cq

    def fold(x):
        return x.reshape(x.shape[0] // SUBLANES, SUBLANES, tq)

    def block(j, stats, mask):
        start = pl.multiple_of(j * tk, tk)
        mxs = []
        for c in range(n_maps):
            h = c // 2
            s = _dot_nt(k_ref[pl.ds(start, tk), h * LANES:(h + 1) * LANES], qms[c])
            if mask is not None:
                s = jnp.where(mask, s, NEG_MASK)
            s_ref[c] = s
            mxs.append(jnp.max(fold(s), axis=0))
        ms, ls = stats
        new_m, new_l, alphas = [], [], []
        for c in range(n_maps):
            m_new = jnp.maximum(ms[c], jnp.max(mxs[c], axis=0, keepdims=True))
            lsum = jnp.zeros((SUBLANES, tq), F32)
            for r in range(tk // SLAB):
                p = jnp.exp2(s_ref[c, r * SLAB:(r + 1) * SLAB, :] - m_new)
                lsum = lsum + jnp.sum(fold(p), axis=0)
                p_ref[c, r * SLAB:(r + 1) * SLAB, :] = p.astype(BF16)
            alpha = jnp.exp2(ms[c] - m_new)
            new_m.append(m_new)
            new_l.append(alpha * ls[c] + jnp.sum(lsum, axis=0, keepdims=True))
            alphas.append(alpha)
        for c in range(n_maps):
            h = c // 2
            vt = vt_ref[j, h * LANES:(h + 1) * LANES, :]
            acc_ref[c] = alphas[c] * acc_ref[c] + _dot(vt, p_ref[c])
        return tuple(new_m), tuple(new_l)

    m0 = jnp.full((1, tq), NEG_INIT, F32)
    l0 = jnp.zeros((1, tq), F32)
    stats = lax.fori_loop(0, i, lambda j, st: block(j, st, None),
                          ((m0,) * n_maps, (l0,) * n_maps))
    _, ls = block(i, stats, diag_ok)

    lv = lv_ref[...]
    lam = (jnp.exp(jnp.sum(lv[0:1] * lv[1:2], axis=-1, keepdims=True))
           - jnp.exp(jnp.sum(lv[2:3] * lv[3:4], axis=-1, keepdims=True)) + lam_init)
    for h in range(H_B):
        o = acc_ref[2 * h] / ls[2 * h] - lam * (acc_ref[2 * h + 1] / ls[2 * h + 1])
        ms = jnp.mean(o * o, axis=0, keepdims=True)
        o = (o * lax.rsqrt(ms + EPS)) * sub_ref[...] * (1.0 - lam_init)
        o_ref[:, h * LANES:(h + 1) * LANES] = o.T.astype(o_ref.dtype)


def _diff_attention(proj3, vbt, lambda_vec, subln_col, lam_init):
    b, s, _ = proj3.shape
    tq = TQ_B
    nkb = s // tq
    wb = H_B * 2 * HEAD_DIM
    kern = functools.partial(_diff_kernel, lam_init=lam_init)
    return pl.pallas_call(
        kern,
        out_shape=jax.ShapeDtypeStruct((b, s, wb), BF16),
        grid=(b, s // tq),
        in_specs=[
            pl.BlockSpec((None, tq, wb), lambda bb, i: (bb, i, CB_QB * LANES // wb)),
            pl.BlockSpec((None, s, wb), lambda bb, i: (bb, 0, CB_KB * LANES // wb)),
            pl.BlockSpec((nkb, wb, tq), lambda bb, i: (bb, 0, 0)),
            pl.BlockSpec((4, HEAD_DIM), lambda bb, i: (0, 0)),
            pl.BlockSpec((LANES, 1), lambda bb, i: (0, 0)),
        ],
        out_specs=pl.BlockSpec((None, tq, wb), lambda bb, i: (bb, i, 0)),
        scratch_shapes=[
            pltpu.VMEM((2 * H_B, LANES, tq), F32),
            pltpu.VMEM((2 * H_B, tq, tq), F32),
            pltpu.VMEM((2 * H_B, tq, tq), BF16),
        ],
        compiler_params=_params(("arbitrary", "arbitrary"), VMEM_LIMIT),
        name="diff_attn",
    )(proj3, proj3, vbt, lambda_vec, subln_col)


def _sparse_kernel(q_ref, k_ref, vt_ref, qi_ref, ki_ref, wi_ref, o_ref,
                   keys_ref, acc_ref, s_ref, p_ref, b_ref, *, topk):
    i = pl.program_id(1)
    tq = q_ref.shape[0]
    tk = vt_ref.shape[2]

    wt = wi_ref[...].astype(F32).T * (H_IDX ** -0.5)
    w_rows = [wt[h:h + 1, :] for h in range(H_IDX)]
    qi = qi_ref[...]
    qi_heads = (_head_halves(qi[:, :LANES]) + _head_halves(qi[:, LANES:]))
    ck = lax.broadcasted_iota(I32, (tk, tq), 0) // CHUNK
    cq = lax.broadcasted_iota(I32, (tk, tq), 1) // CHUNK
    diag_ok = ck <= cq

    def score_block(j, admissible):
        start = pl.multiple_of(j * tk, tk)
        kk = ki_ref[pl.ds(start, tk), :]
        sc = jnp.zeros((tk, tq), F32)
        for h in range(H_IDX):
            sc = sc + jnp.maximum(_dot_nt(kk, qi_heads[h]), 0.0) * w_rows[h]
        bits = pltpu.bitcast(sc, I32)
        key = bits ^ ((bits >> 31) & 0x7FFFFFFF)
        key = jnp.where(sc == 0.0, 0, key)
        if admissible is not None:
            key = jnp.where(admissible, key, INT_MIN)
        keys_ref[j] = key

    def score_body(j, carry):
        score_block(j, None)
        return carry

    lax.fori_loop(0, i, score_body, 0)
    score_block(i, diag_ok)

    def count(pred_fn):
        def inner(j, acc):
            blk = keys_ref[j].reshape(tk // SUBLANES, SUBLANES, tq)
            return acc + jnp.sum(jnp.where(pred_fn(blk), 1.0, 0.0), axis=0)
        acc = lax.fori_loop(0, i + 1, inner, jnp.zeros((SUBLANES, tq), F32))
        return jnp.sum(acc, axis=0, keepdims=True)

    def radix_body(it, prefix):
        bit = lax.shift_left(jnp.int32(1), 31 - it)
        cand_u = prefix | bit
        cand = cand_u ^ INT_MIN
        cnt = count(lambda blk: blk >= cand)
        return jnp.where(cnt >= topk, cand_u, prefix)

    prefix = lax.fori_loop(0, 32, radix_body, jnp.zeros((1, tq), I32))
    thr = prefix ^ INT_MIN
    n_gt = count(lambda blk: blk > thr)
    need = jnp.where(prefix == 0, 0.0, topk - n_gt)

    acc_ref[...] = jnp.zeros(acc_ref.shape, F32)
    q = q_ref[...]
    q_heads = ()
    for hp in range(H_C // 2):
        q_heads += _head_halves(q[:, hp * LANES:(hp + 1) * LANES])
    earlier = (lax.broadcasted_iota(I32, (tk, tk), 0)
               > lax.broadcasted_iota(I32, (tk, tk), 1))
    earlier = jnp.where(earlier, 1.0, 0.0).astype(BF16)

    def fold(x):
        return x.reshape(x.shape[0] // SUBLANES, SUBLANES, tq)

    def attn_body(j, carry):
        tie_seen, ms, ls = carry
        start = pl.multiple_of(j * tk, tk)
        kblk = keys_ref[j]
        tie = kblk == thr
        tie_f = jnp.where(tie, 1.0, 0.0)
        rank = _dot(earlier, tie_f.astype(BF16)) + tie_seen
        sel = (kblk > thr) | (tie & (rank < need))
        b_ref[...] = jnp.where(sel, 0.0, NEG_MASK)
        mxs = []
        for h in range(H_C):
            hp = h // 2
            kb = k_ref[pl.ds(start, tk), hp * LANES:(hp + 1) * LANES]
            s = _dot_nt(kb, q_heads[h]) + b_ref[...]
            s_ref[h] = s
            mxs.append(jnp.max(fold(s), axis=0))
        new_m, new_l, alphas = [], [], []
        for h in range(H_C):
            m_new = jnp.maximum(ms[h], jnp.max(mxs[h], axis=0, keepdims=True))
            lsum = jnp.zeros((SUBLANES, tq), F32)
            for r in range(tk // SLAB):
                p = jnp.exp2(s_ref[h, r * SLAB:(r + 1) * SLAB, :] - m_new)
                lsum = lsum + jnp.sum(fold(p), axis=0)
                p_ref[h, r * SLAB:(r + 1) * SLAB, :] = p.astype(BF16)
            alpha = jnp.exp2(ms[h] - m_new)
            new_m.append(m_new)
            new_l.append(alpha * ls[h] + jnp.sum(lsum, axis=0, keepdims=True))
            alphas.append(alpha)
        for h in range(H_C):
            rows = slice(h * HEAD_DIM, (h + 1) * HEAD_DIM)
            acc_ref[rows, :] = alphas[h] * acc_ref[rows, :] + _dot(vt_ref[j, rows, :], p_ref[h])
        return (tie_seen + jnp.sum(tie_f, axis=0, keepdims=True), tuple(new_m), tuple(new_l))

    m0 = jnp.full((1, tq), NEG_INIT, F32)
    l0 = jnp.zeros((1, tq), F32)
    _, _, ls = lax.fori_loop(0, i + 1, attn_body, (l0, (m0,) * H_C, (l0,) * H_C))

    for h in range(H_C):
        rows = slice(h * HEAD_DIM, (h + 1) * HEAD_DIM)
        acc_ref[rows, :] = acc_ref[rows, :] / ls[h]
    o_ref[...] = acc_ref[...].T.astype(o_ref.dtype)


def _sparse_attention(proj3, vct, topk):
    b, s, _ = proj3.shape
    tq = TQ
    wc = H_C * HEAD_DIM
    nkb = s // TK
    kern = functools.partial(_sparse_kernel, topk=float(topk))
    return pl.pallas_call(
        kern,
        out_shape=jax.ShapeDtypeStruct((b, s, wc), BF16),
        grid=(b, s // tq),
        in_specs=[
            pl.BlockSpec((None, tq, wc), lambda bb, i: (bb, i, CB_QC * LANES // wc)),
            pl.BlockSpec((None, s, wc), lambda bb, i: (bb, 0, CB_KC * LANES // wc)),
            pl.BlockSpec((nkb, wc, TK), lambda bb, i: (bb, 0, 0)),
            pl.BlockSpec((None, tq, 2 * LANES), lambda bb, i: (bb, i, CB_QI // 2)),
            pl.BlockSpec((None, s, LANES), lambda bb, i: (bb, 0, CB_KI)),
            pl.BlockSpec((None, tq, LANES), lambda bb, i: (bb, i, CB_WI)),
        ],
        out_specs=pl.BlockSpec((None, tq, wc), lambda bb, i: (bb, i, 0)),
        scratch_shapes=[
            pltpu.VMEM((nkb, TK, tq), I32),
            pltpu.VMEM((wc, tq), F32),
            pltpu.VMEM((H_C, TK, tq), F32),
            pltpu.VMEM((H_C, TK, tq), BF16),
            pltpu.VMEM((TK, tq), F32),
        ],
        compiler_params=_params(("arbitrary", "arbitrary"), VMEM_LIMIT),
        name="sparse_attn",
    )(proj3, proj3, vct, proj3, proj3, proj3)


def _mix_kernel(x_ref, oa_ref, ob_ref, oc_ref, ga_ref, gb_ref, gc_ref,
                ua_ref, ub_ref, uc_ref, wo_ref, o_ref):
    y = ga_ref[...].astype(F32) * _dot(oa_ref[...], ua_ref[...])
    y = y + gb_ref[...].astype(F32) * _dot(ob_ref[...], ub_ref[...])
    y = y + gc_ref[...].astype(F32) * _dot(oc_ref[...], uc_ref[...])
    o_ref[...] = x_ref[...] + _dot(y.astype(BF16), wo_ref[...])


def _mix_out(x2, oa, ob, oc, proj2, ua, ub, uc, wo):
    t, d = x2.shape
    tm = 512
    gate_blk = CB_GATES * LANES // d
    w = oa.shape[1]

    def full(shape):
        return pl.BlockSpec(shape, lambda i: (0, 0))

    return pl.pallas_call(
        _mix_kernel,
        out_shape=jax.ShapeDtypeStruct((t, d), F32),
        grid=(t // tm,),
        in_specs=[
            pl.BlockSpec((tm, d), lambda i: (i, 0)),
            pl.BlockSpec((tm, w), lambda i: (i, 0)),
            pl.BlockSpec((tm, w), lambda i: (i, 0)),
            pl.BlockSpec((tm, w), lambda i: (i, 0)),
            pl.BlockSpec((tm, d), lambda i: (i, gate_blk)),
            pl.BlockSpec((tm, d), lambda i: (i, gate_blk + 1)),
            pl.BlockSpec((tm, d), lambda i: (i, gate_blk + 2)),
            full((w, d)), full((w, d)), full((w, d)), full((d, d)),
        ],
        out_specs=pl.BlockSpec((tm, d), lambda i: (i, 0)),
        compiler_params=_params(("arbitrary",), VMEM_LIMIT),
        name="mix_out",
    )(x2, oa, ob, oc, proj2, proj2, proj2, ua, ub, uc, wo)


def _mem_kv_kernel(mem_ref, g_ref, w_ref, o_ref):
    o_ref[...] = _dot(_rms(mem_ref[...], g_ref[...]).astype(BF16), w_ref[...]).astype(o_ref.dtype)


def _mem_kv(mem2, mem_norm, w_kv):
    r, d = mem2.shape
    depth, _, n = w_kv.shape
    tm = min(1024, r)
    return pl.pallas_call(
        _mem_kv_kernel,
        out_shape=jax.ShapeDtypeStruct((depth, r, n), BF16),
        grid=(depth, r // tm),
        in_specs=[
            pl.BlockSpec((tm, d), lambda l, i: (i, 0)),
            pl.BlockSpec((1, d), lambda l, i: (0, 0)),
            pl.BlockSpec((None, d, n), lambda l, i: (l, 0, 0)),
        ],
        out_specs=pl.BlockSpec((None, tm, n), lambda l, i: (l, i, 0)),
        compiler_params=_params(("arbitrary", "arbitrary"), VMEM_LIMIT),
        name="mem_kv",
    )(mem2, mem_norm, w_kv)


def _cross_kernel(x_ref, g_ref, wq_ref, kv_ref, wo_ref, o_ref):
    x = x_ref[...]
    tm = x.shape[0]
    hn = _rms(x, g_ref[...]).astype(BF16)
    q = _dot(hn, wq_ref[...]).astype(BF16)
    kv = kv_ref[...]
    wkv = H_X * HEAD_DIM
    lo = lax.broadcasted_iota(I32, (tm, LANES), 1) < HEAD_DIM
    pairs = []
    for hp in range(H_X // 2):
        kp = kv[:, hp * LANES:(hp + 1) * LANES]
        vp = kv[:, wkv + hp * LANES:wkv + (hp + 1) * LANES]
        outs = []
        for qm in _head_halves(q[:, hp * LANES:(hp + 1) * LANES]):
            s = _dot_nt(qm, kp)
            m = jnp.max(s, axis=-1, keepdims=True)
            p = jnp.exp(s - m)
            l = jnp.sum(p, axis=-1, keepdims=True)
            outs.append(_dot(p.astype(BF16), vp) / l)
        pairs.append(jnp.where(lo, outs[0], outs[1]).astype(BF16))
    o = jnp.concatenate(pairs, axis=1)
    o_ref[...] = x + _dot(o, wo_ref[...])


def _cross_attention(x2, gain, wq, kv, wo, seq):
    t, d = x2.shape
    tm = min(512, seq)
    per_batch = seq // tm
    n_mem, wkv2 = kv.shape[1], kv.shape[2]
    wq_cols = wq.shape[1]
    return pl.pallas_call(
        _cross_kernel,
        out_shape=jax.ShapeDtypeStruct((t, d), F32),
        grid=(t // tm,),
        in_specs=[
            pl.BlockSpec((tm, d), lambda i: (i, 0)),
            pl.BlockSpec((1, d), lambda i: (0, 0)),
            pl.BlockSpec((d, wq_cols), lambda i: (0, 0)),
            pl.BlockSpec((None, n_mem, wkv2), lambda i: (i // per_batch, 0, 0)),
            pl.BlockSpec((wq_cols, d), lambda i: (0, 0)),
        ],
        out_specs=pl.BlockSpec((tm, d), lambda i: (i, 0)),
        compiler_params=_params(("arbitrary",), VMEM_LIMIT),
        name="cross_attn",
    )(x2, gain, wq, kv, wo)


def _ffn_kernel(x_ref, g_ref, wg_ref, wu_ref, wd_ref, fg_ref, o_ref, hn_ref, acc_ref, *,
                final_norm):
    c = pl.program_id(1)

    @pl.when(c == 0)
    def _():
        hn_ref[...] = _rms(x_ref[...], g_ref[...]).astype(BF16)
        acc_ref[...] = jnp.zeros(acc_ref.shape, F32)

    hn = hn_ref[...]
    gate = _dot(hn, wg_ref[...])
    up = _dot(hn, wu_ref[...])
    h = (gate / (1.0 + jnp.exp(-gate))) * up
    acc_ref[...] += _dot(h.astype(BF16), wd_ref[...])

    @pl.when(c == pl.num_programs(1) - 1)
    def _():
        y = x_ref[...] + acc_ref[...]
        if final_norm:
            y = _rms(y, fg_ref[...])
        o_ref[...] = y


def _ffn(x2, gain, w_gu, w_down, final_gain, final_norm):
    t, d = x2.shape
    d_ff = w_down.shape[0]
    tm = min(1024, t)
    tf = 256
    nf = d_ff // tf
    kern = functools.partial(_ffn_kernel, final_norm=final_norm)
    return pl.pallas_call(
        kern,
        out_shape=jax.ShapeDtypeStruct((t, d), F32),
        grid=(t // tm, nf),
        in_specs=[
            pl.BlockSpec((tm, d), lambda i, c: (i, 0)),
            pl.BlockSpec((1, d), lambda i, c: (0, 0)),
            pl.BlockSpec((d, tf), lambda i, c: (0, c)),
            pl.BlockSpec((d, tf), lambda i, c: (0, c + nf)),
            pl.BlockSpec((tf, d), lambda i, c: (c, 0)),
            pl.BlockSpec((1, d), lambda i, c: (0, 0)),
        ],
        out_specs=pl.BlockSpec((tm, d), lambda i, c: (i, 0)),
        scratch_shapes=[pltpu.VMEM((tm, d), BF16), pltpu.VMEM((tm, d), F32)],
        compiler_params=_params(("arbitrary", "arbitrary"), VMEM_LIMIT),
        name="ffn",
    )(x2, gain, w_gu, w_gu, w_down, final_gain)


def _layout_in_weights(w_in_l):
    d = w_in_l.shape[0]
    w_a, w_b, w_c = H_A * HEAD_DIM, H_B * 2 * HEAD_DIM, H_C * HEAD_DIM
    splits = (w_a, w_a, w_a, w_b, w_b, w_b, w_c, w_c, w_c, H_IDX * D_IDX, D_IDX, H_IDX, 3 * d)
    offs = [int(o) for o in np.cumsum(splits)[:-1]]
    qa, ka, va, qb, kb, vb, qc, kc, vc, qi, ki, wi, gates = jnp.split(w_in_l, offs, axis=1)
    scale = HEAD_DIM ** -0.5
    scale2 = scale * math.log2(math.e)
    qa, qb, qc, qi = qa * scale, qb * scale2, qc * scale2, qi * (D_IDX ** -0.5)
    pad = jnp.zeros((d, LANES - H_IDX), w_in_l.dtype)
    tail = jnp.concatenate([qi, ki, ki, wi, pad], axis=1)
    w_main = jnp.concatenate([qb, kb, qc, kc, tail, qa, ka, va, gates], axis=1)
    assert w_main.shape[1] == N_PROJ_COLS
    return w_main.astype(BF16), vb.T.astype(BF16), vc.T.astype(BF16)


def _rope_tables(seq):
    pos = jnp.arange(seq, dtype=F32)
    inv = ROPE_THETA ** (-jnp.arange(0, HEAD_DIM, 2, dtype=F32) / HEAD_DIM)
    ang = pos[:, None] * inv[None, :]
    cos, sin = jnp.cos(ang), jnp.sin(ang)
    zero = jnp.zeros_like(sin)
    reps = LANES // HEAD_DIM
    cos_t = jnp.tile(jnp.concatenate([cos, cos], axis=1), (1, reps))
    sina_t = jnp.tile(jnp.concatenate([-sin, zero], axis=1), (1, reps))
    sinb_t = jnp.tile(jnp.concatenate([zero, sin], axis=1), (1, reps))
    return cos_t, sina_t, sinb_t


def kernel(x, mem, norm_mix, w_in, rel_bias_a, lambda_vecs, subln_b, w_up_a, w_up_b, w_up_c,
           w_out, norm_cross, w_q_x, w_kv_x, w_o_x, norm_ffn, w_gu, w_down, mem_norm, final_norm):
    b, s, d = x.shape
    depth = w_in.shape[0]
    assert s % TQ_B == 0 and LEFT_CHUNKS * CHUNK == 2 * TQ and TQ == TK
    topk = min(TOPK_MAX, s // 4)
    cos_t, sina_t, sinb_t = _rope_tables(s)
    x2 = x.reshape(b * s, d)
    mem2 = mem.reshape(b * mem.shape[1], d)
    kv_all = _mem_kv(mem2, mem_norm.reshape(1, d), w_kv_x.astype(BF16))
    kv_all = kv_all.reshape(depth, b, mem.shape[1], -1)
    for l in range(depth):
        w_main, w_vbt, w_vct = _layout_in_weights(w_in[l])
        proj2, vbt, vct = _in_proj(x2, norm_mix[l].reshape(1, d), w_main, cos_t, sina_t, sinb_t,
                                   w_vbt, w_vct, s)
        proj3 = proj2.reshape(b, s, N_PROJ_COLS)
        lam_init = 0.8 - 0.6 * math.exp(-0.3 * l)
        o_a = _band_attention(proj3, _band_bias(rel_bias_a[l], TQ))
        o_b = _diff_attention(proj3, vbt, lambda_vecs[l].astype(F32),
                              subln_b[l].reshape(LANES, 1), lam_init)
        o_c = _sparse_attention(proj3, vct, topk)
        x2 = _mix_out(x2, o_a.reshape(b * s, -1), o_b.reshape(b * s, -1), o_c.reshape(b * s, -1),
                      proj2, w_up_a[l].astype(BF16), w_up_b[l].astype(BF16),
                      w_up_c[l].astype(BF16), w_out[l].astype(BF16))
        x2 = _cross_attention(x2, norm_cross[l].reshape(1, d),
                              (w_q_x[l] * HEAD_DIM ** -0.5).astype(BF16), kv_all[l],
                              w_o_x[l].astype(BF16), s)
        x2 = _ffn(x2, norm_ffn[l].reshape(1, d), w_gu[l].astype(BF16), w_down[l].astype(BF16),
                  final_norm.reshape(1, d), final_norm=(l == depth - 1))
    return x2.reshape(b, s, d)
```

```python
import functools
import math

import jax
import jax.numpy as jnp
import numpy as np
from jax import lax
from jax.experimental import pallas as pl
from jax.experimental.pallas import tpu as pltpu

F32 = jnp.float32
BF16 = jnp.bfloat16
I32 = jnp.int32
I16 = jnp.int16

CHUNK = 64
HEAD_DIM = 64
H_A = 8
LEFT_CHUNKS = 8
REL_MAX = 256
H_B = 4
H_C = 8
H_IDX = 4
D_IDX = 64
TOPK_MAX = 256
H_X = 4
ROPE_THETA = 10000.0
EPS = 1e-6

LANES = 128
SUBLANES = 8
TQ = 256
TK = 256
TQ_B = 512
SLAB = 128
PROJ_TN = 512
VMEM_LIMIT = 56 * 1024 * 1024

NEG_INIT = -1e30
NEG_MASK = -2e30
INT_MIN = -(2 ** 31)

PB_QB, PB_KB, PB_QC, PB_KC, PB_TAIL, PB_QA, PB_KA, PB_VA, PB_GATES = 0, 1, 2, 3, 4, 5, 6, 7, 8
PB_ROPE = 5
PB_COUNT = 14
TAIL_QI, TAIL_KI, TAIL_WI = 0, 256, 384


def _dot(a, b):
    return jnp.dot(a, b, preferred_element_type=F32)


def _dot_nt(a, b):
    return lax.dot_general(a, b, (((1,), (1,)), ((), ())), preferred_element_type=F32)


def _rms(x, g):
    ms = jnp.mean(x * x, axis=-1, keepdims=True)
    return (x * lax.rsqrt(ms + EPS)) * g


def _params(sem, vmem=None):
    return pltpu.CompilerParams(dimension_semantics=sem, vmem_limit_bytes=vmem)


def _head_halves(pair):
    lo = lax.broadcasted_iota(I32, pair.shape, 1) < HEAD_DIM
    zero = jnp.zeros_like(pair)
    return jnp.where(lo, pair, zero), jnp.where(lo, zero, pair)


def _in_proj_kernel(x_ref, g_ref, w_ref, cos_ref, sina_ref, sinb_ref, wvb_ref, wvc_ref,
                    o_ref, vbt_ref, vct_ref):
    xn = _rms(x_ref[...], g_ref[...]).astype(BF16)
    vbt = _dot_nt(wvb_ref[...], xn).astype(BF16)
    for c in range(vbt_ref.shape[0]):
        w = vbt_ref.shape[2]
        vbt_ref[c] = vbt[:, c * w:(c + 1) * w]
    vct = _dot_nt(wvc_ref[...], xn).astype(BF16)
    for c in range(vct_ref.shape[0]):
        w = vct_ref.shape[2]
        vct_ref[c] = vct[:, c * w:(c + 1) * w]

    tn = o_ref.shape[2]
    half = HEAD_DIM // 2
    for j in range(PB_COUNT):
        a = _dot(xn, w_ref[j])
        if j < PB_ROPE:
            for c in range(tn // LANES):
                sl = slice(c * LANES, (c + 1) * LANES)
                ac = a[:, sl]
                if j == PB_TAIL and c * LANES >= TAIL_WI:
                    o_ref[j, :, sl] = ac.astype(o_ref.dtype)
                else:
                    rot = (pltpu.roll(ac, LANES - half, 1) * sina_ref[...]
                           + pltpu.roll(ac, half, 1) * sinb_ref[...])
                    o_ref[j, :, sl] = (ac * cos_ref[...] + rot).astype(o_ref.dtype)
        elif j < PB_GATES:
            o_ref[j] = a.astype(o_ref.dtype)
        else:
            o_ref[j] = (1.0 / (1.0 + jnp.exp(-a))).astype(o_ref.dtype)


def _in_proj(x2, gain, w_main, cos_t, sina_t, sinb_t, w_vbt, w_vct, seq):
    t, d = x2.shape
    tm = min(512, seq)
    tn = PROJ_TN
    pos_blocks = seq // tm
    wb, wc = w_vbt.shape[0], w_vct.shape[0]
    resident = pl.Buffered(1)
    return pl.pallas_call(
        _in_proj_kernel,
        out_shape=(
            jax.ShapeDtypeStruct((PB_COUNT, t, tn), BF16),
            jax.ShapeDtypeStruct((t // TQ_B, wb, TQ_B), BF16),
            jax.ShapeDtypeStruct((t // TK, wc, TK), BF16),
        ),
        grid=(t // tm,),
        in_specs=[
            pl.BlockSpec((tm, d), lambda i: (i, 0)),
            pl.BlockSpec((1, d), lambda i: (0, 0)),
            pl.BlockSpec((PB_COUNT, d, tn), lambda i: (0, 0, 0), pipeline_mode=resident),
            pl.BlockSpec((tm, LANES), lambda i: (i % pos_blocks, 0)),
            pl.BlockSpec((tm, LANES), lambda i: (i % pos_blocks, 0)),
            pl.BlockSpec((tm, LANES), lambda i: (i % pos_blocks, 0)),
            pl.BlockSpec((wb, d), lambda i: (0, 0), pipeline_mode=resident),
            pl.BlockSpec((wc, d), lambda i: (0, 0), pipeline_mode=resident),
        ],
        out_specs=(
            pl.BlockSpec((PB_COUNT, tm, tn), lambda i: (0, i, 0)),
            pl.BlockSpec((tm // TQ_B, wb, TQ_B), lambda i: (i, 0, 0)),
            pl.BlockSpec((tm // TK, wc, TK), lambda i: (i, 0, 0)),
        ),
        compiler_params=_params(("arbitrary",), VMEM_LIMIT),
        name="in_proj",
    )(x2, gain, w_main, cos_t, sina_t, sinb_t, w_vbt, w_vct)


def _band_kernel(q_ref, k0_ref, k1_ref, k2_ref, v0_ref, v1_ref, v2_ref, bias_ref, o_ref):
    i = pl.program_id(1)
    tq = q_ref.shape[0]
    lo = lax.broadcasted_iota(I32, (tq, LANES), 1) < HEAD_DIM
    kj = lax.broadcasted_iota(I32, (tq, 3 * tq), 1)
    kvalid = kj >= (2 - i) * tq
    for hp in range(H_A // 2):
        sl = slice(hp * LANES, (hp + 1) * LANES)
        kw = jnp.concatenate([k0_ref[:, sl], k1_ref[:, sl], k2_ref[:, sl]], axis=0)
        vw = jnp.concatenate([v0_ref[:, sl], v1_ref[:, sl], v2_ref[:, sl]], axis=0)
        outs = []
        for hh, qm in enumerate(_head_halves(q_ref[:, sl])):
            s = _dot_nt(qm, kw) + bias_ref[2 * hp + hh]
            s = jnp.where(kvalid, s, NEG_MASK)
            m = jnp.max(s, axis=-1, keepdims=True)
            p = jnp.exp(s - m)
            l = jnp.sum(p, axis=-1, keepdims=True)
            outs.append(_dot(p.astype(BF16), vw) / l)
        o_ref[:, sl] = jnp.where(lo, outs[0], outs[1]).astype(o_ref.dtype)


def _band_bias(rel_bias, tq):
    back = LEFT_CHUNKS * CHUNK
    assert back == 2 * tq
    width = 4 * tq
    dist = back + tq - 1 - np.arange(width - 1)
    g = rel_bias[:, np.clip(dist, -(CHUNK - 1), REL_MAX) + (CHUNK - 1)].astype(F32)
    g = jnp.concatenate([g, jnp.zeros((g.shape[0], 1), F32)], axis=1)
    g = jnp.roll(g, -(tq - 1), axis=1)
    flat = jnp.tile(g, (1, tq))[:, :tq * (width - 1)]
    bias = flat.reshape(-1, tq, width - 1)[:, :, :3 * tq]
    qi = np.arange(tq)[:, None] + back
    kj = np.arange(3 * tq)[None, :]
    dc = qi // CHUNK - kj // CHUNK
    valid = (dc >= 0) & (dc <= LEFT_CHUNKS)
    return jnp.where(jnp.asarray(valid)[None], bias, NEG_MASK)


def _band_attention(proj, bias, b, s):
    tq = TQ
    nq = s // tq
    wa = H_A * HEAD_DIM

    def wspec(blk, off):
        return pl.BlockSpec((None, tq, wa),
                            lambda bb, i: (blk, bb * nq + jnp.maximum(i + off, 0), 0))

    return pl.pallas_call(
        _band_kernel,
        out_shape=jax.ShapeDtypeStruct((b * s, wa), BF16),
        grid=(b, nq),
        in_specs=[
            wspec(PB_QA, 0),
            wspec(PB_KA, -2), wspec(PB_KA, -1), wspec(PB_KA, 0),
            wspec(PB_VA, -2), wspec(PB_VA, -1), wspec(PB_VA, 0),
            pl.BlockSpec((H_A, tq, 3 * tq), lambda bb, i: (0, 0, 0)),
        ],
        out_specs=pl.BlockSpec((tq, wa), lambda bb, i: (bb * nq + i, 0)),
        compiler_params=_params(("arbitrary", "arbitrary"), VMEM_LIMIT),
        name="band_attn",
    )(proj, proj, proj, proj, proj, proj, proj, bias)


def _diff_kernel(q_ref, k_ref, vt_ref, lv_ref, sub_ref, o_ref, acc_ref, s_ref, p_ref, *,
                 lam_init):
    i = pl.program_id(1)
    tq = q_ref.shape[0]
    tk = vt_ref.shape[2]
    n_maps = 2 * H_B
    q = q_ref[...]
    qms = ()
    for h in range(H_B):
        qms += _head_halves(q[:, h * LANES:(h + 1) * LANES])
    acc_ref[...] = jnp.zeros(acc_ref.shape, F32)
    ck = lax.broadcasted_iota(I32, (tk, tq), 0) // CHUNK
    cq = lax.broadcasted_iota(I32, (tk, tq), 1) // CHUNK
    diag_ok = ck <= cq

    def fold(x):
        return x.reshape(x.shape[0] // SUBLANES, SUBLANES, tq)

    def block(j, stats, mask):
        start = pl.multiple_of(j * tk, tk)
        mxs = []
        for c in range(n_maps):
            h = c // 2
            s = _dot_nt(k_ref[pl.ds(start, tk), h * LANES:(h + 1) * LANES], qms[c])
            if mask is not None:
                s = jnp.where(mask, s, NEG_MASK)
            s_ref[c] = s
            mxs.append(jnp.max(fold(s), axis=0))
        ms, ls = stats
        new_m, new_l, alphas = [], [], []
        for c in range(n_maps):
            m_new = jnp.maximum(ms[c], jnp.max(mxs[c], axis=0, keepdims=True))
            lsum = jnp.zeros((SUBLANES, tq), F32)
            for r in range(tk // SLAB):
                p = jnp.exp2(s_ref[c, r * SLAB:(r + 1) * SLAB, :] - m_new)
                lsum = lsum + jnp.sum(fold(p), axis=0)
                p_ref[c, r * SLAB:(r + 1) * SLAB, :] = p.astype(BF16)
            alpha = jnp.exp2(ms[c] - m_new)
            new_m.append(m_new)
            new_l.append(alpha * ls[c] + jnp.sum(lsum, axis=0, keepdims=True))
            alphas.append(alpha)
        for c in range(n_maps):
            h = c // 2
            vt = vt_ref[j, h * LANES:(h + 1) * LANES, :]
            acc_ref[c] = alphas[c] * acc_ref[c] + _dot(vt, p_ref[c])
        return tuple(new_m), tuple(new_l)

    m0 = jnp.full((1, tq), NEG_INIT, F32)
    l0 = jnp.zeros((1, tq), F32)
    stats = lax.fori_loop(0, i, lambda j, st: block(j, st, None),
                          ((m0,) * n_maps, (l0,) * n_maps))
    _, ls = block(i, stats, diag_ok)

    lv = lv_ref[...]
    lam = (jnp.exp(jnp.sum(lv[0:1] * lv[1:2], axis=-1, keepdims=True))
           - jnp.exp(jnp.sum(lv[2:3] * lv[3:4], axis=-1, keepdims=True)) + lam_init)
    for h in range(H_B):
        o = acc_ref[2 * h] / ls[2 * h] - lam * (acc_ref[2 * h + 1] / ls[2 * h + 1])
        ms = jnp.mean(o * o, axis=0, keepdims=True)
        o = (o * lax.rsqrt(ms + EPS)) * sub_ref[...] * (1.0 - lam_init)
        o_ref[:, h * LANES:(h + 1) * LANES] = o.T.astype(o_ref.dtype)


def _diff_attention(proj, vbt, lambda_vec, subln_col, lam_init, b, s):
    tq = TQ_B
    nkb = s // tq
    wb = H_B * 2 * HEAD_DIM
    kern = functools.partial(_diff_kernel, lam_init=lam_init)
    return pl.pallas_call(
        kern,
        out_shape=jax.ShapeDtypeStruct((b * s, wb), BF16),
        grid=(b, nkb),
        in_specs=[
            pl.BlockSpec((None, tq, wb), lambda bb, i: (PB_QB, bb * nkb + i, 0)),
            pl.BlockSpec((None, s, wb), lambda bb, i: (PB_KB, bb, 0)),
            pl.BlockSpec((nkb, wb, tq), lambda bb, i: (bb, 0, 0)),
            pl.BlockSpec((4, HEAD_DIM), lambda bb, i: (0, 0)),
            pl.BlockSpec((LANES, 1), lambda bb, i: (0, 0)),
        ],
        out_specs=pl.BlockSpec((tq, wb), lambda bb, i: (bb * nkb + i, 0)),
        scratch_shapes=[
            pltpu.VMEM((2 * H_B, LANES, tq), F32),
            pltpu.VMEM((2 * H_B, tq, tq), F32),
            pltpu.VMEM((2 * H_B, tq, tq), BF16),
        ],
        compiler_params=_params(("arbitrary", "arbitrary"), VMEM_LIMIT),
        name="diff_attn",
    )(proj, proj, vbt, lambda_vec, subln_col)


def _sparse_kernel(q_ref, k_ref, vt_ref, tail_ref, ki_ref, o_ref,
                   keys_ref, hi_ref, lo_ref, acc_ref, s_ref, p_ref, b_ref, *, topk):
    i = pl.program_id(1)
    tq = q_ref.shape[0]
    tk = vt_ref.shape[2]

    wi = tail_ref[:, TAIL_WI:TAIL_WI + LANES]
    wt = wi.astype(F32).T * (H_IDX ** -0.5)
    w_rows = [wt[h:h + 1, :] for h in range(H_IDX)]
    qi_heads = (_head_halves(tail_ref[:, TAIL_QI:TAIL_QI + LANES])
                + _head_halves(tail_ref[:, TAIL_QI + LANES:TAIL_QI + 2 * LANES]))
    ck = lax.broadcasted_iota(I32, (tk, tq), 0) // CHUNK
    cq = lax.broadcasted_iota(I32, (tk, tq), 1) // CHUNK
    diag_ok = ck <= cq

    def score_block(j, admissible):
        start = pl.multiple_of(j * tk, tk)
        kk = ki_ref[pl.ds(start, tk), :]
        sc = jnp.zeros((tk, tq), F32)
        for h in range(H_IDX):
            sc = sc + jnp.maximum(_dot_nt(kk, qi_heads[h]), 0.0) * w_rows[h]
        bits = pltpu.bitcast(sc, I32)
        key = bits ^ ((bits >> 31) & 0x7FFFFFFF)
        key = jnp.where(sc == 0.0, 0, key)
        if admissible is not None:
            key = jnp.where(admissible, key, INT_MIN)
        keys_ref[j] = key
        hi_ref[j] = (key >> 16).astype(I16)

    def score_body(j, carry):
        score_block(j, None)
        return carry

    lax.fori_loop(0, i, score_body, 0)
    score_block(i, diag_ok)

    n_acc = 4
    grp = 2 * SUBLANES

    def count16(ref, pred_fn):
        def inner(j, accs):
            accs = list(accs)
            for g in range(tk // grp):
                rows = ref[j, g * grp:(g + 1) * grp, :]
                a = accs[g % n_acc]
                accs[g % n_acc] = jnp.where(pred_fn(rows), a + 1, a)
            return tuple(accs)
        accs = lax.fori_loop(0, i + 1, inner, (jnp.zeros((grp, tq), I16),) * n_acc)
        acc = (accs[0] + accs[1]) + (accs[2] + accs[3])
        return jnp.sum(acc.astype(F32), axis=0, keepdims=True)

    def digit16(x):
        return jnp.broadcast_to(x, (grp, tq)).astype(I16)

    def radix16(ref, want):
        def body(it, prefix):
            cand_u = prefix | lax.shift_left(jnp.int32(1), 15 - it)
            cand = digit16(cand_u - 32768)
            cnt = count16(ref, lambda rows: rows >= cand)
            return jnp.where(cnt >= want, cand_u, prefix)
        return lax.fori_loop(0, 16, body, jnp.zeros((1, tq), I32)) - 32768

    t_hi = radix16(hi_ref, topk)
    t_hi16 = digit16(t_hi)
    n_above = count16(hi_ref, lambda rows: rows > t_hi16)

    def low_body(j, carry):
        key = keys_ref[j]
        lo = ((key ^ 0x8000) << 16) >> 16
        lo_ref[j] = jnp.where((key >> 16) == t_hi, lo, -32768).astype(I16)
        return carry

    lax.fori_loop(0, i + 1, low_body, 0)
    t_lo = radix16(lo_ref, topk - n_above)
    thr = (t_hi << 16) | (t_lo + 32768)

    def count(pred_fn):
        def inner(j, accs):
            accs = list(accs)
            for g in range(tk // SUBLANES):
                rows = keys_ref[j, g * SUBLANES:(g + 1) * SUBLANES, :]
                a = accs[g % n_acc]
                accs[g % n_acc] = jnp.where(pred_fn(rows), a + 1.0, a)
            return tuple(accs)
        accs = lax.fori_loop(0, i + 1, inner, (jnp.zeros((SUBLANES, tq), F32),) * n_acc)
        acc = (accs[0] + accs[1]) + (accs[2] + accs[3])
        return jnp.sum(acc, axis=0, keepdims=True)

    n_gt = count(lambda blk: blk > thr)
    need = jnp.where(thr == INT_MIN, 0.0, topk - n_gt)

    acc_ref[...] = jnp.zeros(acc_ref.shape, F32)
    q = q_ref[...]
    q_heads = ()
    for hp in range(H_C // 2):
        q_heads += _head_halves(q[:, hp * LANES:(hp + 1) * LANES])
    earlier = (lax.broadcasted_iota(I32, (tk, tk), 0)
               > lax.broadcasted_iota(I32, (tk, tk), 1))
    earlier = jnp.where(earlier, 1.0, 0.0).astype(BF16)

    def fold(x):
        return x.reshape(x.shape[0] // SUBLANES, SUBLANES, tq)

    def attn_body(j, carry):
        tie_seen, ms, ls = carry
        start = pl.multiple_of(j * tk, tk)
        kblk = keys_ref[j]
        tie = kblk == thr
        tie_f = jnp.where(tie, 1.0, 0.0)
        rank = _dot(earlier, tie_f.astype(BF16)) + tie_seen
        sel = (kblk > thr) | (tie & (rank < need))
        b_ref[...] = jnp.where(sel, 0.0, NEG_MASK)
        mxs = []
        for h in range(H_C):
            hp = h // 2
            kb = k_ref[pl.ds(start, tk), hp * LANES:(hp + 1) * LANES]
            s = _dot_nt(kb, q_heads[h]) + b_ref[...]
            s_ref[h] = s
            mxs.append(jnp.max(fold(s), axis=0))
        new_m, new_l, alphas = [], [], []
        for h in range(H_C):
            m_new = jnp.maximum(ms[h], jnp.max(mxs[h], axis=0, keepdims=True))
            lsum = jnp.zeros((SUBLANES, tq), F32)
            for r in range(tk // SLAB):
                p = jnp.exp2(s_ref[h, r * SLAB:(r + 1) * SLAB, :] - m_new)
                lsum = lsum + jnp.sum(fold(p), axis=0)
                p_ref[h, r * SLAB:(r + 1) * SLAB, :] = p.astype(BF16)
            alpha = jnp.exp2(ms[h] - m_new)
            new_m.append(m_new)
            new_l.append(alpha * ls[h] + jnp.sum(lsum, axis=0, keepdims=True))
            alphas.append(alpha)
        for h in range(H_C):
            rows = slice(h * HEAD_DIM, (h + 1) * HEAD_DIM)
            acc_ref[rows, :] = alphas[h] * acc_ref[rows, :] + _dot(vt_ref[j, rows, :], p_ref[h])
        return (tie_seen + jnp.sum(tie_f, axis=0, keepdims=True), tuple(new_m), tuple(new_l))

    m0 = jnp.full((1, tq), NEG_INIT, F32)
    l0 = jnp.zeros((1, tq), F32)
    _, _, ls = lax.fori_loop(0, i + 1, attn_body, (l0, (m0,) * H_C, (l0,) * H_C))

    for h in range(H_C):
        rows = slice(h * HEAD_DIM, (h + 1) * HEAD_DIM)
        acc_ref[rows, :] = acc_ref[rows, :] / ls[h]
    o_ref[...] = acc_ref[...].T.astype(o_ref.dtype)


def _sparse_attention(proj, vct, topk, b, s):
    tq = TQ
    wc = H_C * HEAD_DIM
    nkb = s // TK
    nq = s // tq
    kern = functools.partial(_sparse_kernel, topk=float(topk))
    return pl.pallas_call(
        kern,
        out_shape=jax.ShapeDtypeStruct((b * s, wc), BF16),
        grid=(b, nq),
        in_specs=[
            pl.BlockSpec((None, tq, wc), lambda bb, i: (PB_QC, bb * nq + i, 0)),
            pl.BlockSpec((None, s, wc), lambda bb, i: (PB_KC, bb, 0)),
            pl.BlockSpec((nkb, wc, TK), lambda bb, i: (bb, 0, 0)),
            pl.BlockSpec((None, tq, PROJ_TN), lambda bb, i: (PB_TAIL, bb * nq + i, 0)),
            pl.BlockSpec((None, s, LANES), lambda bb, i: (PB_TAIL, bb, TAIL_KI // LANES)),
        ],
        out_specs=pl.BlockSpec((tq, wc), lambda bb, i: (bb * nq + i, 0)),
        scratch_shapes=[
            pltpu.VMEM((nkb, TK, tq), I32),
            pltpu.VMEM((nkb, TK, tq), I16),
            pltpu.VMEM((nkb, TK, tq), I16),
            pltpu.VMEM((wc, tq), F32),
            pltpu.VMEM((H_C, TK, tq), F32),
            pltpu.VMEM((H_C, TK, tq), BF16),
            pltpu.VMEM((TK, tq), F32),
        ],
        compiler_params=_params(("arbitrary", "arbitrary"), VMEM_LIMIT),
        name="sparse_attn",
    )(proj, proj, vct, proj, proj)


def _mix_kernel(x_ref, oa_ref, ob_ref, oc_ref, ga_ref, gb_ref, gc_ref,
                ua_ref, ub_ref, uc_ref, wo_ref, o_ref):
    def gate(g_ref):
        return jnp.concatenate([g_ref[0], g_ref[1]], axis=1).astype(F32)

    y = gate(ga_ref) * _dot(oa_ref[...], ua_ref[...])
    y = y + gate(gb_ref) * _dot(ob_ref[...], ub_ref[...])
    y = y + gate(gc_ref) * _dot(oc_ref[...], uc_ref[...])
    o_ref[...] = x_ref[...] + _dot(y.astype(BF16), wo_ref[...])


def _mix_out(x2, oa, ob, oc, proj, ua, ub, uc, wo):
    t, d = x2.shape
    tm = 512
    per_gate = d // PROJ_TN
    gate_blk = PB_GATES // per_gate
    w = oa.shape[1]

    def full(shape):
        return pl.BlockSpec(shape, lambda i: (0, 0))

    return pl.pallas_call(
        _mix_kernel,
        out_shape=jax.ShapeDtypeStruct((t, d), F32),
        grid=(t // tm,),
        in_specs=[
            pl.BlockSpec((tm, d), lambda i: (i, 0)),
            pl.BlockSpec((tm, w), lambda i: (i, 0)),
            pl.BlockSpec((tm, w), lambda i: (i, 0)),
            pl.BlockSpec((tm, w), lambda i: (i, 0)),
            pl.BlockSpec((per_gate, tm, PROJ_TN), lambda i: (gate_blk, i, 0)),
            pl.BlockSpec((per_gate, tm, PROJ_TN), lambda i: (gate_blk + 1, i, 0)),
            pl.BlockSpec((per_gate, tm, PROJ_TN), lambda i: (gate_blk + 2, i, 0)),
            full((w, d)), full((w, d)), full((w, d)), full((d, d)),
        ],
        out_specs=pl.BlockSpec((tm, d), lambda i: (i, 0)),
        compiler_params=_params(("arbitrary",), VMEM_LIMIT),
        name="mix_out",
    )(x2, oa, ob, oc, proj, proj, proj, ua, ub, uc, wo)


def _mem_kv_kernel(mem_ref, g_ref, w_ref, o_ref):
    o_ref[...] = _dot(_rms(mem_ref[...], g_ref[...]).astype(BF16), w_ref[...]).astype(o_ref.dtype)


def _mem_kv(mem2, mem_norm, w_kv):
    r, d = mem2.shape
    depth, _, n = w_kv.shape
    tm = min(1024, r)
    return pl.pallas_call(
        _mem_kv_kernel,
        out_shape=jax.ShapeDtypeStruct((depth, r, n), BF16),
        grid=(depth, r // tm),
        in_specs=[
            pl.BlockSpec((tm, d), lambda l, i: (i, 0)),
            pl.BlockSpec((1, d), lambda l, i: (0, 0)),
            pl.BlockSpec((None, d, n), lambda l, i: (l, 0, 0)),
        ],
        out_specs=pl.BlockSpec((None, tm, n), lambda l, i: (l, i, 0)),
        compiler_params=_params(("arbitrary", "arbitrary"), VMEM_LIMIT),
        name="mem_kv",
    )(mem2, mem_norm, w_kv)


def _cross_kernel(x_ref, g_ref, wq_ref, kv_ref, wo_ref, o_ref):
    x = x_ref[...]
    tm = x.shape[0]
    hn = _rms(x, g_ref[...]).astype(BF16)
    q = _dot(hn, wq_ref[...]).astype(BF16)
    kv = kv_ref[...]
    wkv = H_X * HEAD_DIM
    lo = lax.broadcasted_iota(I32, (tm, LANES), 1) < HEAD_DIM
    pairs = []
    for hp in range(H_X // 2):
        kp = kv[:, hp * LANES:(hp + 1) * LANES]
        vp = kv[:, wkv + hp * LANES:wkv + (hp + 1) * LANES]
        outs = []
        for qm in _head_halves(q[:, hp * LANES:(hp + 1) * LANES]):
            s = _dot_nt(qm, kp)
            m = jnp.max(s, axis=-1, keepdims=True)
            p = jnp.exp(s - m)
            l = jnp.sum(p, axis=-1, keepdims=True)
            outs.append(_dot(p.astype(BF16), vp) / l)
        pairs.append(jnp.where(lo, outs[0], outs[1]).astype(BF16))
    o = jnp.concatenate(pairs, axis=1)
    o_ref[...] = x + _dot(o, wo_ref[...])


def _cross_attention(x2, gain, wq, kv, wo, seq):
    t, d = x2.shape
    tm = min(512, seq)
    per_batch = seq // tm
    n_mem, wkv2 = kv.shape[1], kv.shape[2]
    wq_cols = wq.shape[1]
    return pl.pallas_call(
        _cross_kernel,
        out_shape=jax.ShapeDtypeStruct((t, d), F32),
        grid=(t // tm,),
        in_specs=[
            pl.BlockSpec((tm, d), lambda i: (i, 0)),
            pl.BlockSpec((1, d), lambda i: (0, 0)),
            pl.BlockSpec((d, wq_cols), lambda i: (0, 0)),
            pl.BlockSpec((None, n_mem, wkv2), lambda i: (i // per_batch, 0, 0)),
            pl.BlockSpec((wq_cols, d), lambda i: (0, 0)),
        ],
        out_specs=pl.BlockSpec((tm, d), lambda i: (i, 0)),
        compiler_params=_params(("arbitrary",), VMEM_LIMIT),
        name="cross_attn",
    )(x2, gain, wq, kv, wo)


def _ffn_kernel(x_ref, g_ref, wgu_ref, wd_ref, fg_ref, o_ref, acc_ref, *, final_norm, tf):
    x = x_ref[...]
    hn = _rms(x, g_ref[...]).astype(BF16)
    d_ff = wd_ref.shape[0]
    for c in range(d_ff // tf):
        gate = _dot(hn, wgu_ref[:, c * tf:(c + 1) * tf])
        up = _dot(hn, wgu_ref[:, d_ff + c * tf:d_ff + (c + 1) * tf])
        h = (gate / (1.0 + jnp.exp(-gate))) * up
        part = _dot(h.astype(BF16), wd_ref[c * tf:(c + 1) * tf, :])
        if c == 0:
            acc_ref[...] = part
        else:
            acc_ref[...] += part
    y = x + acc_ref[...]
    if final_norm:
        y = _rms(y, fg_ref[...])
    o_ref[...] = y


def _ffn(x2, gain, w_gu, w_down, final_gain, final_norm):
    t, d = x2.shape
    d_ff = w_down.shape[0]
    tm = min(512, t)
    tf = 2 * LANES
    assert d_ff % tf == 0
    kern = functools.partial(_ffn_kernel, final_norm=final_norm, tf=tf)
    resident = pl.Buffered(1)
    return pl.pallas_call(
        kern,
        out_shape=jax.ShapeDtypeStruct((t, d), F32),
        grid=(t // tm,),
        in_specs=[
            pl.BlockSpec((tm, d), lambda i: (i, 0)),
            pl.BlockSpec((1, d), lambda i: (0, 0)),
            pl.BlockSpec((d, 2 * d_ff), lambda i: (0, 0), pipeline_mode=resident),
            pl.BlockSpec((d_ff, d), lambda i: (0, 0), pipeline_mode=resident),
            pl.BlockSpec((1, d), lambda i: (0, 0)),
        ],
        out_specs=pl.BlockSpec((tm, d), lambda i: (i, 0)),
        scratch_shapes=[pltpu.VMEM((tm, d), F32)],
        compiler_params=_params(("arbitrary",), VMEM_LIMIT),
        name="ffn",
    )(x2, gain, w_gu, w_down, final_gain)


def _layout_in_weights(w_in_l):
    d = w_in_l.shape[0]
    w_a, w_b, w_c = H_A * HEAD_DIM, H_B * 2 * HEAD_DIM, H_C * HEAD_DIM
    splits = (w_a, w_a, w_a, w_b, w_b, w_b, w_c, w_c, w_c, H_IDX * D_IDX, D_IDX, H_IDX, 3 * d)
    offs = [int(o) for o in np.cumsum(splits)[:-1]]
    qa, ka, va, qb, kb, vb, qc, kc, vc, qi, ki, wi, gates = jnp.split(w_in_l, offs, axis=1)
    scale = HEAD_DIM ** -0.5
    scale2 = scale * math.log2(math.e)
    qa, qb, qc, qi = qa * scale, qb * scale2, qc * scale2, qi * (D_IDX ** -0.5)
    pad = jnp.zeros((d, LANES - H_IDX), w_in_l.dtype)
    tail = jnp.concatenate([qi, ki, ki, wi, pad], axis=1)
    w_main = jnp.concatenate([qb, kb, qc, kc, tail, qa, ka, va, gates], axis=1)
    assert w_main.shape[1] == PB_COUNT * PROJ_TN
    w_blocks = w_main.reshape(d, PB_COUNT, PROJ_TN).transpose(1, 0, 2)
    return w_blocks.astype(BF16), vb.T.astype(BF16), vc.T.astype(BF16)


def _rope_tables(seq):
    pos = jnp.arange(seq, dtype=F32)
    inv = ROPE_THETA ** (-jnp.arange(0, HEAD_DIM, 2, dtype=F32) / HEAD_DIM)
    ang = pos[:, None] * inv[None, :]
    cos, sin = jnp.cos(ang), jnp.sin(ang)
    zero = jnp.zeros_like(sin)
    reps = LANES // HEAD_DIM
    cos_t = jnp.tile(jnp.concatenate([cos, cos], axis=1), (1, reps))
    sina_t = jnp.tile(jnp.concatenate([-sin, zero], axis=1), (1, reps))
    sinb_t = jnp.tile(jnp.concatenate([zero, sin], axis=1), (1, reps))
    return cos_t, sina_t, sinb_t


def kernel(x, mem, norm_mix, w_in, rel_bias_a, lambda_vecs, subln_b, w_up_a, w_up_b, w_up_c,
           w_out, norm_cross, w_q_x, w_kv_x, w_o_x, norm_ffn, w_gu, w_down, mem_norm, final_norm):
    b, s, d = x.shape
    depth = w_in.shape[0]
    assert s % TQ_B == 0 and LEFT_CHUNKS * CHUNK == 2 * TQ and TQ == TK
    topk = min(TOPK_MAX, s // 4)
    cos_t, sina_t, sinb_t = _rope_tables(s)
    x2 = x.reshape(b * s, d)
    mem2 = mem.reshape(b * mem.shape[1], d)
    kv_all = _mem_kv(mem2, mem_norm.reshape(1, d), w_kv_x.astype(BF16))
    kv_all = kv_all.reshape(depth, b, mem.shape[1], -1)
    for l in range(depth):
        w_main, w_vbt, w_vct = _layout_in_weights(w_in[l])
        proj, vbt, vct = _in_proj(x2, norm_mix[l].reshape(1, d), w_main, cos_t, sina_t, sinb_t,
                                  w_vbt, w_vct, s)
        lam_init = 0.8 - 0.6 * math.exp(-0.3 * l)
        o_a = _band_attention(proj, _band_bias(rel_bias_a[l], TQ), b, s)
        o_b = _diff_attention(proj, vbt, lambda_vecs[l].astype(F32),
                              subln_b[l].reshape(LANES, 1), lam_init, b, s)
        o_c = _sparse_attention(proj, vct, topk, b, s)
        x2 = _mix_out(x2, o_a, o_b, o_c,
                      proj, w_up_a[l].astype(BF16), w_up_b[l].astype(BF16),
                      w_up_c[l].astype(BF16), w_out[l].astype(BF16))
        x2 = _cross_attention(x2, norm_cross[l].reshape(1, d),
                              (w_q_x[l] * HEAD_DIM ** -0.5).astype(BF16), kv_all[l],
                              w_o_x[l].astype(BF16), s)
        x2 = _ffn(x2, norm_ffn[l].reshape(1, d), w_gu[l].astype(BF16), w_down[l].astype(BF16),
                  final_norm.reshape(1, d), final_norm=(l == depth - 1))
    return x2.reshape(b, s, d)
```

```python
import functools
import math

import jax
import jax.numpy as jnp
import numpy as np
from jax import lax
from jax.experimental import pallas as pl
from jax.experimental.pallas import tpu as pltpu

F32 = jnp.float32
BF16 = jnp.bfloat16
I32 = jnp.int32
I16 = jnp.int16

CHUNK = 64
HEAD_DIM = 64
H_A = 8
LEFT_CHUNKS = 8
REL_MAX = 256
H_B = 4
H_C = 8
H_IDX = 4
D_IDX = 64
TOPK_MAX = 256
H_X = 4
ROPE_THETA = 10000.0
EPS = 1e-6

LANES = 128
SUBLANES = 8
TQ = 256
TK = 256
TQ_B = 512
SLAB = 128
PROJ_TN = 512
VMEM_LIMIT = 56 * 1024 * 1024

NEG_INIT = -1e30
NEG_MASK = -2e30
INT_MIN = -(2 ** 31)

PB_QB, PB_KB, PB_QC, PB_KC, PB_TAIL, PB_QA, PB_KA, PB_VA, PB_GATES = 0, 1, 2, 3, 4, 5, 6, 7, 8
PB_ROPE = 5
PB_COUNT = 14
TAIL_QI, TAIL_KI, TAIL_WI = 0, 256, 384


def _dot(a, b):
    return jnp.dot(a, b, preferred_element_type=F32)


def _dot_nt(a, b):
    return lax.dot_general(a, b, (((1,), (1,)), ((), ())), preferred_element_type=F32)


def _rms(x, g):
    ms = jnp.mean(x * x, axis=-1, keepdims=True)
    return (x * lax.rsqrt(ms + EPS)) * g


def _params(sem, vmem=None):
    return pltpu.CompilerParams(dimension_semantics=sem, vmem_limit_bytes=vmem)


def _head_halves(pair):
    lo = lax.broadcasted_iota(I32, pair.shape, 1) < HEAD_DIM
    zero = jnp.zeros_like(pair)
    return jnp.where(lo, pair, zero), jnp.where(lo, zero, pair)


def _in_proj_kernel(x_ref, g_ref, w_ref, cos_ref, sina_ref, sinb_ref, wvb_ref, wvc_ref,
                    o_ref, vbt_ref, vct_ref):
    xn = _rms(x_ref[...], g_ref[...]).astype(BF16)
    vbt = _dot_nt(wvb_ref[...], xn).astype(BF16)
    for c in range(vbt_ref.shape[0]):
        w = vbt_ref.shape[2]
        vbt_ref[c] = vbt[:, c * w:(c + 1) * w]
    vct = _dot_nt(wvc_ref[...], xn).astype(BF16)
    for c in range(vct_ref.shape[0]):
        w = vct_ref.shape[2]
        vct_ref[c] = vct[:, c * w:(c + 1) * w]

    tn = o_ref.shape[2]
    half = HEAD_DIM // 2
    for j in range(PB_COUNT):
        a = _dot(xn, w_ref[j])
        if j < PB_ROPE:
            for c in range(tn // LANES):
                sl = slice(c * LANES, (c + 1) * LANES)
                ac = a[:, sl]
                if j == PB_TAIL and c * LANES >= TAIL_WI:
                    o_ref[j, :, sl] = ac.astype(o_ref.dtype)
                else:
                    rot = (pltpu.roll(ac, LANES - half, 1) * sina_ref[...]
                           + pltpu.roll(ac, half, 1) * sinb_ref[...])
                    o_ref[j, :, sl] = (ac * cos_ref[...] + rot).astype(o_ref.dtype)
        elif j < PB_GATES:
            o_ref[j] = a.astype(o_ref.dtype)
        else:
            o_ref[j] = (1.0 / (1.0 + jnp.exp(-a))).astype(o_ref.dtype)


def _in_proj(x2, gain, w_main, cos_t, sina_t, sinb_t, w_vbt, w_vct, seq):
    t, d = x2.shape
    tm = min(512, seq)
    tn = PROJ_TN
    pos_blocks = seq // tm
    wb, wc = w_vbt.shape[0], w_vct.shape[0]
    resident = pl.Buffered(1)
    return pl.pallas_call(
        _in_proj_kernel,
        out_shape=(
            jax.ShapeDtypeStruct((PB_COUNT, t, tn), BF16),
            jax.ShapeDtypeStruct((t // TQ_B, wb, TQ_B), BF16),
            jax.ShapeDtypeStruct((t // TK, wc, TK), BF16),
        ),
        grid=(t // tm,),
        in_specs=[
            pl.BlockSpec((tm, d), lambda i: (i, 0)),
            pl.BlockSpec((1, d), lambda i: (0, 0)),
            pl.BlockSpec((PB_COUNT, d, tn), lambda i: (0, 0, 0), pipeline_mode=resident),
            pl.BlockSpec((tm, LANES), lambda i: (i % pos_blocks, 0)),
            pl.BlockSpec((tm, LANES), lambda i: (i % pos_blocks, 0)),
            pl.BlockSpec((tm, LANES), lambda i: (i % pos_blocks, 0)),
            pl.BlockSpec((wb, d), lambda i: (0, 0), pipeline_mode=resident),
            pl.BlockSpec((wc, d), lambda i: (0, 0), pipeline_mode=resident),
        ],
        out_specs=(
            pl.BlockSpec((PB_COUNT, tm, tn), lambda i: (0, i, 0)),
            pl.BlockSpec((tm // TQ_B, wb, TQ_B), lambda i: (i, 0, 0)),
            pl.BlockSpec((tm // TK, wc, TK), lambda i: (i, 0, 0)),
        ),
        compiler_params=_params(("arbitrary",), VMEM_LIMIT),
        name="in_proj",
    )(x2, gain, w_main, cos_t, sina_t, sinb_t, w_vbt, w_vct)


def _band_kernel(q_ref, k0_ref, k1_ref, k2_ref, v0_ref, v1_ref, v2_ref, bias_ref, o_ref):
    i = pl.program_id(1)
    tq = q_ref.shape[0]
    lo = lax.broadcasted_iota(I32, (tq, LANES), 1) < HEAD_DIM
    kj = lax.broadcasted_iota(I32, (tq, 3 * tq), 1)
    kvalid = kj >= (2 - i) * tq
    for hp in range(H_A // 2):
        sl = slice(hp * LANES, (hp + 1) * LANES)
        kw = jnp.concatenate([k0_ref[:, sl], k1_ref[:, sl], k2_ref[:, sl]], axis=0)
        vw = jnp.concatenate([v0_ref[:, sl], v1_ref[:, sl], v2_ref[:, sl]], axis=0)
        outs = []
        for hh, qm in enumerate(_head_halves(q_ref[:, sl])):
            s = _dot_nt(qm, kw) + bias_ref[2 * hp + hh]
            s = jnp.where(kvalid, s, NEG_MASK)
            m = jnp.max(s, axis=-1, keepdims=True)
            p = jnp.exp(s - m)
            l = jnp.sum(p, axis=-1, keepdims=True)
            outs.append(_dot(p.astype(BF16), vw) / l)
        o_ref[:, sl] = jnp.where(lo, outs[0], outs[1]).astype(o_ref.dtype)


def _band_bias(rel_bias, tq):
    back = LEFT_CHUNKS * CHUNK
    assert back == 2 * tq
    width = 4 * tq
    dist = back + tq - 1 - np.arange(width - 1)
    g = rel_bias[:, np.clip(dist, -(CHUNK - 1), REL_MAX) + (CHUNK - 1)].astype(F32)
    g = jnp.concatenate([g, jnp.zeros((g.shape[0], 1), F32)], axis=1)
    g = jnp.roll(g, -(tq - 1), axis=1)
    flat = jnp.tile(g, (1, tq))[:, :tq * (width - 1)]
    bias = flat.reshape(-1, tq, width - 1)[:, :, :3 * tq]
    qi = np.arange(tq)[:, None] + back
    kj = np.arange(3 * tq)[None, :]
    dc = qi // CHUNK - kj // CHUNK
    valid = (dc >= 0) & (dc <= LEFT_CHUNKS)
    return jnp.where(jnp.asarray(valid)[None], bias, NEG_MASK)


def _band_attention(proj, bias, b, s):
    tq = TQ
    nq = s // tq
    wa = H_A * HEAD_DIM

    def wspec(blk, off):
        return pl.BlockSpec((None, tq, wa),
                            lambda bb, i: (blk, bb * nq + jnp.maximum(i + off, 0), 0))

    return pl.pallas_call(
        _band_kernel,
        out_shape=jax.ShapeDtypeStruct((b * s, wa), BF16),
        grid=(b, nq),
        in_specs=[
            wspec(PB_QA, 0),
            wspec(PB_KA, -2), wspec(PB_KA, -1), wspec(PB_KA, 0),
            wspec(PB_VA, -2), wspec(PB_VA, -1), wspec(PB_VA, 0),
            pl.BlockSpec((H_A, tq, 3 * tq), lambda bb, i: (0, 0, 0)),
        ],
        out_specs=pl.BlockSpec((tq, wa), lambda bb, i: (bb * nq + i, 0)),
        compiler_params=_params(("arbitrary", "arbitrary"), VMEM_LIMIT),
        name="band_attn",
    )(proj, proj, proj, proj, proj, proj, proj, bias)


def _diff_kernel(q_ref, k_ref, vt_ref, lv_ref, sub_ref, o_ref, acc_ref, s_ref, p_ref, *,
                 lam_init):
    i = pl.program_id(1)
    tq = q_ref.shape[0]
    tk = vt_ref.shape[2]
    n_maps = 2 * H_B
    q = q_ref[...]
    qms = ()
    for h in range(H_B):
        qms += _head_halves(q[:, h * LANES:(h + 1) * LANES])
    acc_ref[...] = jnp.zeros(acc_ref.shape, F32)
    ck = lax.broadcasted_iota(I32, (tk, tq), 0) // CHUNK
    cq = lax.broadcasted_iota(I32, (tk, tq), 1) // CHUNK
    diag_ok = ck <= cq

    def fold(x):
        return x.reshape(x.shape[0] // SUBLANES, SUBLANES, tq)

    ones = jnp.ones((2 * SUBLANES, tk), BF16)

    def block(j, stats, mask):
        start = pl.multiple_of(j * tk, tk)
        mxs = []
        for c in range(n_maps):
            h = c // 2
            s = _dot_nt(k_ref[pl.ds(start, tk), h * LANES:(h + 1) * LANES], qms[c])
            if mask is not None:
                s = jnp.where(mask, s, NEG_MASK)
            s_ref[c] = s
            mxs.append(jnp.max(fold(s), axis=0))
        ms, ls = stats
        new_m, new_l, alphas = [], [], []
        for c in range(n_maps):
            m_new = jnp.maximum(ms[c], jnp.max(mxs[c], axis=0, keepdims=True))
            for r in range(tk // SLAB):
                p = jnp.exp2(s_ref[c, r * SLAB:(r + 1) * SLAB, :] - m_new)
                p_ref[c, r * SLAB:(r + 1) * SLAB, :] = p.astype(BF16)
            alpha = jnp.exp2(ms[c] - m_new)
            new_m.append(m_new)
            vt = jnp.concatenate([vt_ref[j, (c // 2) * LANES:(c // 2 + 1) * LANES, :], ones], axis=0)
            pv = _dot(vt, p_ref[c])
            acc_ref[c] = alpha * acc_ref[c] + pv[:LANES]
            new_l.append(alpha * ls[c] + pv[LANES:LANES + 1])
        return tuple(new_m), tuple(new_l)

    m0 = jnp.full((1, tq), NEG_INIT, F32)
    l0 = jnp.zeros((1, tq), F32)
    stats = lax.fori_loop(0, i, lambda j, st: block(j, st, None),
                          ((m0,) * n_maps, (l0,) * n_maps))
    _, ls = block(i, stats, diag_ok)

    lv = lv_ref[...]
    lam = (jnp.exp(jnp.sum(lv[0:1] * lv[1:2], axis=-1, keepdims=True))
           - jnp.exp(jnp.sum(lv[2:3] * lv[3:4], axis=-1, keepdims=True)) + lam_init)
    for h in range(H_B):
        o = acc_ref[2 * h] / ls[2 * h] - lam * (acc_ref[2 * h + 1] / ls[2 * h + 1])
        ms = jnp.mean(o * o, axis=0, keepdims=True)
        o = (o * lax.rsqrt(ms + EPS)) * sub_ref[...] * (1.0 - lam_init)
        o_ref[:, h * LANES:(h + 1) * LANES] = o.T.astype(o_ref.dtype)


def _diff_attention(proj, vbt, lambda_vec, subln_col, lam_init, b, s):
    tq = TQ_B
    nkb = s // tq
    wb = H_B * 2 * HEAD_DIM
    kern = functools.partial(_diff_kernel, lam_init=lam_init)
    return pl.pallas_call(
        kern,
        out_shape=jax.ShapeDtypeStruct((b * s, wb), BF16),
        grid=(b, nkb),
        in_specs=[
            pl.BlockSpec((None, tq, wb), lambda bb, i: (PB_QB, bb * nkb + i, 0)),
            pl.BlockSpec((None, s, wb), lambda bb, i: (PB_KB, bb, 0)),
            pl.BlockSpec((nkb, wb, tq), lambda bb, i: (bb, 0, 0)),
            pl.BlockSpec((4, HEAD_DIM), lambda bb, i: (0, 0)),
            pl.BlockSpec((LANES, 1), lambda bb, i: (0, 0)),
        ],
        out_specs=pl.BlockSpec((tq, wb), lambda bb, i: (bb * nkb + i, 0)),
        scratch_shapes=[
            pltpu.VMEM((2 * H_B, LANES, tq), F32),
            pltpu.VMEM((2 * H_B, tq, tq), F32),
            pltpu.VMEM((2 * H_B, tq, tq), BF16),
        ],
        compiler_params=_params(("arbitrary", "arbitrary"), VMEM_LIMIT),
        name="diff_attn",
    )(proj, proj, vbt, lambda_vec, subln_col)


def _sparse_kernel(q_ref, k_ref, vt_ref, tail_ref, ki_ref, o_ref,
                   keys_ref, hi_ref, lo_ref, acc_ref, s_ref, p_ref, b_ref, *, topk):
    i = pl.program_id(1)
    tq = q_ref.shape[0]
    tk = vt_ref.shape[2]

    wi = tail_ref[:, TAIL_WI:TAIL_WI + LANES]
    wt = wi.astype(F32).T * (H_IDX ** -0.5)
    w_rows = [wt[h:h + 1, :] for h in range(H_IDX)]
    qi_heads = (_head_halves(tail_ref[:, TAIL_QI:TAIL_QI + LANES])
                + _head_halves(tail_ref[:, TAIL_QI + LANES:TAIL_QI + 2 * LANES]))
    ck = lax.broadcasted_iota(I32, (tk, tq), 0) // CHUNK
    cq = lax.broadcasted_iota(I32, (tk, tq), 1) // CHUNK
    diag_ok = ck <= cq

    def score_block(j, admissible):
        start = pl.multiple_of(j * tk, tk)
        kk = ki_ref[pl.ds(start, tk), :]
        for h in range(H_IDX):
            s_ref[h] = _dot_nt(kk, qi_heads[h])
        sc = jnp.zeros((tk, tq), F32)
        for h in range(H_IDX):
            sc = sc + jnp.maximum(s_ref[h], 0.0) * w_rows[h]
        bits = pltpu.bitcast(sc, I32)
        key = bits ^ ((bits >> 31) & 0x7FFFFFFF)
        key = jnp.where(sc == 0.0, 0, key)
        if admissible is not None:
            key = jnp.where(admissible, key, INT_MIN)
        keys_ref[j] = key
        hi_ref[j] = (key >> 16).astype(I16)

    def score_body(j, carry):
        score_block(j, None)
        return carry

    lax.fori_loop(0, i, score_body, 0)
    score_block(i, diag_ok)

    n_acc = 4
    grp = 2 * SUBLANES

    def count16(ref, pred_fn):
        def inner(j, accs):
            accs = list(accs)
            for g in range(tk // grp):
                rows = ref[j, g * grp:(g + 1) * grp, :]
                a = accs[g % n_acc]
                accs[g % n_acc] = jnp.where(pred_fn(rows), a + 1, a)
            return tuple(accs)
        accs = lax.fori_loop(0, i + 1, inner, (jnp.zeros((grp, tq), I16),) * n_acc)
        acc = (accs[0] + accs[1]) + (accs[2] + accs[3])
        return jnp.sum(acc.astype(F32), axis=0, keepdims=True)

    def digit16(x):
        return jnp.broadcast_to(x, (grp, tq)).astype(I16)

    def radix16(ref, want):
        def body(it, prefix):
            cand_u = prefix | lax.shift_left(jnp.int32(1), 15 - it)
            cand = digit16(cand_u - 32768)
            cnt = count16(ref, lambda rows: rows >= cand)
            return jnp.where(cnt >= want, cand_u, prefix)
        return lax.fori_loop(0, 16, body, jnp.zeros((1, tq), I32)) - 32768

    t_hi = radix16(hi_ref, topk)
    t_hi16 = digit16(t_hi)
    n_above = count16(hi_ref, lambda rows: rows > t_hi16)

    def low_body(j, carry):
        key = keys_ref[j]
        lo = ((key ^ 0x8000) << 16) >> 16
        lo_ref[j] = jnp.where((key >> 16) == t_hi, lo, -32768).astype(I16)
        return carry

    lax.fori_loop(0, i + 1, low_body, 0)
    t_lo = radix16(lo_ref, topk - n_above)
    thr = (t_hi << 16) | (t_lo + 32768)

    def count(pred_fn):
        def inner(j, accs):
            accs = list(accs)
            for g in range(tk // SUBLANES):
                rows = keys_ref[j, g * SUBLANES:(g + 1) * SUBLANES, :]
                a = accs[g % n_acc]
                accs[g % n_acc] = jnp.where(pred_fn(rows), a + 1.0, a)
            return tuple(accs)
        accs = lax.fori_loop(0, i + 1, inner, (jnp.zeros((SUBLANES, tq), F32),) * n_acc)
        acc = (accs[0] + accs[1]) + (accs[2] + accs[3])
        return jnp.sum(acc, axis=0, keepdims=True)

    n_gt = count(lambda blk: blk > thr)
    need = jnp.where(thr == INT_MIN, 0.0, topk - n_gt)

    acc_ref[...] = jnp.zeros(acc_ref.shape, F32)
    q = q_ref[...]
    q_heads = ()
    for hp in range(H_C // 2):
        q_heads += _head_halves(q[:, hp * LANES:(hp + 1) * LANES])
    earlier = (lax.broadcasted_iota(I32, (tk, tk), 0)
               > lax.broadcasted_iota(I32, (tk, tk), 1))
    earlier = jnp.where(earlier, 1.0, 0.0).astype(BF16)

    def fold(x):
        return x.reshape(x.shape[0] // SUBLANES, SUBLANES, tq)

    ones = jnp.ones((2 * SUBLANES, tk), BF16)

    def attn_body(j, carry):
        tie_seen, ms, ls = carry
        start = pl.multiple_of(j * tk, tk)
        kblk = keys_ref[j]
        tie = kblk == thr
        tie_f = jnp.where(tie, 1.0, 0.0)
        rank = _dot(earlier, tie_f.astype(BF16)) + tie_seen
        sel = (kblk > thr) | (tie & (rank < need))
        b_ref[...] = jnp.where(sel, 0.0, NEG_MASK)
        mxs = []
        for h in range(H_C):
            hp = h // 2
            kb = k_ref[pl.ds(start, tk), hp * LANES:(hp + 1) * LANES]
            s = _dot_nt(kb, q_heads[h]) + b_ref[...]
            s_ref[h] = s
            mxs.append(jnp.max(fold(s), axis=0))
        new_m, new_l, alphas = [], [], []
        for h in range(H_C):
            m_new = jnp.maximum(ms[h], jnp.max(mxs[h], axis=0, keepdims=True))
            for r in range(tk // SLAB):
                p = jnp.exp2(s_ref[h, r * SLAB:(r + 1) * SLAB, :] - m_new)
                p_ref[h, r * SLAB:(r + 1) * SLAB, :] = p.astype(BF16)
            new_m.append(m_new)
            alphas.append(jnp.exp2(ms[h] - m_new))
        for h in range(H_C):
            rows = slice(h * HEAD_DIM, (h + 1) * HEAD_DIM)
            pv = _dot(jnp.concatenate([vt_ref[j, rows, :], ones], axis=0), p_ref[h])
            acc_ref[rows, :] = alphas[h] * acc_ref[rows, :] + pv[:HEAD_DIM]
            new_l.append(alphas[h] * ls[h] + pv[HEAD_DIM:HEAD_DIM + 1])
        return (tie_seen + jnp.sum(tie_f, axis=0, keepdims=True), tuple(new_m), tuple(new_l))

    m0 = jnp.full((1, tq), NEG_INIT, F32)
    l0 = jnp.zeros((1, tq), F32)
    _, _, ls = lax.fori_loop(0, i + 1, attn_body, (l0, (m0,) * H_C, (l0,) * H_C))

    for h in range(H_C):
        rows = slice(h * HEAD_DIM, (h + 1) * HEAD_DIM)
        acc_ref[rows, :] = acc_ref[rows, :] / ls[h]
    o_ref[...] = acc_ref[...].T.astype(o_ref.dtype)


def _sparse_attention(proj, vct, topk, b, s):
    tq = TQ
    wc = H_C * HEAD_DIM
    nkb = s // TK
    nq = s // tq
    kern = functools.partial(_sparse_kernel, topk=float(topk))
    return pl.pallas_call(
        kern,
        out_shape=jax.ShapeDtypeStruct((b * s, wc), BF16),
        grid=(b, nq),
        in_specs=[
            pl.BlockSpec((None, tq, wc), lambda bb, i: (PB_QC, bb * nq + i, 0)),
            pl.BlockSpec((None, s, wc), lambda bb, i: (PB_KC, bb, 0)),
            pl.BlockSpec((nkb, wc, TK), lambda bb, i: (bb, 0, 0)),
            pl.BlockSpec((None, tq, PROJ_TN), lambda bb, i: (PB_TAIL, bb * nq + i, 0)),
            pl.BlockSpec((None, s, LANES), lambda bb, i: (PB_TAIL, bb, TAIL_KI // LANES)),
        ],
        out_specs=pl.BlockSpec((tq, wc), lambda bb, i: (bb * nq + i, 0)),
        scratch_shapes=[
            pltpu.VMEM((nkb, TK, tq), I32),
            pltpu.VMEM((nkb, TK, tq), I16),
            pltpu.VMEM((nkb, TK, tq), I16),
            pltpu.VMEM((wc, tq), F32),
            pltpu.VMEM((H_C, TK, tq), F32),
            pltpu.VMEM((H_C, TK, tq), BF16),
            pltpu.VMEM((TK, tq), F32),
        ],
        compiler_params=_params(("arbitrary", "arbitrary"), VMEM_LIMIT),
        name="sparse_attn",
    )(proj, proj, vct, proj, proj)


def _mix_kernel(x_ref, oa_ref, ob_ref, oc_ref, ga_ref, gb_ref, gc_ref,
                ua_ref, ub_ref, uc_ref, wo_ref, o_ref):
    def gate(g_ref):
        return jnp.concatenate([g_ref[0], g_ref[1]], axis=1).astype(F32)

    y = gate(ga_ref) * _dot(oa_ref[...], ua_ref[...])
    y = y + gate(gb_ref) * _dot(ob_ref[...], ub_ref[...])
    y = y + gate(gc_ref) * _dot(oc_ref[...], uc_ref[...])
    o_ref[...] = x_ref[...] + _dot(y.astype(BF16), wo_ref[...])


def _mix_out(x2, oa, ob, oc, proj, ua, ub, uc, wo):
    t, d = x2.shape
    tm = 512
    per_gate = d // PROJ_TN
    gate_blk = PB_GATES // per_gate
    w = oa.shape[1]

    def full(shape):
        return pl.BlockSpec(shape, lambda i: (0, 0))

    return pl.pallas_call(
        _mix_kernel,
        out_shape=jax.ShapeDtypeStruct((t, d), F32),
        grid=(t // tm,),
        in_specs=[
            pl.BlockSpec((tm, d), lambda i: (i, 0)),
            pl.BlockSpec((tm, w), lambda i: (i, 0)),
            pl.BlockSpec((tm, w), lambda i: (i, 0)),
            pl.BlockSpec((tm, w), lambda i: (i, 0)),
            pl.BlockSpec((per_gate, tm, PROJ_TN), lambda i: (gate_blk, i, 0)),
            pl.BlockSpec((per_gate, tm, PROJ_TN), lambda i: (gate_blk + 1, i, 0)),
            pl.BlockSpec((per_gate, tm, PROJ_TN), lambda i: (gate_blk + 2, i, 0)),
            full((w, d)), full((w, d)), full((w, d)), full((d, d)),
        ],
        out_specs=pl.BlockSpec((tm, d), lambda i: (i, 0)),
        compiler_params=_params(("arbitrary",), VMEM_LIMIT),
        name="mix_out",
    )(x2, oa, ob, oc, proj, proj, proj, ua, ub, uc, wo)


def _mem_kv_kernel(mem_ref, g_ref, w_ref, o_ref):
    o_ref[...] = _dot(_rms(mem_ref[...], g_ref[...]).astype(BF16), w_ref[...]).astype(o_ref.dtype)


def _mem_kv(mem2, mem_norm, w_kv):
    r, d = mem2.shape
    depth, _, n = w_kv.shape
    tm = min(1024, r)
    return pl.pallas_call(
        _mem_kv_kernel,
        out_shape=jax.ShapeDtypeStruct((depth, r, n), BF16),
        grid=(depth, r // tm),
        in_specs=[
            pl.BlockSpec((tm, d), lambda l, i: (i, 0)),
            pl.BlockSpec((1, d), lambda l, i: (0, 0)),
            pl.BlockSpec((None, d, n), lambda l, i: (l, 0, 0)),
        ],
        out_specs=pl.BlockSpec((None, tm, n), lambda l, i: (l, i, 0)),
        compiler_params=_params(("arbitrary", "arbitrary"), VMEM_LIMIT),
        name="mem_kv",
    )(mem2, mem_norm, w_kv)


def _cross_kernel(x_ref, g_ref, wq_ref, kv_ref, wo_ref, o_ref):
    x = x_ref[...]
    tm = x.shape[0]
    hn = _rms(x, g_ref[...]).astype(BF16)
    q = _dot(hn, wq_ref[...]).astype(BF16)
    kv = kv_ref[...]
    wkv = H_X * HEAD_DIM
    lo = lax.broadcasted_iota(I32, (tm, LANES), 1) < HEAD_DIM
    pairs = []
    for hp in range(H_X // 2):
        kp = kv[:, hp * LANES:(hp + 1) * LANES]
        vp = kv[:, wkv + hp * LANES:wkv + (hp + 1) * LANES]
        outs = []
        for qm in _head_halves(q[:, hp * LANES:(hp + 1) * LANES]):
            s = _dot_nt(qm, kp)
            m = jnp.max(s, axis=-1, keepdims=True)
            p = jnp.exp(s - m)
            l = jnp.sum(p, axis=-1, keepdims=True)
            outs.append(_dot(p.astype(BF16), vp) / l)
        pairs.append(jnp.where(lo, outs[0], outs[1]).astype(BF16))
    o = jnp.concatenate(pairs, axis=1)
    o_ref[...] = x + _dot(o, wo_ref[...])


def _cross_attention(x2, gain, wq, kv, wo, seq):
    t, d = x2.shape
    tm = min(512, seq)
    per_batch = seq // tm
    n_mem, wkv2 = kv.shape[1], kv.shape[2]
    wq_cols = wq.shape[1]
    return pl.pallas_call(
        _cross_kernel,
        out_shape=jax.ShapeDtypeStruct((t, d), F32),
        grid=(t // tm,),
        in_specs=[
            pl.BlockSpec((tm, d), lambda i: (i, 0)),
            pl.BlockSpec((1, d), lambda i: (0, 0)),
            pl.BlockSpec((d, wq_cols), lambda i: (0, 0)),
            pl.BlockSpec((None, n_mem, wkv2), lambda i: (i // per_batch, 0, 0)),
            pl.BlockSpec((wq_cols, d), lambda i: (0, 0)),
        ],
        out_specs=pl.BlockSpec((tm, d), lambda i: (i, 0)),
        compiler_params=_params(("arbitrary",), VMEM_LIMIT),
        name="cross_attn",
    )(x2, gain, wq, kv, wo)


def _ffn_kernel(x_ref, g_ref, wgu_ref, wd_ref, fg_ref, o_ref, acc_ref, *, final_norm, tf):
    x = x_ref[...]
    hn = _rms(x, g_ref[...]).astype(BF16)
    d_ff = wd_ref.shape[0]
    for c in range(d_ff // tf):
        gate = _dot(hn, wgu_ref[:, c * tf:(c + 1) * tf])
        up = _dot(hn, wgu_ref[:, d_ff + c * tf:d_ff + (c + 1) * tf])
        h = (gate / (1.0 + jnp.exp(-gate))) * up
        part = _dot(h.astype(BF16), wd_ref[c * tf:(c + 1) * tf, :])
        if c == 0:
            acc_ref[...] = part
        else:
            acc_ref[...] += part
    y = x + acc_ref[...]
    if final_norm:
        y = _rms(y, fg_ref[...])
    o_ref[...] = y


def _ffn(x2, gain, w_gu, w_down, final_gain, final_norm):
    t, d = x2.shape
    d_ff = w_down.shape[0]
    tm = min(512, t)
    tf = 2 * LANES
    assert d_ff % tf == 0
    kern = functools.partial(_ffn_kernel, final_norm=final_norm, tf=tf)
    resident = pl.Buffered(1)
    return pl.pallas_call(
        kern,
        out_shape=jax.ShapeDtypeStruct((t, d), F32),
        grid=(t // tm,),
        in_specs=[
            pl.BlockSpec((tm, d), lambda i: (i, 0)),
            pl.BlockSpec((1, d), lambda i: (0, 0)),
            pl.BlockSpec((d, 2 * d_ff), lambda i: (0, 0), pipeline_mode=resident),
            pl.BlockSpec((d_ff, d), lambda i: (0, 0), pipeline_mode=resident),
            pl.BlockSpec((1, d), lambda i: (0, 0)),
        ],
        out_specs=pl.BlockSpec((tm, d), lambda i: (i, 0)),
        scratch_shapes=[pltpu.VMEM((tm, d), F32)],
        compiler_params=_params(("arbitrary",), VMEM_LIMIT),
        name="ffn",
    )(x2, gain, w_gu, w_down, final_gain)


def _layout_in_weights(w_in_l):
    d = w_in_l.shape[0]
    w_a, w_b, w_c = H_A * HEAD_DIM, H_B * 2 * HEAD_DIM, H_C * HEAD_DIM
    splits = (w_a, w_a, w_a, w_b, w_b, w_b, w_c, w_c, w_c, H_IDX * D_IDX, D_IDX, H_IDX, 3 * d)
    offs = [int(o) for o in np.cumsum(splits)[:-1]]
    qa, ka, va, qb, kb, vb, qc, kc, vc, qi, ki, wi, gates = jnp.split(w_in_l, offs, axis=1)
    scale = HEAD_DIM ** -0.5
    scale2 = scale * math.log2(math.e)
    qa, qb, qc, qi = qa * scale, qb * scale2, qc * scale2, qi * (D_IDX ** -0.5)
    pad = jnp.zeros((d, LANES - H_IDX), w_in_l.dtype)
    tail = jnp.concatenate([qi, ki, ki, wi, pad], axis=1)
    w_main = jnp.concatenate([qb, kb, qc, kc, tail, qa, ka, va, gates], axis=1)
    assert w_main.shape[1] == PB_COUNT * PROJ_TN
    w_blocks = w_main.reshape(d, PB_COUNT, PROJ_TN).transpose(1, 0, 2)
    return w_blocks.astype(BF16), vb.T.astype(BF16), vc.T.astype(BF16)


def _rope_tables(seq):
    pos = jnp.arange(seq, dtype=F32)
    inv = ROPE_THETA ** (-jnp.arange(0, HEAD_DIM, 2, dtype=F32) / HEAD_DIM)
    ang = pos[:, None] * inv[None, :]
    cos, sin = jnp.cos(ang), jnp.sin(ang)
    zero = jnp.zeros_like(sin)
    reps = LANES // HEAD_DIM
    cos_t = jnp.tile(jnp.concatenate([cos, cos], axis=1), (1, reps))
    sina_t = jnp.tile(jnp.concatenate([-sin, zero], axis=1), (1, reps))
    sinb_t = jnp.tile(jnp.concatenate([zero, sin], axis=1), (1, reps))
    return cos_t, sina_t, sinb_t


def kernel(x, mem, norm_mix, w_in, rel_bias_a, lambda_vecs, subln_b, w_up_a, w_up_b, w_up_c,
           w_out, norm_cross, w_q_x, w_kv_x, w_o_x, norm_ffn, w_gu, w_down, mem_norm, final_norm):
    b, s, d = x.shape
    depth = w_in.shape[0]
    assert s % TQ_B == 0 and LEFT_CHUNKS * CHUNK == 2 * TQ and TQ == TK
    topk = min(TOPK_MAX, s // 4)
    cos_t, sina_t, sinb_t = _rope_tables(s)
    x2 = x.reshape(b * s, d)
    mem2 = mem.reshape(b * mem.shape[1], d)
    kv_all = _mem_kv(mem2, mem_norm.reshape(1, d), w_kv_x.astype(BF16))
    kv_all = kv_all.reshape(depth, b, mem.shape[1], -1)
    for l in range(depth):
        w_main, w_vbt, w_vct = _layout_in_weights(w_in[l])
        proj, vbt, vct = _in_proj(x2, norm_mix[l].reshape(1, d), w_main, cos_t, sina_t, sinb_t,
                                  w_vbt, w_vct, s)
        lam_init = 0.8 - 0.6 * math.exp(-0.3 * l)
        o_a = _band_attention(proj, _band_bias(rel_bias_a[l], TQ), b, s)
        o_b = _diff_attention(proj, vbt, lambda_vecs[l].astype(F32),
                              subln_b[l].reshape(LANES, 1), lam_init, b, s)
        o_c = _sparse_attention(proj, vct, topk, b, s)
        x2 = _mix_out(x2, o_a, o_b, o_c,
                      proj, w_up_a[l].astype(BF16), w_up_b[l].astype(BF16),
                      w_up_c[l].astype(BF16), w_out[l].astype(BF16))
        x2 = _cross_attention(x2, norm_cross[l].reshape(1, d),
                              (w_q_x[l] * HEAD_DIM ** -0.5).astype(BF16), kv_all[l],
                              w_o_x[l].astype(BF16), s)
        x2 = _ffn(x2, norm_ffn[l].reshape(1, d), w_gu[l].astype(BF16), w_down[l].astype(BF16),
                  final_norm.reshape(1, d), final_norm=(l == depth - 1))
    return x2.reshape(b, s, d)
```

```python
import functools
import math

import jax
import jax.numpy as jnp
import numpy as np
from jax import lax
from jax.experimental import pallas as pl
from jax.experimental.pallas import tpu as pltpu

F32 = jnp.float32
BF16 = jnp.bfloat16
I32 = jnp.int32
I16 = jnp.int16

CHUNK = 64
HEAD_DIM = 64
H_A = 8
LEFT_CHUNKS = 8
REL_MAX = 256
H_B = 4
H_C = 8
H_IDX = 4
D_IDX = 64
TOPK_MAX = 256
H_X = 4
ROPE_THETA = 10000.0
EPS = 1e-6

LANES = 128
SUBLANES = 8
TQ = 256
TK = 256
TQ_B = 512
SLAB = 128
PROJ_TN = 512
VMEM_LIMIT = 56 * 1024 * 1024

NEG_INIT = -1e30
NEG_MASK = -2e30
INT_MIN = -(2 ** 31)

PB_GATES = 0
PB_QB, PB_KB, PB_QC, PB_KC, PB_TAIL = 6, 7, 8, 9, 10
PB_QA, PB_KA = 11, 12
PB_COUNT = 13
TAIL_QI, TAIL_KI, TAIL_WI = 0, 256, 384


def _dot(a, b):
    return jnp.dot(a, b, preferred_element_type=F32)


def _dot_nt(a, b):
    return lax.dot_general(a, b, (((1,), (1,)), ((), ())), preferred_element_type=F32)


def _rms(x, g):
    ms = jnp.mean(x * x, axis=-1, keepdims=True)
    return (x * lax.rsqrt(ms + EPS)) * g


def _params(sem, vmem=None):
    return pltpu.CompilerParams(dimension_semantics=sem, vmem_limit_bytes=vmem)


def _head_halves(pair):
    lo = lax.broadcasted_iota(I32, pair.shape, 1) < HEAD_DIM
    zero = jnp.zeros_like(pair)
    return jnp.where(lo, pair, zero), jnp.where(lo, zero, pair)


def _in_proj_kernel(x_ref, g_ref, w_ref, cos_ref, sina_ref, sinb_ref, wva_ref, wvb_ref, wvc_ref,
                    o_ref, vat_ref, vbt_ref, vct_ref):
    xn = _rms(x_ref[...], g_ref[...]).astype(BF16)
    for wv_ref, vt_ref in ((wva_ref, vat_ref), (wvb_ref, vbt_ref), (wvc_ref, vct_ref)):
        vt = _dot_nt(wv_ref[...], xn).astype(BF16)
        w = vt_ref.shape[2]
        for c in range(vt_ref.shape[0]):
            vt_ref[c] = vt[:, c * w:(c + 1) * w]

    tn = o_ref.shape[2]
    half = HEAD_DIM // 2
    for j in range(PB_COUNT):
        a = _dot(xn, w_ref[:, j * tn:(j + 1) * tn])
        if PB_QB <= j <= PB_TAIL:
            for c in range(tn // LANES):
                sl = slice(c * LANES, (c + 1) * LANES)
                ac = a[:, sl]
                if j == PB_TAIL and c * LANES >= TAIL_WI:
                    o_ref[j, :, sl] = ac.astype(o_ref.dtype)
                else:
                    rot = (pltpu.roll(ac, LANES - half, 1) * sina_ref[...]
                           + pltpu.roll(ac, half, 1) * sinb_ref[...])
                    o_ref[j, :, sl] = (ac * cos_ref[...] + rot).astype(o_ref.dtype)
        elif j >= PB_QA:
            o_ref[j] = a.astype(o_ref.dtype)
        else:
            o_ref[j] = (1.0 / (1.0 + jnp.exp(-a))).astype(o_ref.dtype)


def _in_proj(x2, gain, w_main, cos_t, sina_t, sinb_t, w_vat, w_vbt, w_vct, seq):
    t, d = x2.shape
    tm = min(512, seq)
    tn = PROJ_TN
    pos_blocks = seq // tm
    wa, wb, wc = w_vat.shape[0], w_vbt.shape[0], w_vct.shape[0]
    resident = pl.Buffered(1)
    return pl.pallas_call(
        _in_proj_kernel,
        out_shape=(
            jax.ShapeDtypeStruct((PB_COUNT, t, tn), BF16),
            jax.ShapeDtypeStruct((t // TQ, wa, TQ), BF16),
            jax.ShapeDtypeStruct((t // TQ_B, wb, TQ_B), BF16),
            jax.ShapeDtypeStruct((t // TK, wc, TK), BF16),
        ),
        grid=(t // tm,),
        in_specs=[
            pl.BlockSpec((tm, d), lambda i: (i, 0)),
            pl.BlockSpec((1, d), lambda i: (0, 0)),
            pl.BlockSpec((d, PB_COUNT * tn), lambda i: (0, 0), pipeline_mode=resident),
            pl.BlockSpec((tm, LANES), lambda i: (i % pos_blocks, 0)),
            pl.BlockSpec((tm, LANES), lambda i: (i % pos_blocks, 0)),
            pl.BlockSpec((tm, LANES), lambda i: (i % pos_blocks, 0)),
            pl.BlockSpec((wa, d), lambda i: (0, 0), pipeline_mode=resident),
            pl.BlockSpec((wb, d), lambda i: (0, 0), pipeline_mode=resident),
            pl.BlockSpec((wc, d), lambda i: (0, 0), pipeline_mode=resident),
        ],
        out_specs=(
            pl.BlockSpec((PB_COUNT, tm, tn), lambda i: (0, i, 0)),
            pl.BlockSpec((tm // TQ, wa, TQ), lambda i: (i, 0, 0)),
            pl.BlockSpec((tm // TQ_B, wb, TQ_B), lambda i: (i, 0, 0)),
            pl.BlockSpec((tm // TK, wc, TK), lambda i: (i, 0, 0)),
        ),
        compiler_params=_params(("arbitrary",), VMEM_LIMIT),
        name="in_proj",
    )(x2, gain, w_main, cos_t, sina_t, sinb_t, w_vat, w_vbt, w_vct)


def _band_kernel(q_ref, k0_ref, k1_ref, k2_ref, vt0_ref, vt1_ref, vt2_ref, bias_ref, o_ref,
                 acc_ref, s_ref, p_ref):
    tq = q_ref.shape[0]
    ones = jnp.ones((2 * SUBLANES, 3 * tq), BF16)

    def fold(x):
        return x.reshape(x.shape[0] // SUBLANES, SUBLANES, tq)

    ms = []
    for h in range(H_A):
        sl = slice((h // 2) * LANES, (h // 2 + 1) * LANES)
        kw = jnp.concatenate([k0_ref[:, sl], k1_ref[:, sl], k2_ref[:, sl]], axis=0)
        s = _dot_nt(kw, _head_halves(q_ref[:, sl])[h % 2]) + bias_ref[h]
        s_ref[h] = s
        ms.append(jnp.max(jnp.max(fold(s), axis=0), axis=0, keepdims=True))
    for h in range(H_A):
        for r in range(3 * tq // SLAB):
            p_ref[h, r * SLAB:(r + 1) * SLAB, :] = jnp.exp2(
                s_ref[h, r * SLAB:(r + 1) * SLAB, :] - ms[h]).astype(BF16)
    for h in range(H_A):
        rows = slice(h * HEAD_DIM, (h + 1) * HEAD_DIM)
        vt = jnp.concatenate([vt0_ref[0, rows, :], vt1_ref[0, rows, :], vt2_ref[0, rows, :]], axis=1)
        pv = _dot(jnp.concatenate([vt, ones], axis=0), p_ref[h])
        acc_ref[rows, :] = pv[:HEAD_DIM] / pv[HEAD_DIM:HEAD_DIM + 1]
    o_ref[...] = acc_ref[...].T.astype(o_ref.dtype)


def _band_bias(rel_bias, tq):
    back = LEFT_CHUNKS * CHUNK
    assert back == 2 * tq
    width = 4 * tq
    dist = back + tq - 1 - np.arange(width - 1)
    g = rel_bias[:, np.clip(dist, -(CHUNK - 1), REL_MAX) + (CHUNK - 1)].astype(F32)
    g = jnp.concatenate([g, jnp.zeros((g.shape[0], 1), F32)], axis=1) * math.log2(math.e)
    g = jnp.roll(g, -(tq - 1), axis=1)
    flat = jnp.tile(g, (1, tq))[:, :tq * (width - 1)]
    bias = flat.reshape(-1, tq, width - 1)[:, :, :3 * tq]
    qi = np.arange(tq)[:, None] + back
    kj = np.arange(3 * tq)[None, :]
    dc = qi // CHUNK - kj // CHUNK
    in_band = (dc >= 0) & (dc <= LEFT_CHUNKS)
    valid = np.stack([in_band & (kj >= (2 - v) * tq) for v in range(3)])
    bias = jnp.where(jnp.asarray(valid)[:, None], bias[None], NEG_MASK)
    return bias.transpose(0, 1, 3, 2)


def _band_attention(proj, vat, bias, b, s):
    tq = TQ
    nq = s // tq
    wa = H_A * HEAD_DIM

    def wspec(blk, off):
        return pl.BlockSpec((None, tq, wa),
                            lambda bb, i: (blk, bb * nq + jnp.maximum(i + off, 0), 0))

    def vspec(off):
        return pl.BlockSpec((1, wa, tq), lambda bb, i: (bb * nq + jnp.maximum(i + off, 0), 0, 0))

    return pl.pallas_call(
        _band_kernel,
        out_shape=jax.ShapeDtypeStruct((b * s, wa), BF16),
        grid=(b, nq),
        in_specs=[
            wspec(PB_QA, 0),
            wspec(PB_KA, -2), wspec(PB_KA, -1), wspec(PB_KA, 0),
            vspec(-2), vspec(-1), vspec(0),
            pl.BlockSpec((None, H_A, 3 * tq, tq), lambda bb, i: (jnp.minimum(i, 2), 0, 0, 0)),
        ],
        out_specs=pl.BlockSpec((tq, wa), lambda bb, i: (bb * nq + i, 0)),
        scratch_shapes=[
            pltpu.VMEM((wa, tq), F32),
            pltpu.VMEM((H_A, 3 * tq, tq), F32),
            pltpu.VMEM((H_A, 3 * tq, tq), BF16),
        ],
        compiler_params=_params(("arbitrary", "arbitrary"), VMEM_LIMIT),
        name="band_attn",
    )(proj, proj, proj, proj, vat, vat, vat, bias)


def _diff_kernel(q_ref, k_ref, vt_ref, lv_ref, sub_ref, o_ref, acc_ref, s_ref, p_ref, *,
                 lam_init):
    i = pl.program_id(1)
    tq = q_ref.shape[0]
    tk = vt_ref.shape[2]
    n_maps = 2 * H_B
    q = q_ref[...]
    qms = ()
    for h in range(H_B):
        qms += _head_halves(q[:, h * LANES:(h + 1) * LANES])
    acc_ref[...] = jnp.zeros(acc_ref.shape, F32)
    ck = lax.broadcasted_iota(I32, (tk, tq), 0) // CHUNK
    cq = lax.broadcasted_iota(I32, (tk, tq), 1) // CHUNK
    diag_ok = ck <= cq

    def fold(x):
        return x.reshape(x.shape[0] // SUBLANES, SUBLANES, tq)

    ones = jnp.ones((2 * SUBLANES, tk), BF16)

    def block(j, stats, mask):
        start = pl.multiple_of(j * tk, tk)
        mxs = []
        for c in range(n_maps):
            h = c // 2
            s = _dot_nt(k_ref[pl.ds(start, tk), h * LANES:(h + 1) * LANES], qms[c])
            if mask is not None:
                s = jnp.where(mask, s, NEG_MASK)
            s_ref[c] = s
            mxs.append(jnp.max(fold(s), axis=0))
        ms, ls = stats
        new_m, new_l, alphas = [], [], []
        for c in range(n_maps):
            m_new = jnp.maximum(ms[c], jnp.max(mxs[c], axis=0, keepdims=True))
            for r in range(tk // SLAB):
                p = jnp.exp2(s_ref[c, r * SLAB:(r + 1) * SLAB, :] - m_new)
                p_ref[c, r * SLAB:(r + 1) * SLAB, :] = p.astype(BF16)
            alpha = jnp.exp2(ms[c] - m_new)
            new_m.append(m_new)
            vt = jnp.concatenate([vt_ref[j, (c // 2) * LANES:(c // 2 + 1) * LANES, :], ones], axis=0)
            pv = _dot(vt, p_ref[c])
            acc_ref[c] = alpha * acc_ref[c] + pv[:LANES]
            new_l.append(alpha * ls[c] + pv[LANES:LANES + 1])
        return tuple(new_m), tuple(new_l)

    m0 = jnp.full((1, tq), NEG_INIT, F32)
    l0 = jnp.zeros((1, tq), F32)
    stats = lax.fori_loop(0, i, lambda j, st: block(j, st, None),
                          ((m0,) * n_maps, (l0,) * n_maps))
    _, ls = block(i, stats, diag_ok)

    lv = lv_ref[...]
    lam = (jnp.exp(jnp.sum(lv[0:1] * lv[1:2], axis=-1, keepdims=True))
           - jnp.exp(jnp.sum(lv[2:3] * lv[3:4], axis=-1, keepdims=True)) + lam_init)
    for h in range(H_B):
        o = acc_ref[2 * h] / ls[2 * h] - lam * (acc_ref[2 * h + 1] / ls[2 * h + 1])
        ms = jnp.mean(o * o, axis=0, keepdims=True)
        o = (o * lax.rsqrt(ms + EPS)) * sub_ref[...] * (1.0 - lam_init)
        o_ref[:, h * LANES:(h + 1) * LANES] = o.T.astype(o_ref.dtype)


def _diff_attention(proj, vbt, lambda_vec, subln_col, lam_init, b, s):
    tq = TQ_B
    nkb = s // tq
    wb = H_B * 2 * HEAD_DIM
    kern = functools.partial(_diff_kernel, lam_init=lam_init)
    return pl.pallas_call(
        kern,
        out_shape=jax.ShapeDtypeStruct((b * s, wb), BF16),
        grid=(b, nkb),
        in_specs=[
            pl.BlockSpec((None, tq, wb), lambda bb, i: (PB_QB, bb * nkb + i, 0)),
            pl.BlockSpec((None, s, wb), lambda bb, i: (PB_KB, bb, 0)),
            pl.BlockSpec((nkb, wb, tq), lambda bb, i: (bb, 0, 0)),
            pl.BlockSpec((4, HEAD_DIM), lambda bb, i: (0, 0)),
            pl.BlockSpec((LANES, 1), lambda bb, i: (0, 0)),
        ],
        out_specs=pl.BlockSpec((tq, wb), lambda bb, i: (bb * nkb + i, 0)),
        scratch_shapes=[
            pltpu.VMEM((2 * H_B, LANES, tq), F32),
            pltpu.VMEM((2 * H_B, tq, tq), F32),
            pltpu.VMEM((2 * H_B, tq, tq), BF16),
        ],
        compiler_params=_params(("arbitrary", "arbitrary"), VMEM_LIMIT),
        name="diff_attn",
    )(proj, proj, vbt, lambda_vec, subln_col)


def _sparse_kernel(q_ref, k_ref, vt_ref, tail_ref, ki_ref, o_ref,
                   keys_ref, hi_ref, lo_ref, acc_ref, s_ref, p_ref, b_ref, *, topk):
    i = pl.program_id(1)
    tq = q_ref.shape[0]
    tk = vt_ref.shape[2]

    wi = tail_ref[:, TAIL_WI:TAIL_WI + LANES]
    wt = wi.astype(F32).T * (H_IDX ** -0.5)
    w_rows = [wt[h:h + 1, :] for h in range(H_IDX)]
    qi_heads = (_head_halves(tail_ref[:, TAIL_QI:TAIL_QI + LANES])
                + _head_halves(tail_ref[:, TAIL_QI + LANES:TAIL_QI + 2 * LANES]))
    ck = lax.broadcasted_iota(I32, (tk, tq), 0) // CHUNK
    cq = lax.broadcasted_iota(I32, (tk, tq), 1) // CHUNK
    diag_ok = ck <= cq

    def score_block(j, admissible):
        start = pl.multiple_of(j * tk, tk)
        kk = ki_ref[pl.ds(start, tk), :]
        for h in range(H_IDX):
            s_ref[h] = _dot_nt(kk, qi_heads[h])
        sc = jnp.zeros((tk, tq), F32)
        for h in range(H_IDX):
            sc = sc + jnp.maximum(s_ref[h], 0.0) * w_rows[h]
        bits = pltpu.bitcast(sc, I32)
        key = bits ^ ((bits >> 31) & 0x7FFFFFFF)
        key = jnp.where(sc == 0.0, 0, key)
        if admissible is not None:
            key = jnp.where(admissible, key, INT_MIN)
        keys_ref[j] = key
        hi_ref[j] = (key >> 16).astype(I16)

    def score_body(j, carry):
        score_block(j, None)
        return carry

    lax.fori_loop(0, i, score_body, 0)
    score_block(i, diag_ok)

    n_acc = 4
    grp = 2 * SUBLANES

    def count16(ref, pred_fn):
        def inner(j, accs):
            accs = list(accs)
            for g in range(tk // grp):
                rows = ref[j, g * grp:(g + 1) * grp, :]
                a = accs[g % n_acc]
                accs[g % n_acc] = jnp.where(pred_fn(rows), a + 1, a)
            return tuple(accs)
        accs = lax.fori_loop(0, i + 1, inner, (jnp.zeros((grp, tq), I16),) * n_acc)
        acc = (accs[0] + accs[1]) + (accs[2] + accs[3])
        return jnp.sum(acc.astype(F32), axis=0, keepdims=True)

    def digit16(x):
        return jnp.broadcast_to(x, (grp, tq)).astype(I16)

    def radix16(ref, want):
        def body(it, prefix):
            cand_u = prefix | lax.shift_left(jnp.int32(1), 15 - it)
            cand = digit16(cand_u - 32768)
            cnt = count16(ref, lambda rows: rows >= cand)
            return jnp.where(cnt >= want, cand_u, prefix)
        return lax.fori_loop(0, 16, body, jnp.zeros((1, tq), I32)) - 32768

    t_hi = radix16(hi_ref, topk)
    t_hi16 = digit16(t_hi)
    n_above = count16(hi_ref, lambda rows: rows > t_hi16)

    def low_body(j, carry):
        key = keys_ref[j]
        lo = ((key ^ 0x8000) << 16) >> 16
        lo_ref[j] = jnp.where((key >> 16) == t_hi, lo, -32768).astype(I16)
        return carry

    lax.fori_loop(0, i + 1, low_body, 0)
    t_lo = radix16(lo_ref, topk - n_above)
    thr = (t_hi << 16) | (t_lo + 32768)

    def count(pred_fn):
        def inner(j, accs):
            accs = list(accs)
            for g in range(tk // SUBLANES):
                rows = keys_ref[j, g * SUBLANES:(g + 1) * SUBLANES, :]
                a = accs[g % n_acc]
                accs[g % n_acc] = jnp.where(pred_fn(rows), a + 1.0, a)
            return tuple(accs)
        accs = lax.fori_loop(0, i + 1, inner, (jnp.zeros((SUBLANES, tq), F32),) * n_acc)
        acc = (accs[0] + accs[1]) + (accs[2] + accs[3])
        return jnp.sum(acc, axis=0, keepdims=True)

    n_gt = count(lambda blk: blk > thr)
    need = jnp.where(thr == INT_MIN, 0.0, topk - n_gt)

    acc_ref[...] = jnp.zeros(acc_ref.shape, F32)
    q = q_ref[...]
    q_heads = ()
    for hp in range(H_C // 2):
        q_heads += _head_halves(q[:, hp * LANES:(hp + 1) * LANES])
    earlier = (lax.broadcasted_iota(I32, (tk, tk), 0)
               > lax.broadcasted_iota(I32, (tk, tk), 1))
    earlier = jnp.where(earlier, 1.0, 0.0).astype(BF16)

    def fold(x):
        return x.reshape(x.shape[0] // SUBLANES, SUBLANES, tq)

    ones = jnp.ones((2 * SUBLANES, tk), BF16)

    def attn_body(j, carry):
        tie_seen, ms, ls = carry
        start = pl.multiple_of(j * tk, tk)
        kblk = keys_ref[j]
        tie = kblk == thr
        tie_f = jnp.where(tie, 1.0, 0.0)
        rank = _dot(earlier, tie_f.astype(BF16)) + tie_seen
        sel = (kblk > thr) | (tie & (rank < need))
        b_ref[...] = jnp.where(sel, 0.0, NEG_MASK)
        mxs = []
        for h in range(H_C):
            hp = h // 2
            kb = k_ref[pl.ds(start, tk), hp * LANES:(hp + 1) * LANES]
            s = _dot_nt(kb, q_heads[h]) + b_ref[...]
            s_ref[h] = s
            mxs.append(jnp.max(fold(s), axis=0))
        new_m, new_l, alphas = [], [], []
        for h in range(H_C):
            m_new = jnp.maximum(ms[h], jnp.max(mxs[h], axis=0, keepdims=True))
            for r in range(tk // SLAB):
                p = jnp.exp2(s_ref[h, r * SLAB:(r + 1) * SLAB, :] - m_new)
                p_ref[h, r * SLAB:(r + 1) * SLAB, :] = p.astype(BF16)
            new_m.append(m_new)
            alphas.append(jnp.exp2(ms[h] - m_new))
        for h in range(H_C):
            rows = slice(h * HEAD_DIM, (h + 1) * HEAD_DIM)
            pv = _dot(jnp.concatenate([vt_ref[j, rows, :], ones], axis=0), p_ref[h])
            acc_ref[rows, :] = alphas[h] * acc_ref[rows, :] + pv[:HEAD_DIM]
            new_l.append(alphas[h] * ls[h] + pv[HEAD_DIM:HEAD_DIM + 1])
        return (tie_seen + jnp.sum(tie_f, axis=0, keepdims=True), tuple(new_m), tuple(new_l))

    m0 = jnp.full((1, tq), NEG_INIT, F32)
    l0 = jnp.zeros((1, tq), F32)
    _, _, ls = lax.fori_loop(0, i + 1, attn_body, (l0, (m0,) * H_C, (l0,) * H_C))

    for h in range(H_C):
        rows = slice(h * HEAD_DIM, (h + 1) * HEAD_DIM)
        acc_ref[rows, :] = acc_ref[rows, :] / ls[h]
    o_ref[...] = acc_ref[...].T.astype(o_ref.dtype)


def _sparse_attention(proj, vct, topk, b, s):
    tq = TQ
    wc = H_C * HEAD_DIM
    nkb = s // TK
    nq = s // tq
    kern = functools.partial(_sparse_kernel, topk=float(topk))
    return pl.pallas_call(
        kern,
        out_shape=jax.ShapeDtypeStruct((b * s, wc), BF16),
        grid=(b, nq),
        in_specs=[
            pl.BlockSpec((None, tq, wc), lambda bb, i: (PB_QC, bb * nq + i, 0)),
            pl.BlockSpec((None, s, wc), lambda bb, i: (PB_KC, bb, 0)),
            pl.BlockSpec((nkb, wc, TK), lambda bb, i: (bb, 0, 0)),
            pl.BlockSpec((None, tq, PROJ_TN), lambda bb, i: (PB_TAIL, bb * nq + i, 0)),
            pl.BlockSpec((None, s, LANES), lambda bb, i: (PB_TAIL, bb, TAIL_KI // LANES)),
        ],
        out_specs=pl.BlockSpec((tq, wc), lambda bb, i: (bb * nq + i, 0)),
        scratch_shapes=[
            pltpu.VMEM((nkb, TK, tq), I32),
            pltpu.VMEM((nkb, TK, tq), I16),
            pltpu.VMEM((nkb, TK, tq), I16),
            pltpu.VMEM((wc, tq), F32),
            pltpu.VMEM((H_C, TK, tq), F32),
            pltpu.VMEM((H_C, TK, tq), BF16),
            pltpu.VMEM((TK, tq), F32),
        ],
        compiler_params=_params(("arbitrary", "arbitrary"), VMEM_LIMIT),
        name="sparse_attn",
    )(proj, proj, vct, proj, proj)


def _mix_kernel(x_ref, oa_ref, ob_ref, oc_ref, ga_ref, gb_ref, gc_ref,
                ua_ref, ub_ref, uc_ref, wo_ref, o_ref):
    def gate(g_ref):
        return jnp.concatenate([g_ref[0], g_ref[1]], axis=1).astype(F32)

    y = gate(ga_ref) * _dot(oa_ref[...], ua_ref[...])
    y = y + gate(gb_ref) * _dot(ob_ref[...], ub_ref[...])
    y = y + gate(gc_ref) * _dot(oc_ref[...], uc_ref[...])
    o_ref[...] = x_ref[...] + _dot(y.astype(BF16), wo_ref[...])


def _mix_out(x2, oa, ob, oc, proj, ua, ub, uc, wo):
    t, d = x2.shape
    tm = 512
    per_gate = d // PROJ_TN
    assert PB_GATES % per_gate == 0
    gate_blk = PB_GATES // per_gate
    w = oa.shape[1]

    def full(shape):
        return pl.BlockSpec(shape, lambda i: (0, 0))

    return pl.pallas_call(
        _mix_kernel,
        out_shape=jax.ShapeDtypeStruct((t, d), F32),
        grid=(t // tm,),
        in_specs=[
            pl.BlockSpec((tm, d), lambda i: (i, 0)),
            pl.BlockSpec((tm, w), lambda i: (i, 0)),
            pl.BlockSpec((tm, w), lambda i: (i, 0)),
            pl.BlockSpec((tm, w), lambda i: (i, 0)),
            pl.BlockSpec((per_gate, tm, PROJ_TN), lambda i: (gate_blk, i, 0)),
            pl.BlockSpec((per_gate, tm, PROJ_TN), lambda i: (gate_blk + 1, i, 0)),
            pl.BlockSpec((per_gate, tm, PROJ_TN), lambda i: (gate_blk + 2, i, 0)),
            full((w, d)), full((w, d)), full((w, d)), full((d, d)),
        ],
        out_specs=pl.BlockSpec((tm, d), lambda i: (i, 0)),
        compiler_params=_params(("arbitrary",), VMEM_LIMIT),
        name="mix_out",
    )(x2, oa, ob, oc, proj, proj, proj, ua, ub, uc, wo)


def _mem_kv_kernel(mem_ref, g_ref, w_ref, o_ref):
    o_ref[...] = _dot(_rms(mem_ref[...], g_ref[...]).astype(BF16), w_ref[...]).astype(o_ref.dtype)


def _mem_kv(mem2, mem_norm, w_kv):
    r, d = mem2.shape
    depth, _, n = w_kv.shape
    tm = min(1024, r)
    return pl.pallas_call(
        _mem_kv_kernel,
        out_shape=jax.ShapeDtypeStruct((depth, r, n), BF16),
        grid=(depth, r // tm),
        in_specs=[
            pl.BlockSpec((tm, d), lambda l, i: (i, 0)),
            pl.BlockSpec((1, d), lambda l, i: (0, 0)),
            pl.BlockSpec((None, d, n), lambda l, i: (l, 0, 0)),
        ],
        out_specs=pl.BlockSpec((None, tm, n), lambda l, i: (l, i, 0)),
        compiler_params=_params(("arbitrary", "arbitrary"), VMEM_LIMIT),
        name="mem_kv",
    )(mem2, mem_norm, w_kv)


def _cross_kernel(x_ref, g_ref, wq_ref, kv_ref, wo_ref, o_ref):
    x = x_ref[...]
    tm = x.shape[0]
    hn = _rms(x, g_ref[...]).astype(BF16)
    q = _dot(hn, wq_ref[...]).astype(BF16)
    kv = kv_ref[...]
    wkv = H_X * HEAD_DIM
    lo = lax.broadcasted_iota(I32, (tm, LANES), 1) < HEAD_DIM
    pairs = []
    for hp in range(H_X // 2):
        kp = kv[:, hp * LANES:(hp + 1) * LANES]
        vp = kv[:, wkv + hp * LANES:wkv + (hp + 1) * LANES]
        outs = []
        for qm in _head_halves(q[:, hp * LANES:(hp + 1) * LANES]):
            s = _dot_nt(qm, kp)
            m = jnp.max(s, axis=-1, keepdims=True)
            p = jnp.exp(s - m)
            l = jnp.sum(p, axis=-1, keepdims=True)
            outs.append(_dot(p.astype(BF16), vp) / l)
        pairs.append(jnp.where(lo, outs[0], outs[1]).astype(BF16))
    o = jnp.concatenate(pairs, axis=1)
    o_ref[...] = x + _dot(o, wo_ref[...])


def _cross_attention(x2, gain, wq, kv, wo, seq):
    t, d = x2.shape
    tm = min(512, seq)
    per_batch = seq // tm
    n_mem, wkv2 = kv.shape[1], kv.shape[2]
    wq_cols = wq.shape[1]
    return pl.pallas_call(
        _cross_kernel,
        out_shape=jax.ShapeDtypeStruct((t, d), F32),
        grid=(t // tm,),
        in_specs=[
            pl.BlockSpec((tm, d), lambda i: (i, 0)),
            pl.BlockSpec((1, d), lambda i: (0, 0)),
            pl.BlockSpec((d, wq_cols), lambda i: (0, 0)),
            pl.BlockSpec((None, n_mem, wkv2), lambda i: (i // per_batch, 0, 0)),
            pl.BlockSpec((wq_cols, d), lambda i: (0, 0)),
        ],
        out_specs=pl.BlockSpec((tm, d), lambda i: (i, 0)),
        compiler_params=_params(("arbitrary",), VMEM_LIMIT),
        name="cross_attn",
    )(x2, gain, wq, kv, wo)


def _ffn_kernel(x_ref, g_ref, wgu_ref, wd_ref, fg_ref, o_ref, acc_ref, *, final_norm, tf):
    x = x_ref[...]
    hn = _rms(x, g_ref[...]).astype(BF16)
    d_ff = wd_ref.shape[0]
    for c in range(d_ff // tf):
        gate = _dot(hn, wgu_ref[:, c * tf:(c + 1) * tf])
        up = _dot(hn, wgu_ref[:, d_ff + c * tf:d_ff + (c + 1) * tf])
        h = (gate / (1.0 + jnp.exp(-gate))) * up
        part = _dot(h.astype(BF16), wd_ref[c * tf:(c + 1) * tf, :])
        if c == 0:
            acc_ref[...] = part
        else:
            acc_ref[...] += part
    y = x + acc_ref[...]
    if final_norm:
        y = _rms(y, fg_ref[...])
    o_ref[...] = y


def _ffn(x2, gain, w_gu, w_down, final_gain, final_norm):
    t, d = x2.shape
    d_ff = w_down.shape[0]
    tm = min(512, t)
    tf = 2 * LANES
    assert d_ff % tf == 0
    kern = functools.partial(_ffn_kernel, final_norm=final_norm, tf=tf)
    resident = pl.Buffered(1)
    return pl.pallas_call(
        kern,
        out_shape=jax.ShapeDtypeStruct((t, d), F32),
        grid=(t // tm,),
        in_specs=[
            pl.BlockSpec((tm, d), lambda i: (i, 0)),
            pl.BlockSpec((1, d), lambda i: (0, 0)),
            pl.BlockSpec((d, 2 * d_ff), lambda i: (0, 0), pipeline_mode=resident),
            pl.BlockSpec((d_ff, d), lambda i: (0, 0), pipeline_mode=resident),
            pl.BlockSpec((1, d), lambda i: (0, 0)),
        ],
        out_specs=pl.BlockSpec((tm, d), lambda i: (i, 0)),
        scratch_shapes=[pltpu.VMEM((tm, d), F32)],
        compiler_params=_params(("arbitrary",), VMEM_LIMIT),
        name="ffn",
    )(x2, gain, w_gu, w_down, final_gain)


def _layout_in_weights(w_in):
    depth, d, _ = w_in.shape
    w_a, w_b, w_c = H_A * HEAD_DIM, H_B * 2 * HEAD_DIM, H_C * HEAD_DIM
    splits = (w_a, w_a, w_a, w_b, w_b, w_b, w_c, w_c, w_c, H_IDX * D_IDX, D_IDX, H_IDX, 3 * d)
    offs = [0] + [int(o) for o in np.cumsum(splits)]
    scale = HEAD_DIM ** -0.5
    col_scale = np.ones((offs[-1],), np.float32)
    for part in (0, 3, 6):
        col_scale[offs[part]:offs[part + 1]] = scale * math.log2(math.e)
    col_scale[offs[9]:offs[10]] = D_IDX ** -0.5
    w = (w_in * col_scale).astype(BF16)
    qa, ka, va, qb, kb, vb, qc, kc, vc, qi, ki, wi, gates = (
        w[:, :, offs[n]:offs[n + 1]] for n in range(len(splits)))
    pad = jnp.zeros((depth, d, LANES - H_IDX), BF16)
    w_main = jnp.concatenate([gates, qb, kb, qc, kc, qi, ki, ki, wi, pad, qa, ka], axis=2)
    assert w_main.shape[2] == PB_COUNT * PROJ_TN
    return w_main, va.swapaxes(1, 2), vb.swapaxes(1, 2), vc.swapaxes(1, 2)


def _rope_tables(seq):
    pos = jnp.arange(seq, dtype=F32)
    inv = ROPE_THETA ** (-jnp.arange(0, HEAD_DIM, 2, dtype=F32) / HEAD_DIM)
    ang = pos[:, None] * inv[None, :]
    cos, sin = jnp.cos(ang), jnp.sin(ang)
    zero = jnp.zeros_like(sin)
    reps = LANES // HEAD_DIM
    cos_t = jnp.tile(jnp.concatenate([cos, cos], axis=1), (1, reps))
    sina_t = jnp.tile(jnp.concatenate([-sin, zero], axis=1), (1, reps))
    sinb_t = jnp.tile(jnp.concatenate([zero, sin], axis=1), (1, reps))
    return cos_t, sina_t, sinb_t


def kernel(x, mem, norm_mix, w_in, rel_bias_a, lambda_vecs, subln_b, w_up_a, w_up_b, w_up_c,
           w_out, norm_cross, w_q_x, w_kv_x, w_o_x, norm_ffn, w_gu, w_down, mem_norm, final_norm):
    b, s, d = x.shape
    depth = w_in.shape[0]
    assert s % TQ_B == 0 and LEFT_CHUNKS * CHUNK == 2 * TQ and TQ == TK
    topk = min(TOPK_MAX, s // 4)
    cos_t, sina_t, sinb_t = _rope_tables(s)
    x2 = x.reshape(b * s, d)
    mem2 = mem.reshape(b * mem.shape[1], d)
    kv_all = _mem_kv(mem2, mem_norm.reshape(1, d), w_kv_x.astype(BF16))
    kv_all = kv_all.reshape(depth, b, mem.shape[1], -1)
    w_main, w_vat, w_vbt, w_vct = _layout_in_weights(w_in)
    ua, ub, uc, wo = (w.astype(BF16) for w in (w_up_a, w_up_b, w_up_c, w_out))
    wq_x = (w_q_x * HEAD_DIM ** -0.5).astype(BF16)
    wo_x = w_o_x.astype(BF16)
    wgu, wdn = w_gu.astype(BF16), w_down.astype(BF16)
    for l in range(depth):
        proj, vat, vbt, vct = _in_proj(x2, norm_mix[l].reshape(1, d), w_main[l], cos_t, sina_t,
                                       sinb_t, w_vat[l], w_vbt[l], w_vct[l], s)
        lam_init = 0.8 - 0.6 * math.exp(-0.3 * l)
        o_a = _band_attention(proj, vat, _band_bias(rel_bias_a[l], TQ), b, s)
        o_b = _diff_attention(proj, vbt, lambda_vecs[l].astype(F32),
                              subln_b[l].reshape(LANES, 1), lam_init, b, s)
        o_c = _sparse_attention(proj, vct, topk, b, s)
        x2 = _mix_out(x2, o_a, o_b, o_c, proj, ua[l], ub[l], uc[l], wo[l])
        x2 = _cross_attention(x2, norm_cross[l].reshape(1, d), wq_x[l], kv_all[l], wo_x[l], s)
        x2 = _ffn(x2, norm_ffn[l].reshape(1, d), wgu[l], wdn[l],
                  final_norm.reshape(1, d), final_norm=(l == depth - 1))
    return x2.reshape(b, s, d)
```

```python
import functools
import math

import jax
import jax.numpy as jnp
import numpy as np
from jax import lax
from jax.experimental import pallas as pl
from jax.experimental.pallas import tpu as pltpu

F32 = jnp.float32
BF16 = jnp.bfloat16
I32 = jnp.int32
I16 = jnp.int16

CHUNK = 64
HEAD_DIM = 64
H_A = 8
LEFT_CHUNKS = 8
REL_MAX = 256
H_B = 4
H_C = 8
H_IDX = 4
D_IDX = 64
TOPK_MAX = 256
H_X = 4
ROPE_THETA = 10000.0
EPS = 1e-6

LANES = 128
SUBLANES = 8
TQ = 256
TK = 256
TQ_B = 512
SLAB = 128
PROJ_TN = 512
VMEM_LIMIT = 56 * 1024 * 1024

NEG_INIT = -1e30
NEG_MASK = -2e30
INT_MIN = -(2 ** 31)

PB_GATES = 0
PB_QB, PB_KB, PB_QC, PB_KC, PB_TAIL = 6, 7, 8, 9, 10
PB_QA, PB_KA = 11, 12
PB_COUNT = 13
TAIL_QI, TAIL_KI, TAIL_WI = 0, 256, 384


def _dot(a, b):
    return jnp.dot(a, b, preferred_element_type=F32)


def _dot_nt(a, b):
    return lax.dot_general(a, b, (((1,), (1,)), ((), ())), preferred_element_type=F32)


def _rms(x, g):
    ms = jnp.mean(x * x, axis=-1, keepdims=True)
    return (x * lax.rsqrt(ms + EPS)) * g


def _params(sem, vmem=None):
    return pltpu.CompilerParams(dimension_semantics=sem, vmem_limit_bytes=vmem)


def _head_halves(pair):
    lo = lax.broadcasted_iota(I32, pair.shape, 1) < HEAD_DIM
    zero = jnp.zeros_like(pair)
    return jnp.where(lo, pair, zero), jnp.where(lo, zero, pair)


def _in_proj_kernel(x_ref, g_ref, w_ref, cos_ref, sina_ref, sinb_ref, wva_ref, wvb_ref, wvc_ref,
                    o_ref, vat_ref, vbt_ref, vct_ref):
    xn = _rms(x_ref[...], g_ref[...]).astype(BF16)
    for wv_ref, vt_ref in ((wva_ref, vat_ref), (wvb_ref, vbt_ref), (wvc_ref, vct_ref)):
        vt = _dot_nt(wv_ref[...], xn).astype(BF16)
        w = vt_ref.shape[2]
        for c in range(vt_ref.shape[0]):
            vt_ref[c] = vt[:, c * w:(c + 1) * w]

    tn = o_ref.shape[2]
    half = HEAD_DIM // 2
    for j in range(PB_COUNT):
        a = _dot(xn, w_ref[:, j * tn:(j + 1) * tn])
        if PB_QB <= j <= PB_TAIL:
            for c in range(tn // LANES):
                sl = slice(c * LANES, (c + 1) * LANES)
                ac = a[:, sl]
                if j == PB_TAIL and c * LANES >= TAIL_WI:
                    o_ref[j, :, sl] = ac.astype(o_ref.dtype)
                else:
                    rot = (pltpu.roll(ac, LANES - half, 1) * sina_ref[...]
                           + pltpu.roll(ac, half, 1) * sinb_ref[...])
                    o_ref[j, :, sl] = (ac * cos_ref[...] + rot).astype(o_ref.dtype)
        elif j >= PB_QA:
            o_ref[j] = a.astype(o_ref.dtype)
        else:
            o_ref[j] = (1.0 / (1.0 + jnp.exp(-a))).astype(o_ref.dtype)


def _in_proj(x2, gain, w_main, cos_t, sina_t, sinb_t, w_vat, w_vbt, w_vct, seq):
    t, d = x2.shape
    tm = min(512, seq)
    tn = PROJ_TN
    pos_blocks = seq // tm
    wa, wb, wc = w_vat.shape[0], w_vbt.shape[0], w_vct.shape[0]
    resident = pl.Buffered(1)
    return pl.pallas_call(
        _in_proj_kernel,
        out_shape=(
            jax.ShapeDtypeStruct((PB_COUNT, t, tn), BF16),
            jax.ShapeDtypeStruct((t // TQ, wa, TQ), BF16),
            jax.ShapeDtypeStruct((t // TQ_B, wb, TQ_B), BF16),
            jax.ShapeDtypeStruct((t // TK, wc, TK), BF16),
        ),
        grid=(t // tm,),
        in_specs=[
            pl.BlockSpec((tm, d), lambda i: (i, 0)),
            pl.BlockSpec((1, d), lambda i: (0, 0)),
            pl.BlockSpec((d, PB_COUNT * tn), lambda i: (0, 0), pipeline_mode=resident),
            pl.BlockSpec((tm, LANES), lambda i: (i % pos_blocks, 0)),
            pl.BlockSpec((tm, LANES), lambda i: (i % pos_blocks, 0)),
            pl.BlockSpec((tm, LANES), lambda i: (i % pos_blocks, 0)),
            pl.BlockSpec((wa, d), lambda i: (0, 0), pipeline_mode=resident),
            pl.BlockSpec((wb, d), lambda i: (0, 0), pipeline_mode=resident),
            pl.BlockSpec((wc, d), lambda i: (0, 0), pipeline_mode=resident),
        ],
        out_specs=(
            pl.BlockSpec((PB_COUNT, tm, tn), lambda i: (0, i, 0)),
            pl.BlockSpec((tm // TQ, wa, TQ), lambda i: (i, 0, 0)),
            pl.BlockSpec((tm // TQ_B, wb, TQ_B), lambda i: (i, 0, 0)),
            pl.BlockSpec((tm // TK, wc, TK), lambda i: (i, 0, 0)),
        ),
        compiler_params=_params(("arbitrary",), VMEM_LIMIT),
        name="in_proj",
    )(x2, gain, w_main, cos_t, sina_t, sinb_t, w_vat, w_vbt, w_vct)


def _band_kernel(q_ref, k0_ref, k1_ref, k2_ref, vt0_ref, vt1_ref, vt2_ref, bias_ref, o_ref,
                 acc_ref, s_ref, p_ref):
    tq = q_ref.shape[0]
    ones = jnp.ones((2 * SUBLANES, 3 * tq), BF16)

    def fold(x):
        return x.reshape(x.shape[0] // SUBLANES, SUBLANES, tq)

    ms = []
    for h in range(H_A):
        sl = slice((h // 2) * LANES, (h // 2 + 1) * LANES)
        kw = jnp.concatenate([k0_ref[:, sl], k1_ref[:, sl], k2_ref[:, sl]], axis=0)
        s = _dot_nt(kw, _head_halves(q_ref[:, sl])[h % 2]) + bias_ref[h]
        s_ref[h] = s
        ms.append(jnp.max(jnp.max(fold(s), axis=0), axis=0, keepdims=True))
    for h in range(H_A):
        for r in range(3 * tq // SLAB):
            p_ref[h, r * SLAB:(r + 1) * SLAB, :] = jnp.exp2(
                s_ref[h, r * SLAB:(r + 1) * SLAB, :] - ms[h]).astype(BF16)
    for h in range(H_A):
        rows = slice(h * HEAD_DIM, (h + 1) * HEAD_DIM)
        vt = jnp.concatenate([vt0_ref[0, rows, :], vt1_ref[0, rows, :], vt2_ref[0, rows, :]], axis=1)
        pv = _dot(jnp.concatenate([vt, ones], axis=0), p_ref[h])
        acc_ref[rows, :] = pv[:HEAD_DIM] / pv[HEAD_DIM:HEAD_DIM + 1]
    o_ref[...] = acc_ref[...].T.astype(o_ref.dtype)


def _band_bias(rel_bias, tq):
    back = LEFT_CHUNKS * CHUNK
    assert back == 2 * tq
    width = 4 * tq
    dist = back + tq - 1 - np.arange(width - 1)
    g = rel_bias[:, np.clip(dist, -(CHUNK - 1), REL_MAX) + (CHUNK - 1)].astype(F32)
    g = jnp.concatenate([g, jnp.zeros((g.shape[0], 1), F32)], axis=1) * math.log2(math.e)
    g = jnp.roll(g, -(tq - 1), axis=1)
    flat = jnp.tile(g, (1, tq))[:, :tq * (width - 1)]
    bias = flat.reshape(-1, tq, width - 1)[:, :, :3 * tq]
    qi = np.arange(tq)[:, None] + back
    kj = np.arange(3 * tq)[None, :]
    dc = qi // CHUNK - kj // CHUNK
    in_band = (dc >= 0) & (dc <= LEFT_CHUNKS)
    valid = np.stack([in_band & (kj >= (2 - v) * tq) for v in range(3)])
    bias = jnp.where(jnp.asarray(valid)[:, None], bias[None], NEG_MASK)
    return bias.transpose(0, 1, 3, 2)


def _band_attention(proj, vat, bias, b, s):
    tq = TQ
    nq = s // tq
    wa = H_A * HEAD_DIM

    def wspec(blk, off):
        return pl.BlockSpec((None, tq, wa),
                            lambda bb, i: (blk, bb * nq + jnp.maximum(i + off, 0), 0))

    def vspec(off):
        return pl.BlockSpec((1, wa, tq), lambda bb, i: (bb * nq + jnp.maximum(i + off, 0), 0, 0))

    return pl.pallas_call(
        _band_kernel,
        out_shape=jax.ShapeDtypeStruct((b * s, wa), BF16),
        grid=(b, nq),
        in_specs=[
            wspec(PB_QA, 0),
            wspec(PB_KA, -2), wspec(PB_KA, -1), wspec(PB_KA, 0),
            vspec(-2), vspec(-1), vspec(0),
            pl.BlockSpec((None, H_A, 3 * tq, tq), lambda bb, i: (jnp.minimum(i, 2), 0, 0, 0)),
        ],
        out_specs=pl.BlockSpec((tq, wa), lambda bb, i: (bb * nq + i, 0)),
        scratch_shapes=[
            pltpu.VMEM((wa, tq), F32),
            pltpu.VMEM((H_A, 3 * tq, tq), F32),
            pltpu.VMEM((H_A, 3 * tq, tq), BF16),
        ],
        compiler_params=_params(("arbitrary", "arbitrary"), VMEM_LIMIT),
        name="band_attn",
    )(proj, proj, proj, proj, vat, vat, vat, bias)


def _diff_kernel(q_ref, k_ref, vt_ref, lv_ref, sub_ref, o_ref, acc_ref, s_ref, p_ref, *,
                 lam_init):
    i = pl.program_id(1)
    tq = q_ref.shape[0]
    tk = vt_ref.shape[2]
    n_maps = 2 * H_B
    q = q_ref[...]
    qms = ()
    for h in range(H_B):
        qms += _head_halves(q[:, h * LANES:(h + 1) * LANES])
    acc_ref[...] = jnp.zeros(acc_ref.shape, F32)
    ck = lax.broadcasted_iota(I32, (tk, tq), 0) // CHUNK
    cq = lax.broadcasted_iota(I32, (tk, tq), 1) // CHUNK
    diag_ok = ck <= cq

    def fold(x):
        return x.reshape(x.shape[0] // SUBLANES, SUBLANES, tq)

    ones = jnp.ones((2 * SUBLANES, tk), BF16)

    def block(j, stats, mask):
        start = pl.multiple_of(j * tk, tk)
        mxs = []
        for c in range(n_maps):
            h = c // 2
            s = _dot_nt(k_ref[pl.ds(start, tk), h * LANES:(h + 1) * LANES], qms[c])
            if mask is not None:
                s = jnp.where(mask, s, NEG_MASK)
            s_ref[c] = s
            mxs.append(jnp.max(fold(s), axis=0))
        ms, ls = stats
        new_m, new_l, alphas = [], [], []
        for c in range(n_maps):
            m_new = jnp.maximum(ms[c], jnp.max(mxs[c], axis=0, keepdims=True))
            for r in range(tk // SLAB):
                p = jnp.exp2(s_ref[c, r * SLAB:(r + 1) * SLAB, :] - m_new)
                p_ref[c, r * SLAB:(r + 1) * SLAB, :] = p.astype(BF16)
            alpha = jnp.exp2(ms[c] - m_new)
            new_m.append(m_new)
            vt = jnp.concatenate([vt_ref[j, (c // 2) * LANES:(c // 2 + 1) * LANES, :], ones], axis=0)
            pv = _dot(vt, p_ref[c])
            acc_ref[c] = alpha * acc_ref[c] + pv[:LANES]
            new_l.append(alpha * ls[c] + pv[LANES:LANES + 1])
        return tuple(new_m), tuple(new_l)

    m0 = jnp.full((1, tq), NEG_INIT, F32)
    l0 = jnp.zeros((1, tq), F32)
    stats = lax.fori_loop(0, i, lambda j, st: block(j, st, None),
                          ((m0,) * n_maps, (l0,) * n_maps))
    _, ls = block(i, stats, diag_ok)

    lv = lv_ref[...]
    lam = (jnp.exp(jnp.sum(lv[0:1] * lv[1:2], axis=-1, keepdims=True))
           - jnp.exp(jnp.sum(lv[2:3] * lv[3:4], axis=-1, keepdims=True)) + lam_init)
    for h in range(H_B):
        o = acc_ref[2 * h] / ls[2 * h] - lam * (acc_ref[2 * h + 1] / ls[2 * h + 1])
        ms = jnp.mean(o * o, axis=0, keepdims=True)
        o = (o * lax.rsqrt(ms + EPS)) * sub_ref[...] * (1.0 - lam_init)
        o_ref[:, h * LANES:(h + 1) * LANES] = o.T.astype(o_ref.dtype)


def _diff_attention(proj, vbt, lambda_vec, subln_col, lam_init, b, s):
    tq = TQ_B
    nkb = s // tq
    wb = H_B * 2 * HEAD_DIM
    kern = functools.partial(_diff_kernel, lam_init=lam_init)
    return pl.pallas_call(
        kern,
        out_shape=jax.ShapeDtypeStruct((b * s, wb), BF16),
        grid=(b, nkb),
        in_specs=[
            pl.BlockSpec((None, tq, wb), lambda bb, i: (PB_QB, bb * nkb + i, 0)),
            pl.BlockSpec((None, s, wb), lambda bb, i: (PB_KB, bb, 0)),
            pl.BlockSpec((nkb, wb, tq), lambda bb, i: (bb, 0, 0)),
            pl.BlockSpec((4, HEAD_DIM), lambda bb, i: (0, 0)),
            pl.BlockSpec((LANES, 1), lambda bb, i: (0, 0)),
        ],
        out_specs=pl.BlockSpec((tq, wb), lambda bb, i: (bb * nkb + i, 0)),
        scratch_shapes=[
            pltpu.VMEM((2 * H_B, LANES, tq), F32),
            pltpu.VMEM((2 * H_B, tq, tq), F32),
            pltpu.VMEM((2 * H_B, tq, tq), BF16),
        ],
        compiler_params=_params(("arbitrary", "arbitrary"), VMEM_LIMIT),
        name="diff_attn",
    )(proj, proj, vbt, lambda_vec, subln_col)


def _sparse_kernel(q_ref, k_ref, vt_ref, tail_ref, ki_ref, o_ref,
                   keys_ref, hi_ref, lo_ref, acc_ref, s_ref, p_ref, b_ref, *, topk):
    i = pl.program_id(1)
    tq = q_ref.shape[0]
    tk = vt_ref.shape[2]

    wi = tail_ref[:, TAIL_WI:TAIL_WI + LANES]
    wt = wi.astype(F32).T * (H_IDX ** -0.5)
    w_rows = [wt[h:h + 1, :] for h in range(H_IDX)]
    qi_heads = (_head_halves(tail_ref[:, TAIL_QI:TAIL_QI + LANES])
                + _head_halves(tail_ref[:, TAIL_QI + LANES:TAIL_QI + 2 * LANES]))
    n_pairs = (i + 2) // 2

    def score_pair(jj, masked):
        start = pl.multiple_of(jj * 2 * tk, 2 * tk)
        kk = ki_ref[pl.ds(start, 2 * tk), :]
        for h in range(H_IDX):
            s_ref[2 * h:2 * h + 2] = _dot_nt(kk, qi_heads[h]).reshape(2, tk, tq)
        sc = jnp.zeros((2 * tk, tq), F32)
        for h in range(H_IDX):
            logits = s_ref[2 * h:2 * h + 2].reshape(2 * tk, tq)
            sc = sc + jnp.maximum(logits, 0.0) * w_rows[h]
        bits = pltpu.bitcast(sc, I32)
        key = bits ^ ((bits >> 31) & 0x7FFFFFFF)
        key = jnp.where(sc == 0.0, 0, key)
        if masked:
            ck = (lax.broadcasted_iota(I32, (2 * tk, tq), 0) + start) // CHUNK
            cq = (lax.broadcasted_iota(I32, (2 * tk, tq), 1) + i * tq) // CHUNK
            key = jnp.where(ck <= cq, key, INT_MIN)
        for half in range(2):
            part = key[half * tk:(half + 1) * tk]
            keys_ref[2 * jj + half] = part
            hi_ref[2 * jj + half] = (part >> 16).astype(I16)

    def score_body(jj, carry):
        score_pair(jj, False)
        return carry

    lax.fori_loop(0, n_pairs - 1, score_body, 0)
    score_pair(n_pairs - 1, True)

    n_acc = 4
    grp = 2 * SUBLANES

    def count16(ref, pred_fn):
        def inner(j, accs):
            accs = list(accs)
            for g in range(tk // grp):
                rows = ref[j, g * grp:(g + 1) * grp, :]
                a = accs[g % n_acc]
                accs[g % n_acc] = jnp.where(pred_fn(rows), a + 1, a)
            return tuple(accs)
        accs = lax.fori_loop(0, i + 1, inner, (jnp.zeros((grp, tq), I16),) * n_acc)
        acc = (accs[0] + accs[1]) + (accs[2] + accs[3])
        return jnp.sum(acc.astype(F32), axis=0, keepdims=True)

    def digit16(x):
        return jnp.broadcast_to(x, (grp, tq)).astype(I16)

    def radix16(ref, want):
        def body(it, prefix):
            cand_u = prefix | lax.shift_left(jnp.int32(1), 15 - it)
            cand = digit16(cand_u - 32768)
            cnt = count16(ref, lambda rows: rows >= cand)
            return jnp.where(cnt >= want, cand_u, prefix)
        return lax.fori_loop(0, 16, body, jnp.zeros((1, tq), I32)) - 32768

    t_hi = radix16(hi_ref, topk)
    t_hi16 = digit16(t_hi)
    n_above = count16(hi_ref, lambda rows: rows > t_hi16)

    def low_body(j, carry):
        key = keys_ref[j]
        lo = ((key ^ 0x8000) << 16) >> 16
        lo_ref[j] = jnp.where((key >> 16) == t_hi, lo, -32768).astype(I16)
        return carry

    lax.fori_loop(0, i + 1, low_body, 0)
    t_lo = radix16(lo_ref, topk - n_above)
    thr = (t_hi << 16) | (t_lo + 32768)

    def count(pred_fn):
        def inner(j, accs):
            accs = list(accs)
            for g in range(tk // SUBLANES):
                rows = keys_ref[j, g * SUBLANES:(g + 1) * SUBLANES, :]
                a = accs[g % n_acc]
                accs[g % n_acc] = jnp.where(pred_fn(rows), a + 1.0, a)
            return tuple(accs)
        accs = lax.fori_loop(0, i + 1, inner, (jnp.zeros((SUBLANES, tq), F32),) * n_acc)
        acc = (accs[0] + accs[1]) + (accs[2] + accs[3])
        return jnp.sum(acc, axis=0, keepdims=True)

    n_gt = count(lambda blk: blk > thr)
    need = jnp.where(thr == INT_MIN, 0.0, topk - n_gt)

    acc_ref[...] = jnp.zeros(acc_ref.shape, F32)
    q = q_ref[...]
    q_heads = ()
    for hp in range(H_C // 2):
        q_heads += _head_halves(q[:, hp * LANES:(hp + 1) * LANES])
    earlier = (lax.broadcasted_iota(I32, (tk, tk), 0)
               > lax.broadcasted_iota(I32, (tk, tk), 1))
    earlier = jnp.where(earlier, 1.0, 0.0).astype(BF16)

    def fold(x):
        return x.reshape(x.shape[0] // SUBLANES, SUBLANES, tq)

    ones = jnp.ones((2 * SUBLANES, tk), BF16)

    def attn_body(j, carry):
        tie_seen, ms, ls = carry
        start = pl.multiple_of(j * tk, tk)
        kblk = keys_ref[j]
        tie = kblk == thr
        tie_f = jnp.where(tie, 1.0, 0.0)
        rank = _dot(earlier, tie_f.astype(BF16)) + tie_seen
        sel = (kblk > thr) | (tie & (rank < need))
        b_ref[...] = jnp.where(sel, 0.0, NEG_MASK)
        mxs = []
        for h in range(H_C):
            hp = h // 2
            kb = k_ref[pl.ds(start, tk), hp * LANES:(hp + 1) * LANES]
            s = _dot_nt(kb, q_heads[h]) + b_ref[...]
            s_ref[h] = s
            mxs.append(jnp.max(fold(s), axis=0))
        new_m, new_l, alphas = [], [], []
        for h in range(H_C):
            m_new = jnp.maximum(ms[h], jnp.max(mxs[h], axis=0, keepdims=True))
            for r in range(tk // SLAB):
                p = jnp.exp2(s_ref[h, r * SLAB:(r + 1) * SLAB, :] - m_new)
                p_ref[h, r * SLAB:(r + 1) * SLAB, :] = p.astype(BF16)
            new_m.append(m_new)
            alphas.append(jnp.exp2(ms[h] - m_new))
        for h in range(H_C):
            rows = slice(h * HEAD_DIM, (h + 1) * HEAD_DIM)
            pv = _dot(jnp.concatenate([vt_ref[j, rows, :], ones], axis=0), p_ref[h])
            acc_ref[rows, :] = alphas[h] * acc_ref[rows, :] + pv[:HEAD_DIM]
            new_l.append(alphas[h] * ls[h] + pv[HEAD_DIM:HEAD_DIM + 1])
        return (tie_seen + jnp.sum(tie_f, axis=0, keepdims=True), tuple(new_m), tuple(new_l))

    m0 = jnp.full((1, tq), NEG_INIT, F32)
    l0 = jnp.zeros((1, tq), F32)
    _, _, ls = lax.fori_loop(0, i + 1, attn_body, (l0, (m0,) * H_C, (l0,) * H_C))

    for h in range(H_C):
        rows = slice(h * HEAD_DIM, (h + 1) * HEAD_DIM)
        acc_ref[rows, :] = acc_ref[rows, :] / ls[h]
    o_ref[...] = acc_ref[...].T.astype(o_ref.dtype)


def _sparse_attention(proj, vct, topk, b, s):
    tq = TQ
    wc = H_C * HEAD_DIM
    nkb = s // TK
    nq = s // tq
    kern = functools.partial(_sparse_kernel, topk=float(topk))
    return pl.pallas_call(
        kern,
        out_shape=jax.ShapeDtypeStruct((b * s, wc), BF16),
        grid=(b, nq),
        in_specs=[
            pl.BlockSpec((None, tq, wc), lambda bb, i: (PB_QC, bb * nq + i, 0)),
            pl.BlockSpec((None, s, wc), lambda bb, i: (PB_KC, bb, 0)),
            pl.BlockSpec((nkb, wc, TK), lambda bb, i: (bb, 0, 0)),
            pl.BlockSpec((None, tq, PROJ_TN), lambda bb, i: (PB_TAIL, bb * nq + i, 0)),
            pl.BlockSpec((None, s, LANES), lambda bb, i: (PB_TAIL, bb, TAIL_KI // LANES)),
        ],
        out_specs=pl.BlockSpec((tq, wc), lambda bb, i: (bb * nq + i, 0)),
        scratch_shapes=[
            pltpu.VMEM((nkb, TK, tq), I32),
            pltpu.VMEM((nkb, TK, tq), I16),
            pltpu.VMEM((nkb, TK, tq), I16),
            pltpu.VMEM((wc, tq), F32),
            pltpu.VMEM((H_C, TK, tq), F32),
            pltpu.VMEM((H_C, TK, tq), BF16),
            pltpu.VMEM((TK, tq), F32),
        ],
        compiler_params=_params(("arbitrary", "arbitrary"), VMEM_LIMIT),
        name="sparse_attn",
    )(proj, proj, vct, proj, proj)


def _mix_kernel(x_ref, oa_ref, ob_ref, oc_ref, ga_ref, gb_ref, gc_ref,
                ua_ref, ub_ref, uc_ref, wo_ref, o_ref):
    def gate(g_ref):
        return jnp.concatenate([g_ref[0], g_ref[1]], axis=1).astype(F32)

    y = gate(ga_ref) * _dot(oa_ref[...], ua_ref[...])
    y = y + gate(gb_ref) * _dot(ob_ref[...], ub_ref[...])
    y = y + gate(gc_ref) * _dot(oc_ref[...], uc_ref[...])
    o_ref[...] = x_ref[...] + _dot(y.astype(BF16), wo_ref[...])


def _mix_out(x2, oa, ob, oc, proj, ua, ub, uc, wo):
    t, d = x2.shape
    tm = 512
    per_gate = d // PROJ_TN
    assert PB_GATES % per_gate == 0
    gate_blk = PB_GATES // per_gate
    w = oa.shape[1]

    def full(shape):
        return pl.BlockSpec(shape, lambda i: (0, 0))

    return pl.pallas_call(
        _mix_kernel,
        out_shape=jax.ShapeDtypeStruct((t, d), F32),
        grid=(t // tm,),
        in_specs=[
            pl.BlockSpec((tm, d), lambda i: (i, 0)),
            pl.BlockSpec((tm, w), lambda i: (i, 0)),
            pl.BlockSpec((tm, w), lambda i: (i, 0)),
            pl.BlockSpec((tm, w), lambda i: (i, 0)),
            pl.BlockSpec((per_gate, tm, PROJ_TN), lambda i: (gate_blk, i, 0)),
            pl.BlockSpec((per_gate, tm, PROJ_TN), lambda i: (gate_blk + 1, i, 0)),
            pl.BlockSpec((per_gate, tm, PROJ_TN), lambda i: (gate_blk + 2, i, 0)),
            full((w, d)), full((w, d)), full((w, d)), full((d, d)),
        ],
        out_specs=pl.BlockSpec((tm, d), lambda i: (i, 0)),
        compiler_params=_params(("arbitrary",), VMEM_LIMIT),
        name="mix_out",
    )(x2, oa, ob, oc, proj, proj, proj, ua, ub, uc, wo)


def _mem_kv_kernel(mem_ref, g_ref, w_ref, o_ref):
    o_ref[...] = _dot(_rms(mem_ref[...], g_ref[...]).astype(BF16), w_ref[...]).astype(o_ref.dtype)


def _mem_kv(mem2, mem_norm, w_kv):
    r, d = mem2.shape
    depth, _, n = w_kv.shape
    tm = min(1024, r)
    return pl.pallas_call(
        _mem_kv_kernel,
        out_shape=jax.ShapeDtypeStruct((depth, r, n), BF16),
        grid=(depth, r // tm),
        in_specs=[
            pl.BlockSpec((tm, d), lambda l, i: (i, 0)),
            pl.BlockSpec((1, d), lambda l, i: (0, 0)),
            pl.BlockSpec((None, d, n), lambda l, i: (l, 0, 0)),
        ],
        out_specs=pl.BlockSpec((None, tm, n), lambda l, i: (l, i, 0)),
        compiler_params=_params(("arbitrary", "arbitrary"), VMEM_LIMIT),
        name="mem_kv",
    )(mem2, mem_norm, w_kv)


def _cross_kernel(x_ref, g_ref, wq_ref, kv_ref, wo_ref, o_ref):
    x = x_ref[...]
    tm = x.shape[0]
    hn = _rms(x, g_ref[...]).astype(BF16)
    q = _dot(hn, wq_ref[...]).astype(BF16)
    kv = kv_ref[...]
    wkv = H_X * HEAD_DIM
    lo = lax.broadcasted_iota(I32, (tm, LANES), 1) < HEAD_DIM
    pairs = []
    for hp in range(H_X // 2):
        kp = kv[:, hp * LANES:(hp + 1) * LANES]
        vp = kv[:, wkv + hp * LANES:wkv + (hp + 1) * LANES]
        outs = []
        for qm in _head_halves(q[:, hp * LANES:(hp + 1) * LANES]):
            s = _dot_nt(qm, kp)
            m = jnp.max(s, axis=-1, keepdims=True)
            p = jnp.exp(s - m)
            l = jnp.sum(p, axis=-1, keepdims=True)
            outs.append(_dot(p.astype(BF16), vp) / l)
        pairs.append(jnp.where(lo, outs[0], outs[1]).astype(BF16))
    o = jnp.concatenate(pairs, axis=1)
    o_ref[...] = x + _dot(o, wo_ref[...])


def _cross_attention(x2, gain, wq, kv, wo, seq):
    t, d = x2.shape
    tm = min(512, seq)
    per_batch = seq // tm
    n_mem, wkv2 = kv.shape[1], kv.shape[2]
    wq_cols = wq.shape[1]
    return pl.pallas_call(
        _cross_kernel,
        out_shape=jax.ShapeDtypeStruct((t, d), F32),
        grid=(t // tm,),
        in_specs=[
            pl.BlockSpec((tm, d), lambda i: (i, 0)),
            pl.BlockSpec((1, d), lambda i: (0, 0)),
            pl.BlockSpec((d, wq_cols), lambda i: (0, 0)),
            pl.BlockSpec((None, n_mem, wkv2), lambda i: (i // per_batch, 0, 0)),
            pl.BlockSpec((wq_cols, d), lambda i: (0, 0)),
        ],
        out_specs=pl.BlockSpec((tm, d), lambda i: (i, 0)),
        compiler_params=_params(("arbitrary",), VMEM_LIMIT),
        name="cross_attn",
    )(x2, gain, wq, kv, wo)


def _ffn_kernel(x_ref, g_ref, wgu_ref, wd_ref, fg_ref, o_ref, acc_ref, *, final_norm, tf):
    x = x_ref[...]
    hn = _rms(x, g_ref[...]).astype(BF16)
    d_ff = wd_ref.shape[0]
    for c in range(d_ff // tf):
        gate = _dot(hn, wgu_ref[:, c * tf:(c + 1) * tf])
        up = _dot(hn, wgu_ref[:, d_ff + c * tf:d_ff + (c + 1) * tf])
        h = (gate / (1.0 + jnp.exp(-gate))) * up
        part = _dot(h.astype(BF16), wd_ref[c * tf:(c + 1) * tf, :])
        if c == 0:
            acc_ref[...] = part
        else:
            acc_ref[...] += part
    y = x + acc_ref[...]
    if final_norm:
        y = _rms(y, fg_ref[...])
    o_ref[...] = y


def _ffn(x2, gain, w_gu, w_down, final_gain, final_norm):
    t, d = x2.shape
    d_ff = w_down.shape[0]
    tm = min(512, t)
    tf = 2 * LANES
    assert d_ff % tf == 0
    kern = functools.partial(_ffn_kernel, final_norm=final_norm, tf=tf)
    resident = pl.Buffered(1)
    return pl.pallas_call(
        kern,
        out_shape=jax.ShapeDtypeStruct((t, d), F32),
        grid=(t // tm,),
        in_specs=[
            pl.BlockSpec((tm, d), lambda i: (i, 0)),
            pl.BlockSpec((1, d), lambda i: (0, 0)),
            pl.BlockSpec((d, 2 * d_ff), lambda i: (0, 0), pipeline_mode=resident),
            pl.BlockSpec((d_ff, d), lambda i: (0, 0), pipeline_mode=resident),
            pl.BlockSpec((1, d), lambda i: (0, 0)),
        ],
        out_specs=pl.BlockSpec((tm, d), lambda i: (i, 0)),
        scratch_shapes=[pltpu.VMEM((tm, d), F32)],
        compiler_params=_params(("arbitrary",), VMEM_LIMIT),
        name="ffn",
    )(x2, gain, w_gu, w_down, final_gain)


def _layout_in_weights(w_in):
    depth, d, _ = w_in.shape
    w_a, w_b, w_c = H_A * HEAD_DIM, H_B * 2 * HEAD_DIM, H_C * HEAD_DIM
    splits = (w_a, w_a, w_a, w_b, w_b, w_b, w_c, w_c, w_c, H_IDX * D_IDX, D_IDX, H_IDX, 3 * d)
    offs = [0] + [int(o) for o in np.cumsum(splits)]
    scale = HEAD_DIM ** -0.5
    col_scale = np.ones((offs[-1],), np.float32)
    for part in (0, 3, 6):
        col_scale[offs[part]:offs[part + 1]] = scale * math.log2(math.e)
    col_scale[offs[9]:offs[10]] = D_IDX ** -0.5
    w = (w_in * col_scale).astype(BF16)
    qa, ka, va, qb, kb, vb, qc, kc, vc, qi, ki, wi, gates = (
        w[:, :, offs[n]:offs[n + 1]] for n in range(len(splits)))
    pad = jnp.zeros((depth, d, LANES - H_IDX), BF16)
    w_main = jnp.concatenate([gates, qb, kb, qc, kc, qi, ki, ki, wi, pad, qa, ka], axis=2)
    assert w_main.shape[2] == PB_COUNT * PROJ_TN
    return w_main, va.swapaxes(1, 2), vb.swapaxes(1, 2), vc.swapaxes(1, 2)


def _rope_tables(seq):
    pos = jnp.arange(seq, dtype=F32)
    inv = ROPE_THETA ** (-jnp.arange(0, HEAD_DIM, 2, dtype=F32) / HEAD_DIM)
    ang = pos[:, None] * inv[None, :]
    cos, sin = jnp.cos(ang), jnp.sin(ang)
    zero = jnp.zeros_like(sin)
    reps = LANES // HEAD_DIM
    cos_t = jnp.tile(jnp.concatenate([cos, cos], axis=1), (1, reps))
    sina_t = jnp.tile(jnp.concatenate([-sin, zero], axis=1), (1, reps))
    sinb_t = jnp.tile(jnp.concatenate([zero, sin], axis=1), (1, reps))
    return cos_t, sina_t, sinb_t


def kernel(x, mem, norm_mix, w_in, rel_bias_a, lambda_vecs, subln_b, w_up_a, w_up_b, w_up_c,
           w_out, norm_cross, w_q_x, w_kv_x, w_o_x, norm_ffn, w_gu, w_down, mem_norm, final_norm):
    b, s, d = x.shape
    depth = w_in.shape[0]
    assert s % TQ_B == 0 and LEFT_CHUNKS * CHUNK == 2 * TQ and TQ == TK
    topk = min(TOPK_MAX, s // 4)
    cos_t, sina_t, sinb_t = _rope_tables(s)
    x2 = x.reshape(b * s, d)
    mem2 = mem.reshape(b * mem.shape[1], d)
    kv_all = _mem_kv(mem2, mem_norm.reshape(1, d), w_kv_x.astype(BF16))
    kv_all = kv_all.reshape(depth, b, mem.shape[1], -1)
    w_main, w_vat, w_vbt, w_vct = _layout_in_weights(w_in)
    ua, ub, uc, wo = (w.astype(BF16) for w in (w_up_a, w_up_b, w_up_c, w_out))
    wq_x = (w_q_x * HEAD_DIM ** -0.5).astype(BF16)
    wo_x = w_o_x.astype(BF16)
    wgu, wdn = w_gu.astype(BF16), w_down.astype(BF16)
    for l in range(depth):
        proj, vat, vbt, vct = _in_proj(x2, norm_mix[l].reshape(1, d), w_main[l], cos_t, sina_t,
                                       sinb_t, w_vat[l], w_vbt[l], w_vct[l], s)
        lam_init = 0.8 - 0.6 * math.exp(-0.3 * l)
        o_a = _band_attention(proj, vat, _band_bias(rel_bias_a[l], TQ), b, s)
        o_b = _diff_attention(proj, vbt, lambda_vecs[l].astype(F32),
                              subln_b[l].reshape(LANES, 1), lam_init, b, s)
        o_c = _sparse_attention(proj, vct, topk, b, s)
        x2 = _mix_out(x2, o_a, o_b, o_c, proj, ua[l], ub[l], uc[l], wo[l])
        x2 = _cross_attention(x2, norm_cross[l].reshape(1, d), wq_x[l], kv_all[l], wo_x[l], s)
        x2 = _ffn(x2, norm_ffn[l].reshape(1, d), wgu[l], wdn[l],
                  final_norm.reshape(1, d), final_norm=(l == depth - 1))
    return x2.reshape(b, s, d)
```

```python
import functools
import math

import jax
import jax.numpy as jnp
import numpy as np
from jax import lax
from jax.experimental import pallas as pl
from jax.experimental.pallas import tpu as pltpu

F32 = jnp.float32
BF16 = jnp.bfloat16
I32 = jnp.int32
I16 = jnp.int16

CHUNK = 64
HEAD_DIM = 64
H_A = 8
LEFT_CHUNKS = 8
REL_MAX = 256
H_B = 4
H_C = 8
H_IDX = 4
D_IDX = 64
TOPK_MAX = 256
H_X = 4
ROPE_THETA = 10000.0
EPS = 1e-6

LANES = 128
SUBLANES = 8
TQ = 256
TK = 256
TQ_B = 512
SLAB = 128
PROJ_TN = 512
VMEM_LIMIT = 56 * 1024 * 1024

NEG_INIT = -1e30
NEG_MASK = -2e30
INT_MIN = -(2 ** 31)

PB_GATES = 0
PB_QB, PB_KB, PB_QC, PB_KC, PB_TAIL = 6, 7, 8, 9, 10
PB_QA, PB_KA = 11, 12
PB_COUNT = 13
TAIL_QI, TAIL_KI, TAIL_WI = 0, 256, 384


def _dot(a, b):
    return jnp.dot(a, b, preferred_element_type=F32)


def _dot_nt(a, b):
    return lax.dot_general(a, b, (((1,), (1,)), ((), ())), preferred_element_type=F32)


def _rms(x, g):
    ms = jnp.mean(x * x, axis=-1, keepdims=True)
    return (x * lax.rsqrt(ms + EPS)) * g


def _params(sem, vmem=None):
    return pltpu.CompilerParams(dimension_semantics=sem, vmem_limit_bytes=vmem)


def _head_halves(pair):
    lo = lax.broadcasted_iota(I32, pair.shape, 1) < HEAD_DIM
    zero = jnp.zeros_like(pair)
    return jnp.where(lo, pair, zero), jnp.where(lo, zero, pair)


def _in_proj_kernel(x_ref, g_ref, w_ref, cos_ref, sina_ref, sinb_ref, wva_ref, wvb_ref, wvc_ref,
                    o_ref, vat_ref, vbt_ref, vct_ref):
    xn = _rms(x_ref[...], g_ref[...]).astype(BF16)
    for wv_ref, vt_ref in ((wva_ref, vat_ref), (wvb_ref, vbt_ref), (wvc_ref, vct_ref)):
        vt = _dot_nt(wv_ref[...], xn).astype(BF16)
        w = vt_ref.shape[2]
        for c in range(vt_ref.shape[0]):
            vt_ref[c] = vt[:, c * w:(c + 1) * w]

    tn = o_ref.shape[2]
    half = HEAD_DIM // 2
    for j in range(PB_COUNT):
        a = _dot(xn, w_ref[:, j * tn:(j + 1) * tn])
        if PB_QB <= j <= PB_TAIL:
            for c in range(tn // LANES):
                sl = slice(c * LANES, (c + 1) * LANES)
                ac = a[:, sl]
                if j == PB_TAIL and c * LANES >= TAIL_WI:
                    o_ref[j, :, sl] = ac.astype(o_ref.dtype)
                else:
                    rot = (pltpu.roll(ac, LANES - half, 1) * sina_ref[...]
                           + pltpu.roll(ac, half, 1) * sinb_ref[...])
                    o_ref[j, :, sl] = (ac * cos_ref[...] + rot).astype(o_ref.dtype)
        elif j >= PB_QA:
            o_ref[j] = a.astype(o_ref.dtype)
        else:
            o_ref[j] = (1.0 / (1.0 + jnp.exp(-a))).astype(o_ref.dtype)


def _in_proj(x2, gain, w_main, cos_t, sina_t, sinb_t, w_vat, w_vbt, w_vct, seq):
    t, d = x2.shape
    tm = min(512, seq)
    tn = PROJ_TN
    pos_blocks = seq // tm
    wa, wb, wc = w_vat.shape[0], w_vbt.shape[0], w_vct.shape[0]
    resident = pl.Buffered(1)
    return pl.pallas_call(
        _in_proj_kernel,
        out_shape=(
            jax.ShapeDtypeStruct((PB_COUNT, t, tn), BF16),
            jax.ShapeDtypeStruct((t // TQ, wa, TQ), BF16),
            jax.ShapeDtypeStruct((t // TQ_B, wb, TQ_B), BF16),
            jax.ShapeDtypeStruct((t // TK, wc, TK), BF16),
        ),
        grid=(t // tm,),
        in_specs=[
            pl.BlockSpec((tm, d), lambda i: (i, 0)),
            pl.BlockSpec((1, d), lambda i: (0, 0)),
            pl.BlockSpec((d, PB_COUNT * tn), lambda i: (0, 0), pipeline_mode=resident),
            pl.BlockSpec((tm, LANES), lambda i: (i % pos_blocks, 0)),
            pl.BlockSpec((tm, LANES), lambda i: (i % pos_blocks, 0)),
            pl.BlockSpec((tm, LANES), lambda i: (i % pos_blocks, 0)),
            pl.BlockSpec((wa, d), lambda i: (0, 0), pipeline_mode=resident),
            pl.BlockSpec((wb, d), lambda i: (0, 0), pipeline_mode=resident),
            pl.BlockSpec((wc, d), lambda i: (0, 0), pipeline_mode=resident),
        ],
        out_specs=(
            pl.BlockSpec((PB_COUNT, tm, tn), lambda i: (0, i, 0)),
            pl.BlockSpec((tm // TQ, wa, TQ), lambda i: (i, 0, 0)),
            pl.BlockSpec((tm // TQ_B, wb, TQ_B), lambda i: (i, 0, 0)),
            pl.BlockSpec((tm // TK, wc, TK), lambda i: (i, 0, 0)),
        ),
        compiler_params=_params(("arbitrary",), VMEM_LIMIT),
        name="in_proj",
    )(x2, gain, w_main, cos_t, sina_t, sinb_t, w_vat, w_vbt, w_vct)


def _band_kernel(q_ref, k0_ref, k1_ref, k2_ref, vt0_ref, vt1_ref, vt2_ref, bias_ref, o_ref,
                 acc_ref, s_ref, p_ref):
    tq = q_ref.shape[0]
    ones = jnp.ones((2 * SUBLANES, 3 * tq), BF16)

    def fold(x):
        return x.reshape(x.shape[0] // SUBLANES, SUBLANES, tq)

    ms = []
    for h in range(H_A):
        sl = slice((h // 2) * LANES, (h // 2 + 1) * LANES)
        kw = jnp.concatenate([k0_ref[:, sl], k1_ref[:, sl], k2_ref[:, sl]], axis=0)
        s = _dot_nt(kw, _head_halves(q_ref[:, sl])[h % 2]) + bias_ref[h]
        s_ref[h] = s
        ms.append(jnp.max(jnp.max(fold(s), axis=0), axis=0, keepdims=True))
    for h in range(H_A):
        for r in range(3 * tq // SLAB):
            p_ref[h, r * SLAB:(r + 1) * SLAB, :] = jnp.exp2(
                s_ref[h, r * SLAB:(r + 1) * SLAB, :] - ms[h]).astype(BF16)
    for h in range(H_A):
        rows = slice(h * HEAD_DIM, (h + 1) * HEAD_DIM)
        vt = jnp.concatenate([vt0_ref[0, rows, :], vt1_ref[0, rows, :], vt2_ref[0, rows, :]], axis=1)
        pv = _dot(jnp.concatenate([vt, ones], axis=0), p_ref[h])
        acc_ref[rows, :] = pv[:HEAD_DIM] / pv[HEAD_DIM:HEAD_DIM + 1]
    o_ref[...] = acc_ref[...].T.astype(o_ref.dtype)


def _band_bias(rel_bias, tq):
    back = LEFT_CHUNKS * CHUNK
    assert back == 2 * tq
    width = 4 * tq
    dist = back + tq - 1 - np.arange(width - 1)
    g = rel_bias[:, np.clip(dist, -(CHUNK - 1), REL_MAX) + (CHUNK - 1)].astype(F32)
    g = jnp.concatenate([g, jnp.zeros((g.shape[0], 1), F32)], axis=1) * math.log2(math.e)
    g = jnp.roll(g, -(tq - 1), axis=1)
    flat = jnp.tile(g, (1, tq))[:, :tq * (width - 1)]
    bias = flat.reshape(-1, tq, width - 1)[:, :, :3 * tq]
    qi = np.arange(tq)[:, None] + back
    kj = np.arange(3 * tq)[None, :]
    dc = qi // CHUNK - kj // CHUNK
    in_band = (dc >= 0) & (dc <= LEFT_CHUNKS)
    valid = np.stack([in_band & (kj >= (2 - v) * tq) for v in range(3)])
    bias = jnp.where(jnp.asarray(valid)[:, None], bias[None], NEG_MASK)
    return bias.transpose(0, 1, 3, 2)


def _band_attention(proj, vat, bias, b, s):
    tq = TQ
    nq = s // tq
    wa = H_A * HEAD_DIM

    def wspec(blk, off):
        return pl.BlockSpec((None, tq, wa),
                            lambda bb, i: (blk, bb * nq + jnp.maximum(i + off, 0), 0))

    def vspec(off):
        return pl.BlockSpec((1, wa, tq), lambda bb, i: (bb * nq + jnp.maximum(i + off, 0), 0, 0))

    return pl.pallas_call(
        _band_kernel,
        out_shape=jax.ShapeDtypeStruct((b * s, wa), BF16),
        grid=(b, nq),
        in_specs=[
            wspec(PB_QA, 0),
            wspec(PB_KA, -2), wspec(PB_KA, -1), wspec(PB_KA, 0),
            vspec(-2), vspec(-1), vspec(0),
            pl.BlockSpec((None, H_A, 3 * tq, tq), lambda bb, i: (jnp.minimum(i, 2), 0, 0, 0)),
        ],
        out_specs=pl.BlockSpec((tq, wa), lambda bb, i: (bb * nq + i, 0)),
        scratch_shapes=[
            pltpu.VMEM((wa, tq), F32),
            pltpu.VMEM((H_A, 3 * tq, tq), F32),
            pltpu.VMEM((H_A, 3 * tq, tq), BF16),
        ],
        compiler_params=_params(("arbitrary", "arbitrary"), VMEM_LIMIT),
        name="band_attn",
    )(proj, proj, proj, proj, vat, vat, vat, bias)


def _diff_kernel(q_ref, k_ref, vt_ref, lv_ref, sub_ref, o_ref, acc_ref, s_ref, p_ref, *,
                 lam_init):
    i = pl.program_id(1)
    tq = q_ref.shape[0]
    tk = vt_ref.shape[2]
    n_maps = 2 * H_B
    q = q_ref[...]
    qms = ()
    for h in range(H_B):
        qms += _head_halves(q[:, h * LANES:(h + 1) * LANES])
    acc_ref[...] = jnp.zeros(acc_ref.shape, F32)
    ck = lax.broadcasted_iota(I32, (tk, tq), 0) // CHUNK
    cq = lax.broadcasted_iota(I32, (tk, tq), 1) // CHUNK
    diag_ok = ck <= cq

    def fold(x):
        return x.reshape(x.shape[0] // SUBLANES, SUBLANES, tq)

    ones = jnp.ones((2 * SUBLANES, tk), BF16)

    def block(j, stats, mask):
        start = pl.multiple_of(j * tk, tk)
        mxs = []
        for c in range(n_maps):
            h = c // 2
            s = _dot_nt(k_ref[pl.ds(start, tk), h * LANES:(h + 1) * LANES], qms[c])
            if mask is not None:
                s = jnp.where(mask, s, NEG_MASK)
            s_ref[c] = s
            mxs.append(jnp.max(fold(s), axis=0))
        ms, ls = stats
        new_m, new_l, alphas = [], [], []
        for c in range(n_maps):
            m_new = jnp.maximum(ms[c], jnp.max(mxs[c], axis=0, keepdims=True))
            for r in range(tk // SLAB):
                p = jnp.exp2(s_ref[c, r * SLAB:(r + 1) * SLAB, :] - m_new)
                p_ref[c, r * SLAB:(r + 1) * SLAB, :] = p.astype(BF16)
            alpha = jnp.exp2(ms[c] - m_new)
            new_m.append(m_new)
            vt = jnp.concatenate([vt_ref[j, (c // 2) * LANES:(c // 2 + 1) * LANES, :], ones], axis=0)
            pv = _dot(vt, p_ref[c])
            acc_ref[c] = alpha * acc_ref[c] + pv[:LANES]
            new_l.append(alpha * ls[c] + pv[LANES:LANES + 1])
        return tuple(new_m), tuple(new_l)

    m0 = jnp.full((1, tq), NEG_INIT, F32)
    l0 = jnp.zeros((1, tq), F32)
    stats = lax.fori_loop(0, i, lambda j, st: block(j, st, None),
                          ((m0,) * n_maps, (l0,) * n_maps))
    _, ls = block(i, stats, diag_ok)

    lv = lv_ref[...]
    lam = (jnp.exp(jnp.sum(lv[0:1] * lv[1:2], axis=-1, keepdims=True))
           - jnp.exp(jnp.sum(lv[2:3] * lv[3:4], axis=-1, keepdims=True)) + lam_init)
    for h in range(H_B):
        o = acc_ref[2 * h] / ls[2 * h] - lam * (acc_ref[2 * h + 1] / ls[2 * h + 1])
        ms = jnp.mean(o * o, axis=0, keepdims=True)
        o = (o * lax.rsqrt(ms + EPS)) * sub_ref[...] * (1.0 - lam_init)
        o_ref[:, h * LANES:(h + 1) * LANES] = o.T.astype(o_ref.dtype)


def _diff_attention(proj, vbt, lambda_vec, subln_col, lam_init, b, s):
    tq = TQ_B
    nkb = s // tq
    wb = H_B * 2 * HEAD_DIM
    kern = functools.partial(_diff_kernel, lam_init=lam_init)
    return pl.pallas_call(
        kern,
        out_shape=jax.ShapeDtypeStruct((b * s, wb), BF16),
        grid=(b, nkb),
        in_specs=[
            pl.BlockSpec((None, tq, wb), lambda bb, i: (PB_QB, bb * nkb + i, 0)),
            pl.BlockSpec((None, s, wb), lambda bb, i: (PB_KB, bb, 0)),
            pl.BlockSpec((nkb, wb, tq), lambda bb, i: (bb, 0, 0)),
            pl.BlockSpec((4, HEAD_DIM), lambda bb, i: (0, 0)),
            pl.BlockSpec((LANES, 1), lambda bb, i: (0, 0)),
        ],
        out_specs=pl.BlockSpec((tq, wb), lambda bb, i: (bb * nkb + i, 0)),
        scratch_shapes=[
            pltpu.VMEM((2 * H_B, LANES, tq), F32),
            pltpu.VMEM((2 * H_B, tq, tq), F32),
            pltpu.VMEM((2 * H_B, tq, tq), BF16),
        ],
        compiler_params=_params(("arbitrary", "arbitrary"), VMEM_LIMIT),
        name="diff_attn",
    )(proj, proj, vbt, lambda_vec, subln_col)


def _sparse_kernel(q_ref, k_ref, vt_ref, tail_ref, ki_ref, o_ref,
                   keys_ref, hi_ref, lo_ref, acc_ref, s_ref, p_ref, b_ref, *, topk):
    i = pl.program_id(1)
    tq = q_ref.shape[0]
    tk = vt_ref.shape[2]

    wi = tail_ref[:, TAIL_WI:TAIL_WI + LANES]
    wt = wi.astype(F32).T * (H_IDX ** -0.5)
    w_rows = [wt[h:h + 1, :] for h in range(H_IDX)]
    qi_heads = (_head_halves(tail_ref[:, TAIL_QI:TAIL_QI + LANES])
                + _head_halves(tail_ref[:, TAIL_QI + LANES:TAIL_QI + 2 * LANES]))
    n_pairs = (i + 2) // 2

    def score_pair(jj, masked):
        start = pl.multiple_of(jj * 2 * tk, 2 * tk)
        kk = ki_ref[pl.ds(start, 2 * tk), :]
        for h in range(H_IDX):
            s_ref[2 * h:2 * h + 2] = _dot_nt(kk, qi_heads[h]).reshape(2, tk, tq)
        sc = jnp.zeros((2 * tk, tq), F32)
        for h in range(H_IDX):
            logits = s_ref[2 * h:2 * h + 2].reshape(2 * tk, tq)
            sc = sc + jnp.maximum(logits, 0.0) * w_rows[h]
        bits = pltpu.bitcast(sc, I32)
        key = bits ^ ((bits >> 31) & 0x7FFFFFFF)
        key = jnp.where(sc == 0.0, 0, key)
        if masked:
            ck = (lax.broadcasted_iota(I32, (2 * tk, tq), 0) + start) // CHUNK
            cq = (lax.broadcasted_iota(I32, (2 * tk, tq), 1) + i * tq) // CHUNK
            key = jnp.where(ck <= cq, key, INT_MIN)
        for half in range(2):
            part = key[half * tk:(half + 1) * tk]
            keys_ref[2 * jj + half] = part
            hi_ref[2 * jj + half] = (part >> 16).astype(I16)

    def score_body(jj, carry):
        score_pair(jj, False)
        return carry

    lax.fori_loop(0, n_pairs - 1, score_body, 0)
    score_pair(n_pairs - 1, True)

    n_acc = 4
    grp = 2 * SUBLANES

    def count16(ref, pred_fn):
        def inner(j, accs):
            accs = list(accs)
            for g in range(tk // grp):
                rows = ref[j, g * grp:(g + 1) * grp, :]
                a = accs[g % n_acc]
                accs[g % n_acc] = jnp.where(pred_fn(rows), a + 1, a)
            return tuple(accs)
        accs = lax.fori_loop(0, i + 1, inner, (jnp.zeros((grp, tq), I16),) * n_acc)
        acc = (accs[0] + accs[1]) + (accs[2] + accs[3])
        return jnp.sum(acc.astype(F32), axis=0, keepdims=True)

    def digit16(x):
        return jnp.broadcast_to(x, (grp, tq)).astype(I16)

    def radix16(ref, want):
        def body(it, prefix):
            cand_u = prefix | lax.shift_left(jnp.int32(1), 15 - it)
            cand = digit16(cand_u - 32768)
            cnt = count16(ref, lambda rows: rows >= cand)
            return jnp.where(cnt >= want, cand_u, prefix)
        return lax.fori_loop(0, 16, body, jnp.zeros((1, tq), I32)) - 32768

    t_hi = radix16(hi_ref, topk)
    t_hi16 = digit16(t_hi)
    n_above = count16(hi_ref, lambda rows: rows > t_hi16)

    def low_body(j, carry):
        key = keys_ref[j]
        lo = ((key ^ 0x8000) << 16) >> 16
        lo_ref[j] = jnp.where((key >> 16) == t_hi, lo, -32768).astype(I16)
        return carry

    lax.fori_loop(0, i + 1, low_body, 0)
    t_lo = radix16(lo_ref, topk - n_above)
    thr = (t_hi << 16) | (t_lo + 32768)

    def count(pred_fn):
        def inner(j, accs):
            accs = list(accs)
            for g in range(tk // SUBLANES):
                rows = keys_ref[j, g * SUBLANES:(g + 1) * SUBLANES, :]
                a = accs[g % n_acc]
                accs[g % n_acc] = jnp.where(pred_fn(rows), a + 1.0, a)
            return tuple(accs)
        accs = lax.fori_loop(0, i + 1, inner, (jnp.zeros((SUBLANES, tq), F32),) * n_acc)
        acc = (accs[0] + accs[1]) + (accs[2] + accs[3])
        return jnp.sum(acc, axis=0, keepdims=True)

    n_gt = count(lambda blk: blk > thr)
    need = jnp.where(thr == INT_MIN, 0.0, topk - n_gt)

    acc_ref[...] = jnp.zeros(acc_ref.shape, F32)
    q = q_ref[...]
    q_heads = ()
    for hp in range(H_C // 2):
        q_heads += _head_halves(q[:, hp * LANES:(hp + 1) * LANES])
    earlier = (lax.broadcasted_iota(I32, (tk, tk), 0)
               > lax.broadcasted_iota(I32, (tk, tk), 1))
    earlier = jnp.where(earlier, 1.0, 0.0).astype(BF16)

    def fold(x):
        return x.reshape(x.shape[0] // SUBLANES, SUBLANES, tq)

    ones = jnp.ones((2 * SUBLANES, tk), BF16)

    def attn_body(j, carry):
        tie_seen, ms, ls = carry
        start = pl.multiple_of(j * tk, tk)
        kblk = keys_ref[j]
        tie = kblk == thr
        tie_f = jnp.where(tie, 1.0, 0.0)
        rank = _dot(earlier, tie_f.astype(BF16)) + tie_seen
        sel = (kblk > thr) | (tie & (rank < need))
        b_ref[...] = jnp.where(sel, 0.0, NEG_MASK)
        mxs = []
        for h in range(H_C):
            hp = h // 2
            kb = k_ref[pl.ds(start, tk), hp * LANES:(hp + 1) * LANES]
            s = _dot_nt(kb, q_heads[h]) + b_ref[...]
            s_ref[h] = s
            mxs.append(jnp.max(fold(s), axis=0))
        new_m, new_l, alphas = [], [], []
        for h in range(H_C):
            m_new = jnp.maximum(ms[h], jnp.max(mxs[h], axis=0, keepdims=True))
            for r in range(tk // SLAB):
                p = jnp.exp2(s_ref[h, r * SLAB:(r + 1) * SLAB, :] - m_new)
                p_ref[h, r * SLAB:(r + 1) * SLAB, :] = p.astype(BF16)
            new_m.append(m_new)
            alphas.append(jnp.exp2(ms[h] - m_new))
        for h in range(H_C):
            rows = slice(h * HEAD_DIM, (h + 1) * HEAD_DIM)
            pv = _dot(jnp.concatenate([vt_ref[j, rows, :], ones], axis=0), p_ref[h])
            acc_ref[rows, :] = alphas[h] * acc_ref[rows, :] + pv[:HEAD_DIM]
            new_l.append(alphas[h] * ls[h] + pv[HEAD_DIM:HEAD_DIM + 1])
        return (tie_seen + jnp.sum(tie_f, axis=0, keepdims=True), tuple(new_m), tuple(new_l))

    m0 = jnp.full((1, tq), NEG_INIT, F32)
    l0 = jnp.zeros((1, tq), F32)
    _, _, ls = lax.fori_loop(0, i + 1, attn_body, (l0, (m0,) * H_C, (l0,) * H_C))

    for h in range(H_C):
        rows = slice(h * HEAD_DIM, (h + 1) * HEAD_DIM)
        acc_ref[rows, :] = acc_ref[rows, :] / ls[h]
    o_ref[...] = acc_ref[...].T.astype(o_ref.dtype)


def _sparse_attention(proj, vct, topk, b, s):
    tq = TQ
    wc = H_C * HEAD_DIM
    nkb = s // TK
    nq = s // tq
    kern = functools.partial(_sparse_kernel, topk=float(topk))
    return pl.pallas_call(
        kern,
        out_shape=jax.ShapeDtypeStruct((b * s, wc), BF16),
        grid=(b, nq),
        in_specs=[
            pl.BlockSpec((None, tq, wc), lambda bb, i: (PB_QC, bb * nq + i, 0)),
            pl.BlockSpec((None, s, wc), lambda bb, i: (PB_KC, bb, 0)),
            pl.BlockSpec((nkb, wc, TK), lambda bb, i: (bb, 0, 0)),
            pl.BlockSpec((None, tq, PROJ_TN), lambda bb, i: (PB_TAIL, bb * nq + i, 0)),
            pl.BlockSpec((None, s, LANES), lambda bb, i: (PB_TAIL, bb, TAIL_KI // LANES)),
        ],
        out_specs=pl.BlockSpec((tq, wc), lambda bb, i: (bb * nq + i, 0)),
        scratch_shapes=[
            pltpu.VMEM((nkb, TK, tq), I32),
            pltpu.VMEM((nkb, TK, tq), I16),
            pltpu.VMEM((nkb, TK, tq), I16),
            pltpu.VMEM((wc, tq), F32),
            pltpu.VMEM((H_C, TK, tq), F32),
            pltpu.VMEM((H_C, TK, tq), BF16),
            pltpu.VMEM((TK, tq), F32),
        ],
        compiler_params=_params(("arbitrary", "arbitrary"), VMEM_LIMIT),
        name="sparse_attn",
    )(proj, proj, vct, proj, proj)


def _mix_kernel(x_ref, oa_ref, ob_ref, oc_ref, ga_ref, gb_ref, gc_ref,
                ua_ref, ub_ref, uc_ref, wo_ref, o_ref):
    def gate(g_ref):
        return jnp.concatenate([g_ref[0], g_ref[1]], axis=1).astype(F32)

    y = gate(ga_ref) * _dot(oa_ref[...], ua_ref[...])
    y = y + gate(gb_ref) * _dot(ob_ref[...], ub_ref[...])
    y = y + gate(gc_ref) * _dot(oc_ref[...], uc_ref[...])
    o_ref[...] = x_ref[...] + _dot(y.astype(BF16), wo_ref[...])


def _mix_out(x2, oa, ob, oc, proj, ua, ub, uc, wo):
    t, d = x2.shape
    tm = 512
    per_gate = d // PROJ_TN
    assert PB_GATES % per_gate == 0
    gate_blk = PB_GATES // per_gate
    w = oa.shape[1]

    def full(shape):
        return pl.BlockSpec(shape, lambda i: (0, 0))

    return pl.pallas_call(
        _mix_kernel,
        out_shape=jax.ShapeDtypeStruct((t, d), F32),
        grid=(t // tm,),
        in_specs=[
            pl.BlockSpec((tm, d), lambda i: (i, 0)),
            pl.BlockSpec((tm, w), lambda i: (i, 0)),
            pl.BlockSpec((tm, w), lambda i: (i, 0)),
            pl.BlockSpec((tm, w), lambda i: (i, 0)),
            pl.BlockSpec((per_gate, tm, PROJ_TN), lambda i: (gate_blk, i, 0)),
            pl.BlockSpec((per_gate, tm, PROJ_TN), lambda i: (gate_blk + 1, i, 0)),
            pl.BlockSpec((per_gate, tm, PROJ_TN), lambda i: (gate_blk + 2, i, 0)),
            full((w, d)), full((w, d)), full((w, d)), full((d, d)),
        ],
        out_specs=pl.BlockSpec((tm, d), lambda i: (i, 0)),
        compiler_params=_params(("arbitrary",), VMEM_LIMIT),
        name="mix_out",
    )(x2, oa, ob, oc, proj, proj, proj, ua, ub, uc, wo)


def _mem_kv_kernel(mem_ref, g_ref, wk_ref, wvt_ref, k_ref, vt_ref):
    mem_n = _rms(mem_ref[...], g_ref[...]).astype(BF16)
    k_ref[...] = _dot(mem_n, wk_ref[...]).astype(k_ref.dtype)
    vt = _dot_nt(wvt_ref[...], mem_n).astype(vt_ref.dtype)
    n_mem = vt_ref.shape[2]
    for c in range(vt_ref.shape[0]):
        vt_ref[c] = vt[:, c * n_mem:(c + 1) * n_mem]


def _mem_kv(mem2, mem_norm, w_k, w_vt, n_mem):
    r, d = mem2.shape
    depth, _, n = w_k.shape
    tm = min(1024, r)
    return pl.pallas_call(
        _mem_kv_kernel,
        out_shape=(jax.ShapeDtypeStruct((depth, r, n), BF16),
                   jax.ShapeDtypeStruct((depth, r // n_mem, n, n_mem), BF16)),
        grid=(depth, r // tm),
        in_specs=[
            pl.BlockSpec((tm, d), lambda l, i: (i, 0)),
            pl.BlockSpec((1, d), lambda l, i: (0, 0)),
            pl.BlockSpec((None, d, n), lambda l, i: (l, 0, 0)),
            pl.BlockSpec((None, n, d), lambda l, i: (l, 0, 0)),
        ],
        out_specs=(pl.BlockSpec((None, tm, n), lambda l, i: (l, i, 0)),
                   pl.BlockSpec((None, tm // n_mem, n, n_mem), lambda l, i: (l, i, 0, 0))),
        compiler_params=_params(("arbitrary", "arbitrary"), VMEM_LIMIT),
        name="mem_kv",
    )(mem2, mem_norm, w_k, w_vt)


def _cross_kernel(x_ref, g_ref, wq_ref, k_ref, vt_ref, wo_ref, o_ref, acc_ref, s_ref, p_ref):
    n_mem = k_ref.shape[0]
    sub = acc_ref.shape[2]
    ones = jnp.ones((2 * SUBLANES, n_mem), BF16)
    for u in range(acc_ref.shape[0]):
        x = x_ref[u * sub:(u + 1) * sub, :]
        hn = _rms(x, g_ref[...]).astype(BF16)
        q = _dot(hn, wq_ref[...]).astype(BF16)
        ms = []
        for h in range(H_X):
            sl = slice((h // 2) * LANES, (h // 2 + 1) * LANES)
            s = _dot_nt(k_ref[:, sl], _head_halves(q[:, sl])[h % 2])
            s_ref[u, h] = s
            ms.append(jnp.max(s, axis=0, keepdims=True))
        for h in range(H_X):
            p_ref[u, h] = jnp.exp2(s_ref[u, h] - ms[h]).astype(BF16)
        for h in range(H_X):
            rows = slice(h * HEAD_DIM, (h + 1) * HEAD_DIM)
            pv = _dot(jnp.concatenate([vt_ref[rows, :], ones], axis=0), p_ref[u, h])
            acc_ref[u, rows, :] = pv[:HEAD_DIM] / pv[HEAD_DIM:HEAD_DIM + 1]
        o_ref[u * sub:(u + 1) * sub, :] = x + _dot(acc_ref[u].T.astype(BF16), wo_ref[...])


def _cross_attention(x2, gain, wq, k_mem, vt_mem, wo, seq):
    t, d = x2.shape
    sub = min(512, seq)
    tm = min(2 * sub, seq)
    per_batch = seq // tm
    n_mem, wk = k_mem.shape[0] // vt_mem.shape[0], k_mem.shape[1]
    return pl.pallas_call(
        _cross_kernel,
        out_shape=jax.ShapeDtypeStruct((t, d), F32),
        grid=(t // tm,),
        in_specs=[
            pl.BlockSpec((tm, d), lambda i: (i, 0)),
            pl.BlockSpec((1, d), lambda i: (0, 0)),
            pl.BlockSpec((d, wk), lambda i: (0, 0)),
            pl.BlockSpec((n_mem, wk), lambda i: (i // per_batch, 0)),
            pl.BlockSpec((None, wk, n_mem), lambda i: (i // per_batch, 0, 0)),
            pl.BlockSpec((wk, d), lambda i: (0, 0)),
        ],
        out_specs=pl.BlockSpec((tm, d), lambda i: (i, 0)),
        scratch_shapes=[
            pltpu.VMEM((tm // sub, wk, sub), F32),
            pltpu.VMEM((tm // sub, H_X, n_mem, sub), F32),
            pltpu.VMEM((tm // sub, H_X, n_mem, sub), BF16),
        ],
        compiler_params=_params(("arbitrary",), VMEM_LIMIT),
        name="cross_attn",
    )(x2, gain, wq, k_mem, vt_mem, wo)


def _ffn_kernel(x_ref, g_ref, wgu_ref, wd_ref, fg_ref, o_ref, acc_ref, *, final_norm, tf):
    x = x_ref[...]
    hn = _rms(x, g_ref[...]).astype(BF16)
    d_ff = wd_ref.shape[0]
    for c in range(d_ff // tf):
        gate = _dot(hn, wgu_ref[:, c * tf:(c + 1) * tf])
        up = _dot(hn, wgu_ref[:, d_ff + c * tf:d_ff + (c + 1) * tf])
        h = (gate / (1.0 + jnp.exp(-gate))) * up
        part = _dot(h.astype(BF16), wd_ref[c * tf:(c + 1) * tf, :])
        if c == 0:
            acc_ref[...] = part
        else:
            acc_ref[...] += part
    y = x + acc_ref[...]
    if final_norm:
        y = _rms(y, fg_ref[...])
    o_ref[...] = y


def _ffn(x2, gain, w_gu, w_down, final_gain, final_norm):
    t, d = x2.shape
    d_ff = w_down.shape[0]
    tm = min(512, t)
    tf = 2 * LANES
    assert d_ff % tf == 0
    kern = functools.partial(_ffn_kernel, final_norm=final_norm, tf=tf)
    resident = pl.Buffered(1)
    return pl.pallas_call(
        kern,
        out_shape=jax.ShapeDtypeStruct((t, d), F32),
        grid=(t // tm,),
        in_specs=[
            pl.BlockSpec((tm, d), lambda i: (i, 0)),
            pl.BlockSpec((1, d), lambda i: (0, 0)),
            pl.BlockSpec((d, 2 * d_ff), lambda i: (0, 0), pipeline_mode=resident),
            pl.BlockSpec((d_ff, d), lambda i: (0, 0), pipeline_mode=resident),
            pl.BlockSpec((1, d), lambda i: (0, 0)),
        ],
        out_specs=pl.BlockSpec((tm, d), lambda i: (i, 0)),
        scratch_shapes=[pltpu.VMEM((tm, d), F32)],
        compiler_params=_params(("arbitrary",), VMEM_LIMIT),
        name="ffn",
    )(x2, gain, w_gu, w_down, final_gain)


def _layout_in_weights(w_in):
    depth, d, _ = w_in.shape
    w_a, w_b, w_c = H_A * HEAD_DIM, H_B * 2 * HEAD_DIM, H_C * HEAD_DIM
    splits = (w_a, w_a, w_a, w_b, w_b, w_b, w_c, w_c, w_c, H_IDX * D_IDX, D_IDX, H_IDX, 3 * d)
    offs = [0] + [int(o) for o in np.cumsum(splits)]
    scale = HEAD_DIM ** -0.5
    col_scale = np.ones((offs[-1],), np.float32)
    for part in (0, 3, 6):
        col_scale[offs[part]:offs[part + 1]] = scale * math.log2(math.e)
    col_scale[offs[9]:offs[10]] = D_IDX ** -0.5
    w = (w_in * col_scale).astype(BF16)
    qa, ka, va, qb, kb, vb, qc, kc, vc, qi, ki, wi, gates = (
        w[:, :, offs[n]:offs[n + 1]] for n in range(len(splits)))
    pad = jnp.zeros((depth, d, LANES - H_IDX), BF16)
    w_main = jnp.concatenate([gates, qb, kb, qc, kc, qi, ki, ki, wi, pad, qa, ka], axis=2)
    assert w_main.shape[2] == PB_COUNT * PROJ_TN
    return w_main, va.swapaxes(1, 2), vb.swapaxes(1, 2), vc.swapaxes(1, 2)


def _rope_tables(seq):
    pos = jnp.arange(seq, dtype=F32)
    inv = ROPE_THETA ** (-jnp.arange(0, HEAD_DIM, 2, dtype=F32) / HEAD_DIM)
    ang = pos[:, None] * inv[None, :]
    cos, sin = jnp.cos(ang), jnp.sin(ang)
    zero = jnp.zeros_like(sin)
    reps = LANES // HEAD_DIM
    cos_t = jnp.tile(jnp.concatenate([cos, cos], axis=1), (1, reps))
    sina_t = jnp.tile(jnp.concatenate([-sin, zero], axis=1), (1, reps))
    sinb_t = jnp.tile(jnp.concatenate([zero, sin], axis=1), (1, reps))
    return cos_t, sina_t, sinb_t


def kernel(x, mem, norm_mix, w_in, rel_bias_a, lambda_vecs, subln_b, w_up_a, w_up_b, w_up_c,
           w_out, norm_cross, w_q_x, w_kv_x, w_o_x, norm_ffn, w_gu, w_down, mem_norm, final_norm):
    b, s, d = x.shape
    depth = w_in.shape[0]
    assert s % TQ_B == 0 and LEFT_CHUNKS * CHUNK == 2 * TQ and TQ == TK
    topk = min(TOPK_MAX, s // 4)
    cos_t, sina_t, sinb_t = _rope_tables(s)
    x2 = x.reshape(b * s, d)
    mem2 = mem.reshape(b * mem.shape[1], d)
    wkv = H_X * HEAD_DIM
    k_mem, vt_mem = _mem_kv(mem2, mem_norm.reshape(1, d), w_kv_x[:, :, :wkv].astype(BF16),
                            w_kv_x[:, :, wkv:].swapaxes(1, 2).astype(BF16), mem.shape[1])
    w_main, w_vat, w_vbt, w_vct = _layout_in_weights(w_in)
    ua, ub, uc, wo = (w.astype(BF16) for w in (w_up_a, w_up_b, w_up_c, w_out))
    wq_x = (w_q_x * (HEAD_DIM ** -0.5 * math.log2(math.e))).astype(BF16)
    wo_x = w_o_x.astype(BF16)
    wgu, wdn = w_gu.astype(BF16), w_down.astype(BF16)
    for l in range(depth):
        proj, vat, vbt, vct = _in_proj(x2, norm_mix[l].reshape(1, d), w_main[l], cos_t, sina_t,
                                       sinb_t, w_vat[l], w_vbt[l], w_vct[l], s)
        lam_init = 0.8 - 0.6 * math.exp(-0.3 * l)
        o_a = _band_attention(proj, vat, _band_bias(rel_bias_a[l], TQ), b, s)
        o_b = _diff_attention(proj, vbt, lambda_vecs[l].astype(F32),
                              subln_b[l].reshape(LANES, 1), lam_init, b, s)
        o_c = _sparse_attention(proj, vct, topk, b, s)
        x2 = _mix_out(x2, o_a, o_b, o_c, proj, ua[l], ub[l], uc[l], wo[l])
        x2 = _cross_attention(x2, norm_cross[l].reshape(1, d), wq_x[l], k_mem[l], vt_mem[l],
                              wo_x[l], s)
        x2 = _ffn(x2, norm_ffn[l].reshape(1, d), wgu[l], wdn[l],
                  final_norm.reshape(1, d), final_norm=(l == depth - 1))
    return x2.reshape(b, s, d)
```

```python
import functools
import math

import jax
import jax.numpy as jnp
import numpy as np
from jax import lax
from jax.experimental import pallas as pl
from jax.experimental.pallas import tpu as pltpu

F32 = jnp.float32
BF16 = jnp.bfloat16
I32 = jnp.int32
I16 = jnp.int16

CHUNK = 64
HEAD_DIM = 64
H_A = 8
LEFT_CHUNKS = 8
REL_MAX = 256
H_B = 4
H_C = 8
H_IDX = 4
D_IDX = 64
TOPK_MAX = 256
H_X = 4
ROPE_THETA = 10000.0
EPS = 1e-6

LANES = 128
SUBLANES = 8
TQ = 256
TK = 256
TQ_B = 512
SLAB = 128
PROJ_TN = 512
VMEM_LIMIT = 56 * 1024 * 1024

NEG_INIT = -1e30
NEG_MASK = -2e30
INT_MIN = -(2 ** 31)

PB_GATES = 0
PB_QB, PB_KB, PB_QC, PB_KC, PB_TAIL = 6, 7, 8, 9, 10
PB_QA, PB_KA = 11, 12
PB_COUNT = 13
TAIL_QI, TAIL_KI, TAIL_WI = 0, 256, 384


def _dot(a, b):
    return jnp.dot(a, b, preferred_element_type=F32)


def _dot_nt(a, b):
    return lax.dot_general(a, b, (((1,), (1,)), ((), ())), preferred_element_type=F32)


def _rms(x, g):
    ms = jnp.mean(x * x, axis=-1, keepdims=True)
    return (x * lax.rsqrt(ms + EPS)) * g


def _params(sem, vmem=None):
    return pltpu.CompilerParams(dimension_semantics=sem, vmem_limit_bytes=vmem)


def _row_groups(x):
    return x.reshape(x.shape[0] // SUBLANES, SUBLANES, x.shape[1])


def _head_halves(pair):
    lo = lax.broadcasted_iota(I32, pair.shape, 1) < HEAD_DIM
    zero = jnp.zeros_like(pair)
    return jnp.where(lo, pair, zero), jnp.where(lo, zero, pair)


def _in_proj_kernel(x_ref, g_ref, w_ref, cos_ref, sina_ref, sinb_ref, wva_ref, wvb_ref, wvc_ref,
                    o_ref, vat_ref, vbt_ref, vct_ref):
    xn = _rms(x_ref[...], g_ref[...]).astype(BF16)
    for wv_ref, vt_ref in ((wva_ref, vat_ref), (wvb_ref, vbt_ref), (wvc_ref, vct_ref)):
        vt = _dot_nt(wv_ref[...], xn).astype(BF16)
        w = vt_ref.shape[2]
        for c in range(vt_ref.shape[0]):
            vt_ref[c] = vt[:, c * w:(c + 1) * w]

    tn = o_ref.shape[2]
    half = HEAD_DIM // 2
    for j in range(PB_COUNT):
        a = _dot(xn, w_ref[:, j * tn:(j + 1) * tn])
        if PB_QB <= j <= PB_TAIL:
            for c in range(tn // LANES):
                sl = slice(c * LANES, (c + 1) * LANES)
                ac = a[:, sl]
                if j == PB_TAIL and c * LANES >= TAIL_WI:
                    o_ref[j, :, sl] = ac.astype(o_ref.dtype)
                else:
                    rot = (pltpu.roll(ac, LANES - half, 1) * sina_ref[...]
                           + pltpu.roll(ac, half, 1) * sinb_ref[...])
                    o_ref[j, :, sl] = (ac * cos_ref[...] + rot).astype(o_ref.dtype)
        elif j >= PB_QA:
            o_ref[j] = a.astype(o_ref.dtype)
        else:
            o_ref[j] = (1.0 / (1.0 + jnp.exp(-a))).astype(o_ref.dtype)


def _in_proj(x2, gain, w_main, cos_t, sina_t, sinb_t, w_vat, w_vbt, w_vct, seq):
    t, d = x2.shape
    tm = min(512, seq)
    tn = PROJ_TN
    pos_blocks = seq // tm
    wa, wb, wc = w_vat.shape[0], w_vbt.shape[0], w_vct.shape[0]
    resident = pl.Buffered(1)
    return pl.pallas_call(
        _in_proj_kernel,
        out_shape=(
            jax.ShapeDtypeStruct((PB_COUNT, t, tn), BF16),
            jax.ShapeDtypeStruct((t // TQ, wa, TQ), BF16),
            jax.ShapeDtypeStruct((t // TQ_B, wb, TQ_B), BF16),
            jax.ShapeDtypeStruct((t // TK, wc, TK), BF16),
        ),
        grid=(t // tm,),
        in_specs=[
            pl.BlockSpec((tm, d), lambda i: (i, 0)),
            pl.BlockSpec((1, d), lambda i: (0, 0)),
            pl.BlockSpec((d, PB_COUNT * tn), lambda i: (0, 0), pipeline_mode=resident),
            pl.BlockSpec((tm, LANES), lambda i: (i % pos_blocks, 0)),
            pl.BlockSpec((tm, LANES), lambda i: (i % pos_blocks, 0)),
            pl.BlockSpec((tm, LANES), lambda i: (i % pos_blocks, 0)),
            pl.BlockSpec((wa, d), lambda i: (0, 0), pipeline_mode=resident),
            pl.BlockSpec((wb, d), lambda i: (0, 0), pipeline_mode=resident),
            pl.BlockSpec((wc, d), lambda i: (0, 0), pipeline_mode=resident),
        ],
        out_specs=(
            pl.BlockSpec((PB_COUNT, tm, tn), lambda i: (0, i, 0)),
            pl.BlockSpec((tm // TQ, wa, TQ), lambda i: (i, 0, 0)),
            pl.BlockSpec((tm // TQ_B, wb, TQ_B), lambda i: (i, 0, 0)),
            pl.BlockSpec((tm // TK, wc, TK), lambda i: (i, 0, 0)),
        ),
        compiler_params=_params(("arbitrary",), VMEM_LIMIT),
        name="in_proj",
    )(x2, gain, w_main, cos_t, sina_t, sinb_t, w_vat, w_vbt, w_vct)


def _band_kernel(q_ref, k0_ref, k1_ref, k2_ref, vt0_ref, vt1_ref, vt2_ref, bias_ref, o_ref,
                 acc_ref, s_ref, p_ref):
    tq = q_ref.shape[0]
    ones = jnp.ones((2 * SUBLANES, 3 * tq), BF16)

    ms = []
    for h in range(H_A):
        sl = slice((h // 2) * LANES, (h // 2 + 1) * LANES)
        kw = jnp.concatenate([k0_ref[:, sl], k1_ref[:, sl], k2_ref[:, sl]], axis=0)
        s = _dot_nt(kw, _head_halves(q_ref[:, sl])[h % 2]) + bias_ref[h]
        s_ref[h] = s
        ms.append(jnp.max(jnp.max(_row_groups(s), axis=0), axis=0, keepdims=True))
    for h in range(H_A):
        for r in range(3 * tq // SLAB):
            p_ref[h, r * SLAB:(r + 1) * SLAB, :] = jnp.exp2(
                s_ref[h, r * SLAB:(r + 1) * SLAB, :] - ms[h]).astype(BF16)
    for h in range(H_A):
        rows = slice(h * HEAD_DIM, (h + 1) * HEAD_DIM)
        vt = jnp.concatenate([vt0_ref[0, rows, :], vt1_ref[0, rows, :], vt2_ref[0, rows, :]], axis=1)
        pv = _dot(jnp.concatenate([vt, ones], axis=0), p_ref[h])
        acc_ref[rows, :] = pv[:HEAD_DIM] / pv[HEAD_DIM:HEAD_DIM + 1]
    o_ref[...] = acc_ref[...].T.astype(o_ref.dtype)


def _band_bias(rel_bias, tq):
    back = LEFT_CHUNKS * CHUNK
    assert back == 2 * tq
    width = 4 * tq
    dist = back + tq - 1 - np.arange(width - 1)
    g = rel_bias[:, np.clip(dist, -(CHUNK - 1), REL_MAX) + (CHUNK - 1)].astype(F32)
    g = jnp.concatenate([g, jnp.zeros((g.shape[0], 1), F32)], axis=1) * math.log2(math.e)
    g = jnp.roll(g, -(tq - 1), axis=1)
    flat = jnp.tile(g, (1, tq))[:, :tq * (width - 1)]
    bias = flat.reshape(-1, tq, width - 1)[:, :, :3 * tq]
    qi = np.arange(tq)[:, None] + back
    kj = np.arange(3 * tq)[None, :]
    dc = qi // CHUNK - kj // CHUNK
    in_band = (dc >= 0) & (dc <= LEFT_CHUNKS)
    valid = np.stack([in_band & (kj >= (2 - v) * tq) for v in range(3)])
    bias = jnp.where(jnp.asarray(valid)[:, None], bias[None], NEG_MASK)
    return bias.transpose(0, 1, 3, 2)


def _band_attention(proj, vat, bias, b, s):
    tq = TQ
    nq = s // tq
    wa = H_A * HEAD_DIM

    def wspec(blk, off):
        return pl.BlockSpec((None, tq, wa),
                            lambda bb, i: (blk, bb * nq + jnp.maximum(i + off, 0), 0))

    def vspec(off):
        return pl.BlockSpec((1, wa, tq), lambda bb, i: (bb * nq + jnp.maximum(i + off, 0), 0, 0))

    return pl.pallas_call(
        _band_kernel,
        out_shape=jax.ShapeDtypeStruct((b * s, wa), BF16),
        grid=(b, nq),
        in_specs=[
            wspec(PB_QA, 0),
            wspec(PB_KA, -2), wspec(PB_KA, -1), wspec(PB_KA, 0),
            vspec(-2), vspec(-1), vspec(0),
            pl.BlockSpec((None, H_A, 3 * tq, tq), lambda bb, i: (jnp.minimum(i, 2), 0, 0, 0)),
        ],
        out_specs=pl.BlockSpec((tq, wa), lambda bb, i: (bb * nq + i, 0)),
        scratch_shapes=[
            pltpu.VMEM((wa, tq), F32),
            pltpu.VMEM((H_A, 3 * tq, tq), F32),
            pltpu.VMEM((H_A, 3 * tq, tq), BF16),
        ],
        compiler_params=_params(("arbitrary", "arbitrary"), VMEM_LIMIT),
        name="band_attn",
    )(proj, proj, proj, proj, vat, vat, vat, bias)


def _diff_kernel(q_ref, k_ref, vt_ref, lv_ref, sub_ref, o_ref, acc_ref, s_ref, p_ref, *,
                 lam_init):
    i = pl.program_id(1)
    tq = q_ref.shape[0]
    tk = vt_ref.shape[2]
    n_maps = 2 * H_B
    q = q_ref[...]
    qms = ()
    for h in range(H_B):
        qms += _head_halves(q[:, h * LANES:(h + 1) * LANES])
    acc_ref[...] = jnp.zeros(acc_ref.shape, F32)
    ck = lax.broadcasted_iota(I32, (tk, tq), 0) // CHUNK
    cq = lax.broadcasted_iota(I32, (tk, tq), 1) // CHUNK
    diag_ok = ck <= cq

    ones = jnp.ones((2 * SUBLANES, tk), BF16)

    def block(j, stats, mask):
        start = pl.multiple_of(j * tk, tk)
        mxs = []
        for c in range(n_maps):
            h = c // 2
            s = _dot_nt(k_ref[pl.ds(start, tk), h * LANES:(h + 1) * LANES], qms[c])
            if mask is not None:
                s = jnp.where(mask, s, NEG_MASK)
            s_ref[c] = s
            mxs.append(jnp.max(_row_groups(s), axis=0))
        ms, ls = stats
        new_m, new_l = [], []
        for c in range(n_maps):
            m_new = jnp.maximum(ms[c], jnp.max(mxs[c], axis=0, keepdims=True))
            for r in range(tk // SLAB):
                p = jnp.exp2(s_ref[c, r * SLAB:(r + 1) * SLAB, :] - m_new)
                p_ref[c, r * SLAB:(r + 1) * SLAB, :] = p.astype(BF16)
            alpha = jnp.exp2(ms[c] - m_new)
            new_m.append(m_new)
            vt = jnp.concatenate([vt_ref[j, (c // 2) * LANES:(c // 2 + 1) * LANES, :], ones], axis=0)
            pv = _dot(vt, p_ref[c])
            acc_ref[c] = alpha * acc_ref[c] + pv[:LANES]
            new_l.append(alpha * ls[c] + pv[LANES:LANES + 1])
        return tuple(new_m), tuple(new_l)

    m0 = jnp.full((1, tq), NEG_INIT, F32)
    l0 = jnp.zeros((1, tq), F32)
    stats = lax.fori_loop(0, i, lambda j, st: block(j, st, None),
                          ((m0,) * n_maps, (l0,) * n_maps))
    _, ls = block(i, stats, diag_ok)

    lv = lv_ref[...]
    lam = (jnp.exp(jnp.sum(lv[0:1] * lv[1:2], axis=-1, keepdims=True))
           - jnp.exp(jnp.sum(lv[2:3] * lv[3:4], axis=-1, keepdims=True)) + lam_init)
    for h in range(H_B):
        o = acc_ref[2 * h] / ls[2 * h] - lam * (acc_ref[2 * h + 1] / ls[2 * h + 1])
        ms = jnp.mean(o * o, axis=0, keepdims=True)
        o = (o * lax.rsqrt(ms + EPS)) * sub_ref[...] * (1.0 - lam_init)
        o_ref[:, h * LANES:(h + 1) * LANES] = o.T.astype(o_ref.dtype)


def _diff_attention(proj, vbt, lambda_vec, subln_col, lam_init, b, s):
    tq = TQ_B
    nkb = s // tq
    wb = H_B * 2 * HEAD_DIM
    kern = functools.partial(_diff_kernel, lam_init=lam_init)
    return pl.pallas_call(
        kern,
        out_shape=jax.ShapeDtypeStruct((b * s, wb), BF16),
        grid=(b, nkb),
        in_specs=[
            pl.BlockSpec((None, tq, wb), lambda bb, i: (PB_QB, bb * nkb + i, 0)),
            pl.BlockSpec((None, s, wb), lambda bb, i: (PB_KB, bb, 0)),
            pl.BlockSpec((nkb, wb, tq), lambda bb, i: (bb, 0, 0)),
            pl.BlockSpec((4, HEAD_DIM), lambda bb, i: (0, 0)),
            pl.BlockSpec((LANES, 1), lambda bb, i: (0, 0)),
        ],
        out_specs=pl.BlockSpec((tq, wb), lambda bb, i: (bb * nkb + i, 0)),
        scratch_shapes=[
            pltpu.VMEM((2 * H_B, LANES, tq), F32),
            pltpu.VMEM((2 * H_B, tq, tq), F32),
            pltpu.VMEM((2 * H_B, tq, tq), BF16),
        ],
        compiler_params=_params(("arbitrary", "arbitrary"), VMEM_LIMIT),
        name="diff_attn",
    )(proj, proj, vbt, lambda_vec, subln_col)


def _sparse_kernel(q_ref, k_ref, vt_ref, tail_ref, ki_ref, o_ref,
                   keys_ref, hi_ref, lo_ref, acc_ref, s_ref, p_ref, b_ref, *, topk):
    i = pl.program_id(1)
    tq = q_ref.shape[0]
    tk = vt_ref.shape[2]

    wi = tail_ref[:, TAIL_WI:TAIL_WI + LANES]
    wt = wi.astype(F32).T * (H_IDX ** -0.5)
    w_rows = [wt[h:h + 1, :] for h in range(H_IDX)]
    qi_heads = (_head_halves(tail_ref[:, TAIL_QI:TAIL_QI + LANES])
                + _head_halves(tail_ref[:, TAIL_QI + LANES:TAIL_QI + 2 * LANES]))
    n_pairs = (i + 2) // 2

    def score_pair(jj, masked):
        start = pl.multiple_of(jj * 2 * tk, 2 * tk)
        kk = ki_ref[pl.ds(start, 2 * tk), :]
        for h in range(H_IDX):
            s_ref[2 * h:2 * h + 2] = _dot_nt(kk, qi_heads[h]).reshape(2, tk, tq)
        sc = jnp.zeros((2 * tk, tq), F32)
        for h in range(H_IDX):
            logits = s_ref[2 * h:2 * h + 2].reshape(2 * tk, tq)
            sc = sc + jnp.maximum(logits, 0.0) * w_rows[h]
        bits = pltpu.bitcast(sc, I32)
        key = bits ^ ((bits >> 31) & 0x7FFFFFFF)
        key = jnp.where(sc == 0.0, 0, key)
        if masked:
            ck = (lax.broadcasted_iota(I32, (2 * tk, 1), 0) + start) // CHUNK
            cq = (lax.broadcasted_iota(I32, (1, tq), 1) + i * tq) // CHUNK
            key = jnp.where(ck <= cq, key, INT_MIN)
        for half in range(2):
            part = key[half * tk:(half + 1) * tk]
            keys_ref[2 * jj + half] = part
            hi_ref[2 * jj + half] = (part >> 16).astype(I16)

    def score_body(jj, carry):
        score_pair(jj, False)
        return carry

    lax.fori_loop(0, n_pairs - 1, score_body, 0)
    score_pair(n_pairs - 1, True)

    n_acc = 4
    grp = 2 * SUBLANES

    def count16(ref, pred_fn):
        def inner(j, accs):
            accs = list(accs)
            for g in range(tk // grp):
                rows = ref[j, g * grp:(g + 1) * grp, :]
                a = accs[g % n_acc]
                accs[g % n_acc] = jnp.where(pred_fn(rows), a + 1, a)
            return tuple(accs)
        accs = lax.fori_loop(0, i + 1, inner, (jnp.zeros((grp, tq), I16),) * n_acc)
        acc = (accs[0] + accs[1]) + (accs[2] + accs[3])
        return jnp.sum(acc.astype(F32), axis=0, keepdims=True)

    def digit16(x):
        return jnp.broadcast_to(x, (grp, tq)).astype(I16)

    def radix16(ref, want):
        def body(it, prefix):
            cand_u = prefix | lax.shift_left(jnp.int32(1), 15 - it)
            cand = digit16(cand_u - 32768)
            cnt = count16(ref, lambda rows: rows >= cand)
            return jnp.where(cnt >= want, cand_u, prefix)
        return lax.fori_loop(0, 16, body, jnp.zeros((1, tq), I32)) - 32768

    t_hi = radix16(hi_ref, topk)
    t_hi16 = digit16(t_hi)
    n_above = count16(hi_ref, lambda rows: rows > t_hi16)

    def low_body(j, carry):
        key = keys_ref[j]
        lo = ((key ^ 0x8000) << 16) >> 16
        lo_ref[j] = jnp.where((key >> 16) == t_hi, lo, -32768).astype(I16)
        return carry

    lax.fori_loop(0, i + 1, low_body, 0)
    t_lo = radix16(lo_ref, topk - n_above)
    thr = (t_hi << 16) | (t_lo + 32768)

    def count(pred_fn):
        def inner(j, accs):
            accs = list(accs)
            for g in range(tk // SUBLANES):
                rows = keys_ref[j, g * SUBLANES:(g + 1) * SUBLANES, :]
                a = accs[g % n_acc]
                accs[g % n_acc] = jnp.where(pred_fn(rows), a + 1.0, a)
            return tuple(accs)
        accs = lax.fori_loop(0, i + 1, inner, (jnp.zeros((SUBLANES, tq), F32),) * n_acc)
        acc = (accs[0] + accs[1]) + (accs[2] + accs[3])
        return jnp.sum(acc, axis=0, keepdims=True)

    n_gt = count(lambda blk: blk > thr)
    need = jnp.where(thr == INT_MIN, 0.0, topk - n_gt)

    acc_ref[...] = jnp.zeros(acc_ref.shape, F32)
    q = q_ref[...]
    q_heads = ()
    for hp in range(H_C // 2):
        q_heads += _head_halves(q[:, hp * LANES:(hp + 1) * LANES])
    earlier = (lax.broadcasted_iota(I32, (tk, tk), 0)
               > lax.broadcasted_iota(I32, (tk, tk), 1))
    earlier = jnp.where(earlier, 1.0, 0.0).astype(BF16)

    ones = jnp.ones((2 * SUBLANES, tk), BF16)

    def attn_body(j, carry):
        tie_seen, ms, ls = carry
        start = pl.multiple_of(j * tk, tk)
        kblk = keys_ref[j]
        tie = kblk == thr
        tie_f = jnp.where(tie, 1.0, 0.0)
        rank = _dot(earlier, tie_f.astype(BF16)) + tie_seen
        sel = (kblk > thr) | (tie & (rank < need))
        b_ref[...] = jnp.where(sel, 0.0, NEG_MASK)
        mxs = []
        for h in range(H_C):
            hp = h // 2
            kb = k_ref[pl.ds(start, tk), hp * LANES:(hp + 1) * LANES]
            s = _dot_nt(kb, q_heads[h]) + b_ref[...]
            s_ref[h] = s
            mxs.append(jnp.max(_row_groups(s), axis=0))
        new_m, new_l, alphas = [], [], []
        for h in range(H_C):
            m_new = jnp.maximum(ms[h], jnp.max(mxs[h], axis=0, keepdims=True))
            for r in range(tk // SLAB):
                p = jnp.exp2(s_ref[h, r * SLAB:(r + 1) * SLAB, :] - m_new)
                p_ref[h, r * SLAB:(r + 1) * SLAB, :] = p.astype(BF16)
            new_m.append(m_new)
            alphas.append(jnp.exp2(ms[h] - m_new))
        for h in range(H_C):
            rows = slice(h * HEAD_DIM, (h + 1) * HEAD_DIM)
            pv = _dot(jnp.concatenate([vt_ref[j, rows, :], ones], axis=0), p_ref[h])
            acc_ref[rows, :] = alphas[h] * acc_ref[rows, :] + pv[:HEAD_DIM]
            new_l.append(alphas[h] * ls[h] + pv[HEAD_DIM:HEAD_DIM + 1])
        return (tie_seen + jnp.sum(tie_f, axis=0, keepdims=True), tuple(new_m), tuple(new_l))

    m0 = jnp.full((1, tq), NEG_INIT, F32)
    l0 = jnp.zeros((1, tq), F32)
    _, _, ls = lax.fori_loop(0, i + 1, attn_body, (l0, (m0,) * H_C, (l0,) * H_C))

    for h in range(H_C):
        rows = slice(h * HEAD_DIM, (h + 1) * HEAD_DIM)
        acc_ref[rows, :] = acc_ref[rows, :] / ls[h]
    o_ref[...] = acc_ref[...].T.astype(o_ref.dtype)


def _sparse_attention(proj, vct, topk, b, s):
    tq = TQ
    wc = H_C * HEAD_DIM
    nkb = s // TK
    nq = s // tq
    kern = functools.partial(_sparse_kernel, topk=float(topk))
    return pl.pallas_call(
        kern,
        out_shape=jax.ShapeDtypeStruct((b * s, wc), BF16),
        grid=(b, nq),
        in_specs=[
            pl.BlockSpec((None, tq, wc), lambda bb, i: (PB_QC, bb * nq + i, 0)),
            pl.BlockSpec((None, s, wc), lambda bb, i: (PB_KC, bb, 0)),
            pl.BlockSpec((nkb, wc, TK), lambda bb, i: (bb, 0, 0)),
            pl.BlockSpec((None, tq, PROJ_TN), lambda bb, i: (PB_TAIL, bb * nq + i, 0)),
            pl.BlockSpec((None, s, LANES), lambda bb, i: (PB_TAIL, bb, TAIL_KI // LANES)),
        ],
        out_specs=pl.BlockSpec((tq, wc), lambda bb, i: (bb * nq + i, 0)),
        scratch_shapes=[
            pltpu.VMEM((nkb, TK, tq), I32),
            pltpu.VMEM((nkb, TK, tq), I16),
            pltpu.VMEM((nkb, TK, tq), I16),
            pltpu.VMEM((wc, tq), F32),
            pltpu.VMEM((H_C, TK, tq), F32),
            pltpu.VMEM((H_C, TK, tq), BF16),
            pltpu.VMEM((TK, tq), F32),
        ],
        compiler_params=_params(("arbitrary", "arbitrary"), VMEM_LIMIT),
        name="sparse_attn",
    )(proj, proj, vct, proj, proj)


def _mix_kernel(x_ref, oa_ref, ob_ref, oc_ref, ga_ref, gb_ref, gc_ref,
                ua_ref, ub_ref, uc_ref, wo_ref, o_ref):
    def gate(g_ref):
        return jnp.concatenate([g_ref[0], g_ref[1]], axis=1).astype(F32)

    y = gate(ga_ref) * _dot(oa_ref[...], ua_ref[...])
    y = y + gate(gb_ref) * _dot(ob_ref[...], ub_ref[...])
    y = y + gate(gc_ref) * _dot(oc_ref[...], uc_ref[...])
    o_ref[...] = x_ref[...] + _dot(y.astype(BF16), wo_ref[...])


def _mix_out(x2, oa, ob, oc, proj, ua, ub, uc, wo):
    t, d = x2.shape
    tm = 512
    per_gate = d // PROJ_TN
    assert PB_GATES % per_gate == 0
    gate_blk = PB_GATES // per_gate
    w = oa.shape[1]

    def full(shape):
        return pl.BlockSpec(shape, lambda i: (0, 0))

    return pl.pallas_call(
        _mix_kernel,
        out_shape=jax.ShapeDtypeStruct((t, d), F32),
        grid=(t // tm,),
        in_specs=[
            pl.BlockSpec((tm, d), lambda i: (i, 0)),
            pl.BlockSpec((tm, w), lambda i: (i, 0)),
            pl.BlockSpec((tm, w), lambda i: (i, 0)),
            pl.BlockSpec((tm, w), lambda i: (i, 0)),
            pl.BlockSpec((per_gate, tm, PROJ_TN), lambda i: (gate_blk, i, 0)),
            pl.BlockSpec((per_gate, tm, PROJ_TN), lambda i: (gate_blk + 1, i, 0)),
            pl.BlockSpec((per_gate, tm, PROJ_TN), lambda i: (gate_blk + 2, i, 0)),
            full((w, d)), full((w, d)), full((w, d)), full((d, d)),
        ],
        out_specs=pl.BlockSpec((tm, d), lambda i: (i, 0)),
        compiler_params=_params(("arbitrary",), VMEM_LIMIT),
        name="mix_out",
    )(x2, oa, ob, oc, proj, proj, proj, ua, ub, uc, wo)


def _mem_kv_kernel(mem_ref, g_ref, wk_ref, wvt_ref, k_ref, vt_ref):
    mem_n = _rms(mem_ref[...], g_ref[...]).astype(BF16)
    k_ref[...] = _dot(mem_n, wk_ref[...]).astype(k_ref.dtype)
    vt = _dot_nt(wvt_ref[...], mem_n).astype(vt_ref.dtype)
    n_mem = vt_ref.shape[2]
    for c in range(vt_ref.shape[0]):
        vt_ref[c] = vt[:, c * n_mem:(c + 1) * n_mem]


def _mem_kv(mem2, mem_norm, w_k, w_vt, n_mem):
    r, d = mem2.shape
    depth, _, n = w_k.shape
    tm = min(1024, r)
    return pl.pallas_call(
        _mem_kv_kernel,
        out_shape=(jax.ShapeDtypeStruct((depth, r, n), BF16),
                   jax.ShapeDtypeStruct((depth, r // n_mem, n, n_mem), BF16)),
        grid=(depth, r // tm),
        in_specs=[
            pl.BlockSpec((tm, d), lambda l, i: (i, 0)),
            pl.BlockSpec((1, d), lambda l, i: (0, 0)),
            pl.BlockSpec((None, d, n), lambda l, i: (l, 0, 0)),
            pl.BlockSpec((None, n, d), lambda l, i: (l, 0, 0)),
        ],
        out_specs=(pl.BlockSpec((None, tm, n), lambda l, i: (l, i, 0)),
                   pl.BlockSpec((None, tm // n_mem, n, n_mem), lambda l, i: (l, i, 0, 0))),
        compiler_params=_params(("arbitrary", "arbitrary"), VMEM_LIMIT),
        name="mem_kv",
    )(mem2, mem_norm, w_k, w_vt)


def _cross_kernel(x_ref, g_ref, wq_ref, k_ref, vt_ref, wo_ref, o_ref, acc_ref, s_ref, p_ref):
    n_mem = k_ref.shape[0]
    sub = acc_ref.shape[2]
    ones = jnp.ones((2 * SUBLANES, n_mem), BF16)
    for u in range(acc_ref.shape[0]):
        x = x_ref[u * sub:(u + 1) * sub, :]
        hn = _rms(x, g_ref[...]).astype(BF16)
        q = _dot(hn, wq_ref[...]).astype(BF16)
        ms = []
        for h in range(H_X):
            sl = slice((h // 2) * LANES, (h // 2 + 1) * LANES)
            s = _dot_nt(k_ref[:, sl], _head_halves(q[:, sl])[h % 2])
            s_ref[u, h] = s
            ms.append(jnp.max(s, axis=0, keepdims=True))
        for h in range(H_X):
            p_ref[u, h] = jnp.exp2(s_ref[u, h] - ms[h]).astype(BF16)
        for h in range(H_X):
            rows = slice(h * HEAD_DIM, (h + 1) * HEAD_DIM)
            pv = _dot(jnp.concatenate([vt_ref[rows, :], ones], axis=0), p_ref[u, h])
            acc_ref[u, rows, :] = pv[:HEAD_DIM] / pv[HEAD_DIM:HEAD_DIM + 1]
        o_ref[u * sub:(u + 1) * sub, :] = x + _dot(acc_ref[u].T.astype(BF16), wo_ref[...])


def _cross_attention(x2, gain, wq, k_mem, vt_mem, wo, seq):
    t, d = x2.shape
    sub = min(512, seq)
    tm = min(2 * sub, seq)
    per_batch = seq // tm
    n_mem, wk = k_mem.shape[0] // vt_mem.shape[0], k_mem.shape[1]
    return pl.pallas_call(
        _cross_kernel,
        out_shape=jax.ShapeDtypeStruct((t, d), F32),
        grid=(t // tm,),
        in_specs=[
            pl.BlockSpec((tm, d), lambda i: (i, 0)),
            pl.BlockSpec((1, d), lambda i: (0, 0)),
            pl.BlockSpec((d, wk), lambda i: (0, 0)),
            pl.BlockSpec((n_mem, wk), lambda i: (i // per_batch, 0)),
            pl.BlockSpec((None, wk, n_mem), lambda i: (i // per_batch, 0, 0)),
            pl.BlockSpec((wk, d), lambda i: (0, 0)),
        ],
        out_specs=pl.BlockSpec((tm, d), lambda i: (i, 0)),
        scratch_shapes=[
            pltpu.VMEM((tm // sub, wk, sub), F32),
            pltpu.VMEM((tm // sub, H_X, n_mem, sub), F32),
            pltpu.VMEM((tm // sub, H_X, n_mem, sub), BF16),
        ],
        compiler_params=_params(("arbitrary",), VMEM_LIMIT),
        name="cross_attn",
    )(x2, gain, wq, k_mem, vt_mem, wo)


def _ffn_kernel(x_ref, g_ref, wgu_ref, wd_ref, fg_ref, o_ref, acc_ref, *, final_norm, tf):
    x = x_ref[...]
    hn = _rms(x, g_ref[...]).astype(BF16)
    d_ff = wd_ref.shape[0]
    for c in range(d_ff // tf):
        gate = _dot(hn, wgu_ref[:, c * tf:(c + 1) * tf])
        up = _dot(hn, wgu_ref[:, d_ff + c * tf:d_ff + (c + 1) * tf])
        h = (gate / (1.0 + jnp.exp(-gate))) * up
        part = _dot(h.astype(BF16), wd_ref[c * tf:(c + 1) * tf, :])
        if c == 0:
            acc_ref[...] = part
        else:
            acc_ref[...] += part
    y = x + acc_ref[...]
    if final_norm:
        y = _rms(y, fg_ref[...])
    o_ref[...] = y


def _ffn(x2, gain, w_gu, w_down, final_gain, final_norm):
    t, d = x2.shape
    d_ff = w_down.shape[0]
    tm = min(512, t)
    tf = 2 * LANES
    assert d_ff % tf == 0
    kern = functools.partial(_ffn_kernel, final_norm=final_norm, tf=tf)
    resident = pl.Buffered(1)
    return pl.pallas_call(
        kern,
        out_shape=jax.ShapeDtypeStruct((t, d), F32),
        grid=(t // tm,),
        in_specs=[
            pl.BlockSpec((tm, d), lambda i: (i, 0)),
            pl.BlockSpec((1, d), lambda i: (0, 0)),
            pl.BlockSpec((d, 2 * d_ff), lambda i: (0, 0), pipeline_mode=resident),
            pl.BlockSpec((d_ff, d), lambda i: (0, 0), pipeline_mode=resident),
            pl.BlockSpec((1, d), lambda i: (0, 0)),
        ],
        out_specs=pl.BlockSpec((tm, d), lambda i: (i, 0)),
        scratch_shapes=[pltpu.VMEM((tm, d), F32)],
        compiler_params=_params(("arbitrary",), VMEM_LIMIT),
        name="ffn",
    )(x2, gain, w_gu, w_down, final_gain)


def _layout_in_weights(w_in):
    depth, d, _ = w_in.shape
    w_a, w_b, w_c = H_A * HEAD_DIM, H_B * 2 * HEAD_DIM, H_C * HEAD_DIM
    splits = (w_a, w_a, w_a, w_b, w_b, w_b, w_c, w_c, w_c, H_IDX * D_IDX, D_IDX, H_IDX, 3 * d)
    offs = [0] + [int(o) for o in np.cumsum(splits)]
    scale = HEAD_DIM ** -0.5
    col_scale = np.ones((offs[-1],), np.float32)
    for part in (0, 3, 6):
        col_scale[offs[part]:offs[part + 1]] = scale * math.log2(math.e)
    col_scale[offs[9]:offs[10]] = D_IDX ** -0.5
    w = (w_in * col_scale).astype(BF16)
    qa, ka, va, qb, kb, vb, qc, kc, vc, qi, ki, wi, gates = (
        w[:, :, offs[n]:offs[n + 1]] for n in range(len(splits)))
    pad = jnp.zeros((depth, d, LANES - H_IDX), BF16)
    w_main = jnp.concatenate([gates, qb, kb, qc, kc, qi, ki, ki, wi, pad, qa, ka], axis=2)
    assert w_main.shape[2] == PB_COUNT * PROJ_TN
    return w_main, va.swapaxes(1, 2), vb.swapaxes(1, 2), vc.swapaxes(1, 2)


def _rope_tables(seq):
    pos = jnp.arange(seq, dtype=F32)
    inv = ROPE_THETA ** (-jnp.arange(0, HEAD_DIM, 2, dtype=F32) / HEAD_DIM)
    ang = pos[:, None] * inv[None, :]
    cos, sin = jnp.cos(ang), jnp.sin(ang)
    zero = jnp.zeros_like(sin)
    reps = LANES // HEAD_DIM
    cos_t = jnp.tile(jnp.concatenate([cos, cos], axis=1), (1, reps))
    sina_t = jnp.tile(jnp.concatenate([-sin, zero], axis=1), (1, reps))
    sinb_t = jnp.tile(jnp.concatenate([zero, sin], axis=1), (1, reps))
    return cos_t, sina_t, sinb_t


def kernel(x, mem, norm_mix, w_in, rel_bias_a, lambda_vecs, subln_b, w_up_a, w_up_b, w_up_c,
           w_out, norm_cross, w_q_x, w_kv_x, w_o_x, norm_ffn, w_gu, w_down, mem_norm, final_norm):
    b, s, d = x.shape
    depth = w_in.shape[0]
    assert s % TQ_B == 0 and LEFT_CHUNKS * CHUNK == 2 * TQ and TQ == TK
    topk = min(TOPK_MAX, s // 4)
    cos_t, sina_t, sinb_t = _rope_tables(s)
    x2 = x.reshape(b * s, d)
    mem2 = mem.reshape(b * mem.shape[1], d)
    wkv = H_X * HEAD_DIM
    k_mem, vt_mem = _mem_kv(mem2, mem_norm.reshape(1, d), w_kv_x[:, :, :wkv].astype(BF16),
                            w_kv_x[:, :, wkv:].swapaxes(1, 2).astype(BF16), mem.shape[1])
    w_main, w_vat, w_vbt, w_vct = _layout_in_weights(w_in)
    ua, ub, uc, wo = (w.astype(BF16) for w in (w_up_a, w_up_b, w_up_c, w_out))
    wq_x = (w_q_x * (HEAD_DIM ** -0.5 * math.log2(math.e))).astype(BF16)
    wo_x = w_o_x.astype(BF16)
    wgu, wdn = w_gu.astype(BF16), w_down.astype(BF16)
    for l in range(depth):
        proj, vat, vbt, vct = _in_proj(x2, norm_mix[l].reshape(1, d), w_main[l], cos_t, sina_t,
                                       sinb_t, w_vat[l], w_vbt[l], w_vct[l], s)
        lam_init = 0.8 - 0.6 * math.exp(-0.3 * l)
        o_a = _band_attention(proj, vat, _band_bias(rel_bias_a[l], TQ), b, s)
        o_b = _diff_attention(proj, vbt, lambda_vecs[l].astype(F32),
                              subln_b[l].reshape(LANES, 1), lam_init, b, s)
        o_c = _sparse_attention(proj, vct, topk, b, s)
        x2 = _mix_out(x2, o_a, o_b, o_c, proj, ua[l], ub[l], uc[l], wo[l])
        x2 = _cross_attention(x2, norm_cross[l].reshape(1, d), wq_x[l], k_mem[l], vt_mem[l],
                              wo_x[l], s)
        x2 = _ffn(x2, norm_ffn[l].reshape(1, d), wgu[l], wdn[l],
                  final_norm.reshape(1, d), final_norm=(l == depth - 1))
    return x2.reshape(b, s, d)
```

```python
import functools
import math

import jax
import jax.numpy as jnp
import numpy as np
from jax import lax
from jax.experimental import pallas as pl
from jax.experimental.pallas import tpu as pltpu

F32 = jnp.float32
BF16 = jnp.bfloat16
I32 = jnp.int32
I16 = jnp.int16

CHUNK = 64
HEAD_DIM = 64
H_A = 8
LEFT_CHUNKS = 8
REL_MAX = 256
H_B = 4
H_C = 8
H_IDX = 4
D_IDX = 64
TOPK_MAX = 256
H_X = 4
ROPE_THETA = 10000.0
EPS = 1e-6

LANES = 128
SUBLANES = 8
TQ = 256
TK = 256
TQ_B = 512
SLAB = 128
PROJ_TN = 512
VMEM_LIMIT = 56 * 1024 * 1024

NEG_INIT = -1e30
NEG_MASK = -2e30
INT_MIN = -(2 ** 31)

PB_GATES = 0
PB_QB, PB_KB, PB_QC, PB_KC, PB_TAIL = 6, 7, 8, 9, 10
PB_QA, PB_KA = 11, 12
PB_COUNT = 13
TAIL_QI, TAIL_KI, TAIL_WI = 0, 256, 384


def _dot(a, b):
    return jnp.dot(a, b, preferred_element_type=F32)


def _dot_nt(a, b):
    return lax.dot_general(a, b, (((1,), (1,)), ((), ())), preferred_element_type=F32)


def _rms(x, g):
    ms = jnp.mean(x * x, axis=-1, keepdims=True)
    return (x * lax.rsqrt(ms + EPS)) * g


def _params(sem, vmem=None):
    return pltpu.CompilerParams(dimension_semantics=sem, vmem_limit_bytes=vmem)


def _row_groups(x):
    return x.reshape(x.shape[0] // SUBLANES, SUBLANES, x.shape[1])


def _head_halves(pair):
    lo = lax.broadcasted_iota(I32, pair.shape, 1) < HEAD_DIM
    zero = jnp.zeros_like(pair)
    return jnp.where(lo, pair, zero), jnp.where(lo, zero, pair)


def _in_proj_kernel(x_ref, g_ref, w_ref, cos_ref, sina_ref, sinb_ref, wva_ref, wvb_ref, wvc_ref,
                    o_ref, vat_ref, vbt_ref, vct_ref):
    xn = _rms(x_ref[...], g_ref[...]).astype(BF16)
    for wv_ref, vt_ref in ((wva_ref, vat_ref), (wvb_ref, vbt_ref), (wvc_ref, vct_ref)):
        vt = _dot_nt(wv_ref[...], xn).astype(BF16)
        w = vt_ref.shape[2]
        for c in range(vt_ref.shape[0]):
            vt_ref[c] = vt[:, c * w:(c + 1) * w]

    tn = o_ref.shape[2]
    half = HEAD_DIM // 2
    for j in range(PB_COUNT):
        a = _dot(xn, w_ref[:, j * tn:(j + 1) * tn])
        if PB_QB <= j <= PB_TAIL:
            for c in range(tn // LANES):
                sl = slice(c * LANES, (c + 1) * LANES)
                ac = a[:, sl]
                if j == PB_TAIL and c * LANES >= TAIL_WI:
                    o_ref[j, :, sl] = ac.astype(o_ref.dtype)
                else:
                    rot = (pltpu.roll(ac, LANES - half, 1) * sina_ref[...]
                           + pltpu.roll(ac, half, 1) * sinb_ref[...])
                    o_ref[j, :, sl] = (ac * cos_ref[...] + rot).astype(o_ref.dtype)
        elif j >= PB_QA:
            o_ref[j] = a.astype(o_ref.dtype)
        else:
            o_ref[j] = (1.0 / (1.0 + jnp.exp(-a))).astype(o_ref.dtype)


def _in_proj(x2, gain, w_main, cos_t, sina_t, sinb_t, w_vat, w_vbt, w_vct, seq):
    t, d = x2.shape
    tm = min(512, seq)
    tn = PROJ_TN
    pos_blocks = seq // tm
    wa, wb, wc = w_vat.shape[0], w_vbt.shape[0], w_vct.shape[0]
    resident = pl.Buffered(1)
    return pl.pallas_call(
        _in_proj_kernel,
        out_shape=(
            jax.ShapeDtypeStruct((PB_COUNT, t, tn), BF16),
            jax.ShapeDtypeStruct((t // TQ, wa, TQ), BF16),
            jax.ShapeDtypeStruct((t // TQ_B, wb, TQ_B), BF16),
            jax.ShapeDtypeStruct((t // TK, wc, TK), BF16),
        ),
        grid=(t // tm,),
        in_specs=[
            pl.BlockSpec((tm, d), lambda i: (i, 0)),
            pl.BlockSpec((1, d), lambda i: (0, 0)),
            pl.BlockSpec((d, PB_COUNT * tn), lambda i: (0, 0), pipeline_mode=resident),
            pl.BlockSpec((tm, LANES), lambda i: (i % pos_blocks, 0)),
            pl.BlockSpec((tm, LANES), lambda i: (i % pos_blocks, 0)),
            pl.BlockSpec((tm, LANES), lambda i: (i % pos_blocks, 0)),
            pl.BlockSpec((wa, d), lambda i: (0, 0), pipeline_mode=resident),
            pl.BlockSpec((wb, d), lambda i: (0, 0), pipeline_mode=resident),
            pl.BlockSpec((wc, d), lambda i: (0, 0), pipeline_mode=resident),
        ],
        out_specs=(
            pl.BlockSpec((PB_COUNT, tm, tn), lambda i: (0, i, 0)),
            pl.BlockSpec((tm // TQ, wa, TQ), lambda i: (i, 0, 0)),
            pl.BlockSpec((tm // TQ_B, wb, TQ_B), lambda i: (i, 0, 0)),
            pl.BlockSpec((tm // TK, wc, TK), lambda i: (i, 0, 0)),
        ),
        compiler_params=_params(("arbitrary",), VMEM_LIMIT),
        name="in_proj",
    )(x2, gain, w_main, cos_t, sina_t, sinb_t, w_vat, w_vbt, w_vct)


def _band_kernel(q_ref, k0_ref, k1_ref, k2_ref, vt0_ref, vt1_ref, vt2_ref, bias_ref, o_ref,
                 acc_ref, s_ref, p_ref):
    tq = q_ref.shape[0]
    ones = jnp.ones((2 * SUBLANES, 3 * tq), BF16)

    ms = []
    for h in range(H_A):
        sl = slice((h // 2) * LANES, (h // 2 + 1) * LANES)
        kw = jnp.concatenate([k0_ref[:, sl], k1_ref[:, sl], k2_ref[:, sl]], axis=0)
        s = _dot_nt(kw, _head_halves(q_ref[:, sl])[h % 2]) + bias_ref[h]
        s_ref[h] = s
        ms.append(jnp.max(jnp.max(_row_groups(s), axis=0), axis=0, keepdims=True))
    for h in range(H_A):
        for r in range(3 * tq // SLAB):
            p_ref[h, r * SLAB:(r + 1) * SLAB, :] = jnp.exp2(
                s_ref[h, r * SLAB:(r + 1) * SLAB, :] - ms[h]).astype(BF16)
    for h in range(H_A):
        rows = slice(h * HEAD_DIM, (h + 1) * HEAD_DIM)
        vt = jnp.concatenate([vt0_ref[0, rows, :], vt1_ref[0, rows, :], vt2_ref[0, rows, :]], axis=1)
        pv = _dot(jnp.concatenate([vt, ones], axis=0), p_ref[h])
        acc_ref[rows, :] = pv[:HEAD_DIM] / pv[HEAD_DIM:HEAD_DIM + 1]
    o_ref[...] = acc_ref[...].T.astype(o_ref.dtype)


def _band_bias(rel_bias, tq):
    back = LEFT_CHUNKS * CHUNK
    assert back == 2 * tq
    width = 4 * tq
    dist = back + tq - 1 - np.arange(width - 1)
    g = rel_bias[:, np.clip(dist, -(CHUNK - 1), REL_MAX) + (CHUNK - 1)].astype(F32)
    g = jnp.concatenate([g, jnp.zeros((g.shape[0], 1), F32)], axis=1) * math.log2(math.e)
    g = jnp.roll(g, -(tq - 1), axis=1)
    flat = jnp.tile(g, (1, tq))[:, :tq * (width - 1)]
    bias = flat.reshape(-1, tq, width - 1)[:, :, :3 * tq]
    qi = np.arange(tq)[:, None] + back
    kj = np.arange(3 * tq)[None, :]
    dc = qi // CHUNK - kj // CHUNK
    in_band = (dc >= 0) & (dc <= LEFT_CHUNKS)
    valid = np.stack([in_band & (kj >= (2 - v) * tq) for v in range(3)])
    bias = jnp.where(jnp.asarray(valid)[:, None], bias[None], NEG_MASK)
    return bias.transpose(0, 1, 3, 2)


def _band_attention(proj, vat, bias, b, s):
    tq = TQ
    nq = s // tq
    wa = H_A * HEAD_DIM

    def wspec(blk, off):
        return pl.BlockSpec((None, tq, wa),
                            lambda bb, i: (blk, bb * nq + jnp.maximum(i + off, 0), 0))

    def vspec(off):
        return pl.BlockSpec((1, wa, tq), lambda bb, i: (bb * nq + jnp.maximum(i + off, 0), 0, 0))

    return pl.pallas_call(
        _band_kernel,
        out_shape=jax.ShapeDtypeStruct((b * s, wa), BF16),
        grid=(b, nq),
        in_specs=[
            wspec(PB_QA, 0),
            wspec(PB_KA, -2), wspec(PB_KA, -1), wspec(PB_KA, 0),
            vspec(-2), vspec(-1), vspec(0),
            pl.BlockSpec((None, H_A, 3 * tq, tq), lambda bb, i: (jnp.minimum(i, 2), 0, 0, 0)),
        ],
        out_specs=pl.BlockSpec((tq, wa), lambda bb, i: (bb * nq + i, 0)),
        scratch_shapes=[
            pltpu.VMEM((wa, tq), F32),
            pltpu.VMEM((H_A, 3 * tq, tq), F32),
            pltpu.VMEM((H_A, 3 * tq, tq), BF16),
        ],
        compiler_params=_params(("arbitrary", "arbitrary"), VMEM_LIMIT),
        name="band_attn",
    )(proj, proj, proj, proj, vat, vat, vat, bias)


def _diff_kernel(q_ref, k_ref, vt_ref, lv_ref, sub_ref, o_ref, acc_ref, s_ref, p_ref, *,
                 lam_init):
    i = pl.program_id(1)
    tq = q_ref.shape[0]
    tk = vt_ref.shape[2]
    n_maps = 2 * H_B
    q = q_ref[...]
    qms = ()
    for h in range(H_B):
        qms += _head_halves(q[:, h * LANES:(h + 1) * LANES])
    acc_ref[...] = jnp.zeros(acc_ref.shape, F32)
    ck = lax.broadcasted_iota(I32, (tk, tq), 0) // CHUNK
    cq = lax.broadcasted_iota(I32, (tk, tq), 1) // CHUNK
    diag_ok = ck <= cq

    ones = jnp.ones((2 * SUBLANES, tk), BF16)

    def block(j, stats, mask):
        start = pl.multiple_of(j * tk, tk)
        mxs = []
        for c in range(n_maps):
            h = c // 2
            s = _dot_nt(k_ref[pl.ds(start, tk), h * LANES:(h + 1) * LANES], qms[c])
            if mask is not None:
                s = jnp.where(mask, s, NEG_MASK)
            s_ref[c] = s
            mxs.append(jnp.max(_row_groups(s), axis=0))
        ms, ls = stats
        new_m, new_l = [], []
        for c in range(n_maps):
            m_new = jnp.maximum(ms[c], jnp.max(mxs[c], axis=0, keepdims=True))
            for r in range(tk // SLAB):
                p = jnp.exp2(s_ref[c, r * SLAB:(r + 1) * SLAB, :] - m_new)
                p_ref[c, r * SLAB:(r + 1) * SLAB, :] = p.astype(BF16)
            alpha = jnp.exp2(ms[c] - m_new)
            new_m.append(m_new)
            vt = jnp.concatenate([vt_ref[j, (c // 2) * LANES:(c // 2 + 1) * LANES, :], ones], axis=0)
            pv = _dot(vt, p_ref[c])
            acc_ref[c] = alpha * acc_ref[c] + pv[:LANES]
            new_l.append(alpha * ls[c] + pv[LANES:LANES + 1])
        return tuple(new_m), tuple(new_l)

    m0 = jnp.full((1, tq), NEG_INIT, F32)
    l0 = jnp.zeros((1, tq), F32)
    stats = lax.fori_loop(0, i, lambda j, st: block(j, st, None),
                          ((m0,) * n_maps, (l0,) * n_maps))
    _, ls = block(i, stats, diag_ok)

    lv = lv_ref[...]
    lam = (jnp.exp(jnp.sum(lv[0:1] * lv[1:2], axis=-1, keepdims=True))
           - jnp.exp(jnp.sum(lv[2:3] * lv[3:4], axis=-1, keepdims=True)) + lam_init)
    for h in range(H_B):
        o = acc_ref[2 * h] / ls[2 * h] - lam * (acc_ref[2 * h + 1] / ls[2 * h + 1])
        ms = jnp.mean(o * o, axis=0, keepdims=True)
        o = (o * lax.rsqrt(ms + EPS)) * sub_ref[...] * (1.0 - lam_init)
        o_ref[:, h * LANES:(h + 1) * LANES] = o.T.astype(o_ref.dtype)


def _diff_attention(proj, vbt, lambda_vec, subln_col, lam_init, b, s):
    tq = TQ_B
    nkb = s // tq
    wb = H_B * 2 * HEAD_DIM
    kern = functools.partial(_diff_kernel, lam_init=lam_init)
    return pl.pallas_call(
        kern,
        out_shape=jax.ShapeDtypeStruct((b * s, wb), BF16),
        grid=(b, nkb),
        in_specs=[
            pl.BlockSpec((None, tq, wb), lambda bb, i: (PB_QB, bb * nkb + i, 0)),
            pl.BlockSpec((None, s, wb), lambda bb, i: (PB_KB, bb, 0)),
            pl.BlockSpec((nkb, wb, tq), lambda bb, i: (bb, 0, 0)),
            pl.BlockSpec((4, HEAD_DIM), lambda bb, i: (0, 0)),
            pl.BlockSpec((LANES, 1), lambda bb, i: (0, 0)),
        ],
        out_specs=pl.BlockSpec((tq, wb), lambda bb, i: (bb * nkb + i, 0)),
        scratch_shapes=[
            pltpu.VMEM((2 * H_B, LANES, tq), F32),
            pltpu.VMEM((2 * H_B, tq, tq), F32),
            pltpu.VMEM((2 * H_B, tq, tq), BF16),
        ],
        compiler_params=_params(("arbitrary", "arbitrary"), VMEM_LIMIT),
        name="diff_attn",
    )(proj, proj, vbt, lambda_vec, subln_col)


def _sparse_kernel(q_ref, k_ref, vt_ref, tail_ref, ki_ref, o_ref,
                   keys_ref, hi_ref, lo_ref, acc_ref, s_ref, p_ref, b_ref, *, topk):
    i = pl.program_id(1)
    tq = q_ref.shape[0]
    tk = vt_ref.shape[2]

    wi = tail_ref[:, TAIL_WI:TAIL_WI + LANES]
    wt = wi.astype(F32).T * (H_IDX ** -0.5)
    w_rows = [wt[h:h + 1, :] for h in range(H_IDX)]
    qi_heads = (_head_halves(tail_ref[:, TAIL_QI:TAIL_QI + LANES])
                + _head_halves(tail_ref[:, TAIL_QI + LANES:TAIL_QI + 2 * LANES]))
    n_pairs = (i + 2) // 2

    def score_pair(jj, masked):
        start = pl.multiple_of(jj * 2 * tk, 2 * tk)
        kk = ki_ref[pl.ds(start, 2 * tk), :]
        for h in range(H_IDX):
            s_ref[2 * h:2 * h + 2] = _dot_nt(kk, qi_heads[h]).reshape(2, tk, tq)
        sc = jnp.zeros((2 * tk, tq), F32)
        for h in range(H_IDX):
            logits = s_ref[2 * h:2 * h + 2].reshape(2 * tk, tq)
            sc = sc + jnp.maximum(logits, 0.0) * w_rows[h]
        bits = pltpu.bitcast(sc, I32)
        key = bits ^ ((bits >> 31) & 0x7FFFFFFF)
        key = jnp.where(sc == 0.0, 0, key)
        if masked:
            ck = (lax.broadcasted_iota(I32, (2 * tk, 1), 0) + start) // CHUNK
            cq = (lax.broadcasted_iota(I32, (1, tq), 1) + i * tq) // CHUNK
            key = jnp.where(ck <= cq, key, INT_MIN)
        for half in range(2):
            part = key[half * tk:(half + 1) * tk]
            keys_ref[2 * jj + half] = part
            hi_ref[2 * jj + half] = (part >> 16).astype(I16)

    def score_body(jj, carry):
        score_pair(jj, False)
        return carry

    lax.fori_loop(0, n_pairs - 1, score_body, 0)
    score_pair(n_pairs - 1, True)

    n_acc = 4
    grp = 2 * SUBLANES

    def count16(ref, pred_fn):
        def inner(j, accs):
            accs = list(accs)
            for g in range(tk // grp):
                rows = ref[j, g * grp:(g + 1) * grp, :]
                a = accs[g % n_acc]
                accs[g % n_acc] = jnp.where(pred_fn(rows), a + 1, a)
            return tuple(accs)
        accs = lax.fori_loop(0, i + 1, inner, (jnp.zeros((grp, tq), I16),) * n_acc)
        acc = (accs[0] + accs[1]) + (accs[2] + accs[3])
        return jnp.sum(acc.astype(F32), axis=0, keepdims=True)

    def digit16(x):
        return jnp.broadcast_to(x, (grp, tq)).astype(I16)

    def radix16(ref, want):
        def body(it, prefix):
            cand_u = prefix | lax.shift_left(jnp.int32(1), 15 - it)
            cand = digit16(cand_u - 32768)
            cnt = count16(ref, lambda rows: rows >= cand)
            return jnp.where(cnt >= want, cand_u, prefix)
        return lax.fori_loop(0, 16, body, jnp.zeros((1, tq), I32)) - 32768

    t_hi = radix16(hi_ref, topk)
    t_hi16 = digit16(t_hi)
    n_above = count16(hi_ref, lambda rows: rows > t_hi16)

    def low_body(j, carry):
        key = keys_ref[j]
        lo = ((key ^ 0x8000) << 16) >> 16
        lo_ref[j] = jnp.where((key >> 16) == t_hi, lo, -32768).astype(I16)
        return carry

    lax.fori_loop(0, i + 1, low_body, 0)
    t_lo = radix16(lo_ref, topk - n_above)
    thr = (t_hi << 16) | (t_lo + 32768)

    def count(pred_fn):
        def inner(j, accs):
            accs = list(accs)
            for g in range(tk // SUBLANES):
                rows = keys_ref[j, g * SUBLANES:(g + 1) * SUBLANES, :]
                a = accs[g % n_acc]
                accs[g % n_acc] = jnp.where(pred_fn(rows), a + 1.0, a)
            return tuple(accs)
        accs = lax.fori_loop(0, i + 1, inner, (jnp.zeros((SUBLANES, tq), F32),) * n_acc)
        acc = (accs[0] + accs[1]) + (accs[2] + accs[3])
        return jnp.sum(acc, axis=0, keepdims=True)

    n_gt = count(lambda blk: blk > thr)
    need = jnp.where(thr == INT_MIN, 0.0, topk - n_gt)

    acc_ref[...] = jnp.zeros(acc_ref.shape, F32)
    q = q_ref[...]
    q_heads = ()
    for hp in range(H_C // 2):
        q_heads += _head_halves(q[:, hp * LANES:(hp + 1) * LANES])
    earlier = (lax.broadcasted_iota(I32, (tk, tk), 0)
               > lax.broadcasted_iota(I32, (tk, tk), 1))
    earlier = jnp.where(earlier, 1.0, 0.0).astype(BF16)

    ones = jnp.ones((2 * SUBLANES, tk), BF16)

    def attn_body(j, carry):
        tie_seen, ms, ls = carry
        start = pl.multiple_of(j * tk, tk)
        kblk = keys_ref[j]
        tie = kblk == thr
        tie_f = jnp.where(tie, 1.0, 0.0)
        rank = _dot(earlier, tie_f.astype(BF16)) + tie_seen
        sel = (kblk > thr) | (tie & (rank < need))
        b_ref[...] = jnp.where(sel, 0.0, NEG_MASK)
        mxs = []
        for h in range(H_C):
            hp = h // 2
            kb = k_ref[pl.ds(start, tk), hp * LANES:(hp + 1) * LANES]
            s = _dot_nt(kb, q_heads[h]) + b_ref[...]
            s_ref[h] = s
            mxs.append(jnp.max(_row_groups(s), axis=0))
        new_m, new_l, alphas = [], [], []
        for h in range(H_C):
            m_new = jnp.maximum(ms[h], jnp.max(mxs[h], axis=0, keepdims=True))
            for r in range(tk // SLAB):
                p = jnp.exp2(s_ref[h, r * SLAB:(r + 1) * SLAB, :] - m_new)
                p_ref[h, r * SLAB:(r + 1) * SLAB, :] = p.astype(BF16)
            new_m.append(m_new)
            alphas.append(jnp.exp2(ms[h] - m_new))
        for h in range(H_C):
            rows = slice(h * HEAD_DIM, (h + 1) * HEAD_DIM)
            pv = _dot(jnp.concatenate([vt_ref[j, rows, :], ones], axis=0), p_ref[h])
            acc_ref[rows, :] = alphas[h] * acc_ref[rows, :] + pv[:HEAD_DIM]
            new_l.append(alphas[h] * ls[h] + pv[HEAD_DIM:HEAD_DIM + 1])
        return (tie_seen + jnp.sum(tie_f, axis=0, keepdims=True), tuple(new_m), tuple(new_l))

    m0 = jnp.full((1, tq), NEG_INIT, F32)
    l0 = jnp.zeros((1, tq), F32)
    _, _, ls = lax.fori_loop(0, i + 1, attn_body, (l0, (m0,) * H_C, (l0,) * H_C))

    for h in range(H_C):
        rows = slice(h * HEAD_DIM, (h + 1) * HEAD_DIM)
        acc_ref[rows, :] = acc_ref[rows, :] / ls[h]
    o_ref[...] = acc_ref[...].T.astype(o_ref.dtype)


def _sparse_attention(proj, vct, topk, b, s):
    tq = TQ
    wc = H_C * HEAD_DIM
    nkb = s // TK
    nq = s // tq
    kern = functools.partial(_sparse_kernel, topk=float(topk))
    return pl.pallas_call(
        kern,
        out_shape=jax.ShapeDtypeStruct((b * s, wc), BF16),
        grid=(b, nq),
        in_specs=[
            pl.BlockSpec((None, tq, wc), lambda bb, i: (PB_QC, bb * nq + i, 0)),
            pl.BlockSpec((None, s, wc), lambda bb, i: (PB_KC, bb, 0)),
            pl.BlockSpec((nkb, wc, TK), lambda bb, i: (bb, 0, 0)),
            pl.BlockSpec((None, tq, PROJ_TN), lambda bb, i: (PB_TAIL, bb * nq + i, 0)),
            pl.BlockSpec((None, s, LANES), lambda bb, i: (PB_TAIL, bb, TAIL_KI // LANES)),
        ],
        out_specs=pl.BlockSpec((tq, wc), lambda bb, i: (bb * nq + i, 0)),
        scratch_shapes=[
            pltpu.VMEM((nkb, TK, tq), I32),
            pltpu.VMEM((nkb, TK, tq), I16),
            pltpu.VMEM((nkb, TK, tq), I16),
            pltpu.VMEM((wc, tq), F32),
            pltpu.VMEM((H_C, TK, tq), F32),
            pltpu.VMEM((H_C, TK, tq), BF16),
            pltpu.VMEM((TK, tq), F32),
        ],
        compiler_params=_params(("arbitrary", "arbitrary"), VMEM_LIMIT),
        name="sparse_attn",
    )(proj, proj, vct, proj, proj)


def _mix_kernel(x_ref, oa_ref, ob_ref, oc_ref, ga_ref, gb_ref, gc_ref,
                ua_ref, ub_ref, uc_ref, wo_ref, o_ref):
    def gate(g_ref):
        return jnp.concatenate([g_ref[0], g_ref[1]], axis=1).astype(F32)

    y = gate(ga_ref) * _dot(oa_ref[...], ua_ref[...])
    y = y + gate(gb_ref) * _dot(ob_ref[...], ub_ref[...])
    y = y + gate(gc_ref) * _dot(oc_ref[...], uc_ref[...])
    o_ref[...] = x_ref[...] + _dot(y.astype(BF16), wo_ref[...])


def _mix_out(x2, oa, ob, oc, proj, ua, ub, uc, wo):
    t, d = x2.shape
    tm = min(1024, t)
    per_gate = d // PROJ_TN
    assert PB_GATES % per_gate == 0
    gate_blk = PB_GATES // per_gate
    w = oa.shape[1]

    def full(shape):
        return pl.BlockSpec(shape, lambda i: (0, 0), pipeline_mode=pl.Buffered(1))

    return pl.pallas_call(
        _mix_kernel,
        out_shape=jax.ShapeDtypeStruct((t, d), F32),
        grid=(t // tm,),
        in_specs=[
            pl.BlockSpec((tm, d), lambda i: (i, 0)),
            pl.BlockSpec((tm, w), lambda i: (i, 0)),
            pl.BlockSpec((tm, w), lambda i: (i, 0)),
            pl.BlockSpec((tm, w), lambda i: (i, 0)),
            pl.BlockSpec((per_gate, tm, PROJ_TN), lambda i: (gate_blk, i, 0)),
            pl.BlockSpec((per_gate, tm, PROJ_TN), lambda i: (gate_blk + 1, i, 0)),
            pl.BlockSpec((per_gate, tm, PROJ_TN), lambda i: (gate_blk + 2, i, 0)),
            full((w, d)), full((w, d)), full((w, d)), full((d, d)),
        ],
        out_specs=pl.BlockSpec((tm, d), lambda i: (i, 0)),
        compiler_params=_params(("arbitrary",), VMEM_LIMIT),
        name="mix_out",
    )(x2, oa, ob, oc, proj, proj, proj, ua, ub, uc, wo)


def _mem_kv_kernel(mem_ref, g_ref, wk_ref, wvt_ref, k_ref, vt_ref):
    mem_n = _rms(mem_ref[...], g_ref[...]).astype(BF16)
    k_ref[...] = _dot(mem_n, wk_ref[...]).astype(k_ref.dtype)
    vt = _dot_nt(wvt_ref[...], mem_n).astype(vt_ref.dtype)
    n_mem = vt_ref.shape[2]
    for c in range(vt_ref.shape[0]):
        vt_ref[c] = vt[:, c * n_mem:(c + 1) * n_mem]


def _mem_kv(mem2, mem_norm, w_k, w_vt, n_mem):
    r, d = mem2.shape
    depth, _, n = w_k.shape
    tm = min(1024, r)
    return pl.pallas_call(
        _mem_kv_kernel,
        out_shape=(jax.ShapeDtypeStruct((depth, r, n), BF16),
                   jax.ShapeDtypeStruct((depth, r // n_mem, n, n_mem), BF16)),
        grid=(depth, r // tm),
        in_specs=[
            pl.BlockSpec((tm, d), lambda l, i: (i, 0)),
            pl.BlockSpec((1, d), lambda l, i: (0, 0)),
            pl.BlockSpec((None, d, n), lambda l, i: (l, 0, 0)),
            pl.BlockSpec((None, n, d), lambda l, i: (l, 0, 0)),
        ],
        out_specs=(pl.BlockSpec((None, tm, n), lambda l, i: (l, i, 0)),
                   pl.BlockSpec((None, tm // n_mem, n, n_mem), lambda l, i: (l, i, 0, 0))),
        compiler_params=_params(("arbitrary", "arbitrary"), VMEM_LIMIT),
        name="mem_kv",
    )(mem2, mem_norm, w_k, w_vt)


def _cross_kernel(x_ref, g_ref, wq_ref, k_ref, vt_ref, wo_ref, o_ref, acc_ref, s_ref, p_ref):
    n_mem = k_ref.shape[0]
    sub = acc_ref.shape[2]
    ones = jnp.ones((2 * SUBLANES, n_mem), BF16)
    for u in range(acc_ref.shape[0]):
        x = x_ref[u * sub:(u + 1) * sub, :]
        hn = _rms(x, g_ref[...]).astype(BF16)
        q = _dot(hn, wq_ref[...]).astype(BF16)
        ms = []
        for h in range(H_X):
            sl = slice((h // 2) * LANES, (h // 2 + 1) * LANES)
            s = _dot_nt(k_ref[:, sl], _head_halves(q[:, sl])[h % 2])
            s_ref[u, h] = s
            ms.append(jnp.max(s, axis=0, keepdims=True))
        for h in range(H_X):
            p_ref[u, h] = jnp.exp2(s_ref[u, h] - ms[h]).astype(BF16)
        for h in range(H_X):
            rows = slice(h * HEAD_DIM, (h + 1) * HEAD_DIM)
            pv = _dot(jnp.concatenate([vt_ref[rows, :], ones], axis=0), p_ref[u, h])
            acc_ref[u, rows, :] = pv[:HEAD_DIM] / pv[HEAD_DIM:HEAD_DIM + 1]
        o_ref[u * sub:(u + 1) * sub, :] = x + _dot(acc_ref[u].T.astype(BF16), wo_ref[...])


def _cross_attention(x2, gain, wq, k_mem, vt_mem, wo, seq):
    t, d = x2.shape
    sub = min(512, seq)
    tm = min(2 * sub, seq)
    per_batch = seq // tm
    n_mem, wk = k_mem.shape[0] // vt_mem.shape[0], k_mem.shape[1]
    return pl.pallas_call(
        _cross_kernel,
        out_shape=jax.ShapeDtypeStruct((t, d), F32),
        grid=(t // tm,),
        in_specs=[
            pl.BlockSpec((tm, d), lambda i: (i, 0)),
            pl.BlockSpec((1, d), lambda i: (0, 0)),
            pl.BlockSpec((d, wk), lambda i: (0, 0)),
            pl.BlockSpec((n_mem, wk), lambda i: (i // per_batch, 0)),
            pl.BlockSpec((None, wk, n_mem), lambda i: (i // per_batch, 0, 0)),
            pl.BlockSpec((wk, d), lambda i: (0, 0)),
        ],
        out_specs=pl.BlockSpec((tm, d), lambda i: (i, 0)),
        scratch_shapes=[
            pltpu.VMEM((tm // sub, wk, sub), F32),
            pltpu.VMEM((tm // sub, H_X, n_mem, sub), F32),
            pltpu.VMEM((tm // sub, H_X, n_mem, sub), BF16),
        ],
        compiler_params=_params(("arbitrary",), VMEM_LIMIT),
        name="cross_attn",
    )(x2, gain, wq, k_mem, vt_mem, wo)


def _ffn_kernel(x_ref, g_ref, wgu_ref, wd_ref, fg_ref, o_ref, acc_ref, *, final_norm, tf):
    x = x_ref[...]
    hn = _rms(x, g_ref[...]).astype(BF16)
    d_ff = wd_ref.shape[0]
    for c in range(d_ff // tf):
        gate = _dot(hn, wgu_ref[:, c * tf:(c + 1) * tf])
        up = _dot(hn, wgu_ref[:, d_ff + c * tf:d_ff + (c + 1) * tf])
        h = (gate / (1.0 + jnp.exp(-gate))) * up
        part = _dot(h.astype(BF16), wd_ref[c * tf:(c + 1) * tf, :])
        if c == 0:
            acc_ref[...] = part
        else:
            acc_ref[...] += part
    y = x + acc_ref[...]
    if final_norm:
        y = _rms(y, fg_ref[...])
    o_ref[...] = y


def _ffn(x2, gain, w_gu, w_down, final_gain, final_norm):
    t, d = x2.shape
    d_ff = w_down.shape[0]
    tm = min(512, t)
    tf = 2 * LANES
    assert d_ff % tf == 0
    kern = functools.partial(_ffn_kernel, final_norm=final_norm, tf=tf)
    resident = pl.Buffered(1)
    return pl.pallas_call(
        kern,
        out_shape=jax.ShapeDtypeStruct((t, d), F32),
        grid=(t // tm,),
        in_specs=[
            pl.BlockSpec((tm, d), lambda i: (i, 0)),
            pl.BlockSpec((1, d), lambda i: (0, 0)),
            pl.BlockSpec((d, 2 * d_ff), lambda i: (0, 0), pipeline_mode=resident),
            pl.BlockSpec((d_ff, d), lambda i: (0, 0), pipeline_mode=resident),
            pl.BlockSpec((1, d), lambda i: (0, 0)),
        ],
        out_specs=pl.BlockSpec((tm, d), lambda i: (i, 0)),
        scratch_shapes=[pltpu.VMEM((tm, d), F32)],
        compiler_params=_params(("arbitrary",), VMEM_LIMIT),
        name="ffn",
    )(x2, gain, w_gu, w_down, final_gain)


def _layout_in_weights(w_in):
    depth, d, _ = w_in.shape
    w_a, w_b, w_c = H_A * HEAD_DIM, H_B * 2 * HEAD_DIM, H_C * HEAD_DIM
    splits = (w_a, w_a, w_a, w_b, w_b, w_b, w_c, w_c, w_c, H_IDX * D_IDX, D_IDX, H_IDX, 3 * d)
    offs = [0] + [int(o) for o in np.cumsum(splits)]
    scale = HEAD_DIM ** -0.5
    col_scale = np.ones((offs[-1],), np.float32)
    for part in (0, 3, 6):
        col_scale[offs[part]:offs[part + 1]] = scale * math.log2(math.e)
    col_scale[offs[9]:offs[10]] = D_IDX ** -0.5
    w = (w_in * col_scale).astype(BF16)
    qa, ka, va, qb, kb, vb, qc, kc, vc, qi, ki, wi, gates = (
        w[:, :, offs[n]:offs[n + 1]] for n in range(len(splits)))
    pad = jnp.zeros((depth, d, LANES - H_IDX), BF16)
    w_main = jnp.concatenate([gates, qb, kb, qc, kc, qi, ki, ki, wi, pad, qa, ka], axis=2)
    assert w_main.shape[2] == PB_COUNT * PROJ_TN
    return w_main, va.swapaxes(1, 2), vb.swapaxes(1, 2), vc.swapaxes(1, 2)


def _rope_tables(seq):
    pos = jnp.arange(seq, dtype=F32)
    inv = ROPE_THETA ** (-jnp.arange(0, HEAD_DIM, 2, dtype=F32) / HEAD_DIM)
    ang = pos[:, None] * inv[None, :]
    cos, sin = jnp.cos(ang), jnp.sin(ang)
    zero = jnp.zeros_like(sin)
    reps = LANES // HEAD_DIM
    cos_t = jnp.tile(jnp.concatenate([cos, cos], axis=1), (1, reps))
    sina_t = jnp.tile(jnp.concatenate([-sin, zero], axis=1), (1, reps))
    sinb_t = jnp.tile(jnp.concatenate([zero, sin], axis=1), (1, reps))
    return cos_t, sina_t, sinb_t


def kernel(x, mem, norm_mix, w_in, rel_bias_a, lambda_vecs, subln_b, w_up_a, w_up_b, w_up_c,
           w_out, norm_cross, w_q_x, w_kv_x, w_o_x, norm_ffn, w_gu, w_down, mem_norm, final_norm):
    b, s, d = x.shape
    depth = w_in.shape[0]
    assert s % TQ_B == 0 and LEFT_CHUNKS * CHUNK == 2 * TQ and TQ == TK
    topk = min(TOPK_MAX, s // 4)
    cos_t, sina_t, sinb_t = _rope_tables(s)
    x2 = x.reshape(b * s, d)
    mem2 = mem.reshape(b * mem.shape[1], d)
    wkv = H_X * HEAD_DIM
    k_mem, vt_mem = _mem_kv(mem2, mem_norm.reshape(1, d), w_kv_x[:, :, :wkv].astype(BF16),
                            w_kv_x[:, :, wkv:].swapaxes(1, 2).astype(BF16), mem.shape[1])
    w_main, w_vat, w_vbt, w_vct = _layout_in_weights(w_in)
    ua, ub, uc, wo = (w.astype(BF16) for w in (w_up_a, w_up_b, w_up_c, w_out))
    wq_x = (w_q_x * (HEAD_DIM ** -0.5 * math.log2(math.e))).astype(BF16)
    wo_x = w_o_x.astype(BF16)
    wgu, wdn = w_gu.astype(BF16), w_down.astype(BF16)
    for l in range(depth):
        proj, vat, vbt, vct = _in_proj(x2, norm_mix[l].reshape(1, d), w_main[l], cos_t, sina_t,
                                       sinb_t, w_vat[l], w_vbt[l], w_vct[l], s)
        lam_init = 0.8 - 0.6 * math.exp(-0.3 * l)
        o_a = _band_attention(proj, vat, _band_bias(rel_bias_a[l], TQ), b, s)
        o_b = _diff_attention(proj, vbt, lambda_vecs[l].astype(F32),
                              subln_b[l].reshape(LANES, 1), lam_init, b, s)
        o_c = _sparse_attention(proj, vct, topk, b, s)
        x2 = _mix_out(x2, o_a, o_b, o_c, proj, ua[l], ub[l], uc[l], wo[l])
        x2 = _cross_attention(x2, norm_cross[l].reshape(1, d), wq_x[l], k_mem[l], vt_mem[l],
                              wo_x[l], s)
        x2 = _ffn(x2, norm_ffn[l].reshape(1, d), wgu[l], wdn[l],
                  final_norm.reshape(1, d), final_norm=(l == depth - 1))
    return x2.reshape(b, s, d)
```

```python
import functools
import math

import jax
import jax.numpy as jnp
import numpy as np
from jax import lax
from jax.experimental import pallas as pl
from jax.experimental.pallas import tpu as pltpu

F32 = jnp.float32
BF16 = jnp.bfloat16
I32 = jnp.int32
I16 = jnp.int16

CHUNK = 64
HEAD_DIM = 64
H_A = 8
LEFT_CHUNKS = 8
REL_MAX = 256
H_B = 4
H_C = 8
H_IDX = 4
D_IDX = 64
TOPK_MAX = 256
H_X = 4
ROPE_THETA = 10000.0
EPS = 1e-6

LANES = 128
SUBLANES = 8
TQ = 256
TK = 256
TQ_B = 512
SLAB = 128
PROJ_TN = 512
VMEM_LIMIT = 56 * 1024 * 1024

NEG_INIT = -1e30
NEG_MASK = -2e30
INT_MIN = -(2 ** 31)

PB_GATES = 0
PB_QB, PB_KB, PB_QC, PB_KC, PB_TAIL = 6, 7, 8, 9, 10
PB_QA, PB_KA = 11, 12
PB_COUNT = 13
TAIL_QI, TAIL_KI, TAIL_WI = 0, 256, 384


def _dot(a, b):
    return jnp.dot(a, b, preferred_element_type=F32)


def _dot_nt(a, b):
    return lax.dot_general(a, b, (((1,), (1,)), ((), ())), preferred_element_type=F32)


def _rms(x, g):
    ms = jnp.mean(x * x, axis=-1, keepdims=True)
    return (x * lax.rsqrt(ms + EPS)) * g


def _params(sem, vmem=None):
    return pltpu.CompilerParams(dimension_semantics=sem, vmem_limit_bytes=vmem)


def _row_groups(x):
    return x.reshape(x.shape[0] // SUBLANES, SUBLANES, x.shape[1])


def _head_halves(pair):
    lo = lax.broadcasted_iota(I32, pair.shape, 1) < HEAD_DIM
    zero = jnp.zeros_like(pair)
    return jnp.where(lo, pair, zero), jnp.where(lo, zero, pair)


def _in_proj_kernel(x_ref, g_ref, w_ref, cos_ref, sina_ref, sinb_ref, wva_ref, wvb_ref, wvc_ref,
                    o_ref, vat_ref, vbt_ref, vct_ref):
    xn = _rms(x_ref[...], g_ref[...]).astype(BF16)
    for wv_ref, vt_ref in ((wva_ref, vat_ref), (wvb_ref, vbt_ref), (wvc_ref, vct_ref)):
        vt = _dot_nt(wv_ref[...], xn).astype(BF16)
        w = vt_ref.shape[2]
        for c in range(vt_ref.shape[0]):
            vt_ref[c] = vt[:, c * w:(c + 1) * w]

    tn = o_ref.shape[2]
    half = HEAD_DIM // 2
    for j in range(PB_COUNT):
        a = _dot(xn, w_ref[:, j * tn:(j + 1) * tn])
        if PB_QB <= j <= PB_TAIL:
            for c in range(tn // LANES):
                sl = slice(c * LANES, (c + 1) * LANES)
                ac = a[:, sl]
                if j == PB_TAIL and c * LANES >= TAIL_WI:
                    o_ref[j, :, sl] = ac.astype(o_ref.dtype)
                else:
                    rot = (pltpu.roll(ac, LANES - half, 1) * sina_ref[...]
                           + pltpu.roll(ac, half, 1) * sinb_ref[...])
                    o_ref[j, :, sl] = (ac * cos_ref[...] + rot).astype(o_ref.dtype)
        elif j >= PB_QA:
            o_ref[j] = a.astype(o_ref.dtype)
        else:
            o_ref[j] = (1.0 / (1.0 + jnp.exp(-a))).astype(o_ref.dtype)


def _in_proj(x2, gain, w_main, cos_t, sina_t, sinb_t, w_vat, w_vbt, w_vct, seq):
    t, d = x2.shape
    tm = min(512, seq)
    tn = PROJ_TN
    pos_blocks = seq // tm
    wa, wb, wc = w_vat.shape[0], w_vbt.shape[0], w_vct.shape[0]
    resident = pl.Buffered(1)
    return pl.pallas_call(
        _in_proj_kernel,
        out_shape=(
            jax.ShapeDtypeStruct((PB_COUNT, t, tn), BF16),
            jax.ShapeDtypeStruct((t // TQ, wa, TQ), BF16),
            jax.ShapeDtypeStruct((t // TQ_B, wb, TQ_B), BF16),
            jax.ShapeDtypeStruct((t // TK, wc, TK), BF16),
        ),
        grid=(t // tm,),
        in_specs=[
            pl.BlockSpec((tm, d), lambda i: (i, 0)),
            pl.BlockSpec((1, d), lambda i: (0, 0)),
            pl.BlockSpec((d, PB_COUNT * tn), lambda i: (0, 0), pipeline_mode=resident),
            pl.BlockSpec((tm, LANES), lambda i: (i % pos_blocks, 0)),
            pl.BlockSpec((tm, LANES), lambda i: (i % pos_blocks, 0)),
            pl.BlockSpec((tm, LANES), lambda i: (i % pos_blocks, 0)),
            pl.BlockSpec((wa, d), lambda i: (0, 0), pipeline_mode=resident),
            pl.BlockSpec((wb, d), lambda i: (0, 0), pipeline_mode=resident),
            pl.BlockSpec((wc, d), lambda i: (0, 0), pipeline_mode=resident),
        ],
        out_specs=(
            pl.BlockSpec((PB_COUNT, tm, tn), lambda i: (0, i, 0)),
            pl.BlockSpec((tm // TQ, wa, TQ), lambda i: (i, 0, 0)),
            pl.BlockSpec((tm // TQ_B, wb, TQ_B), lambda i: (i, 0, 0)),
            pl.BlockSpec((tm // TK, wc, TK), lambda i: (i, 0, 0)),
        ),
        compiler_params=_params(("arbitrary",), VMEM_LIMIT),
        name="in_proj",
    )(x2, gain, w_main, cos_t, sina_t, sinb_t, w_vat, w_vbt, w_vct)


def _band_kernel(q_ref, k0_ref, k1_ref, k2_ref, vt0_ref, vt1_ref, vt2_ref, bias_ref, o_ref,
                 acc_ref, s_ref, p_ref):
    tq = q_ref.shape[0]
    ones = jnp.ones((2 * SUBLANES, 3 * tq), BF16)

    ms = []
    for h in range(H_A):
        sl = slice((h // 2) * LANES, (h // 2 + 1) * LANES)
        kw = jnp.concatenate([k0_ref[:, sl], k1_ref[:, sl], k2_ref[:, sl]], axis=0)
        s = _dot_nt(kw, _head_halves(q_ref[:, sl])[h % 2]) + bias_ref[h]
        s_ref[h] = s
        ms.append(jnp.max(jnp.max(_row_groups(s), axis=0), axis=0, keepdims=True))
    for h in range(H_A):
        for r in range(3 * tq // SLAB):
            p_ref[h, r * SLAB:(r + 1) * SLAB, :] = jnp.exp2(
                s_ref[h, r * SLAB:(r + 1) * SLAB, :] - ms[h]).astype(BF16)
    for h in range(H_A):
        rows = slice(h * HEAD_DIM, (h + 1) * HEAD_DIM)
        vt = jnp.concatenate([vt0_ref[0, rows, :], vt1_ref[0, rows, :], vt2_ref[0, rows, :]], axis=1)
        pv = _dot(jnp.concatenate([vt, ones], axis=0), p_ref[h])
        acc_ref[rows, :] = pv[:HEAD_DIM] / pv[HEAD_DIM:HEAD_DIM + 1]
    o_ref[...] = acc_ref[...].T.astype(o_ref.dtype)


def _band_bias(rel_bias, tq):
    back = LEFT_CHUNKS * CHUNK
    assert back == 2 * tq
    width = 4 * tq
    dist = back + tq - 1 - np.arange(width - 1)
    g = rel_bias[:, np.clip(dist, -(CHUNK - 1), REL_MAX) + (CHUNK - 1)].astype(F32)
    g = jnp.concatenate([g, jnp.zeros((g.shape[0], 1), F32)], axis=1) * math.log2(math.e)
    g = jnp.roll(g, -(tq - 1), axis=1)
    flat = jnp.tile(g, (1, tq))[:, :tq * (width - 1)]
    bias = flat.reshape(-1, tq, width - 1)[:, :, :3 * tq]
    qi = np.arange(tq)[:, None] + back
    kj = np.arange(3 * tq)[None, :]
    dc = qi // CHUNK - kj // CHUNK
    in_band = (dc >= 0) & (dc <= LEFT_CHUNKS)
    valid = np.stack([in_band & (kj >= (2 - v) * tq) for v in range(3)])
    bias = jnp.where(jnp.asarray(valid)[:, None], bias[None], NEG_MASK)
    return bias.transpose(0, 1, 3, 2)


def _band_attention(proj, vat, bias, b, s):
    tq = TQ
    nq = s // tq
    wa = H_A * HEAD_DIM

    def wspec(blk, off):
        return pl.BlockSpec((None, tq, wa),
                            lambda bb, i: (blk, bb * nq + jnp.maximum(i + off, 0), 0))

    def vspec(off):
        return pl.BlockSpec((1, wa, tq), lambda bb, i: (bb * nq + jnp.maximum(i + off, 0), 0, 0))

    return pl.pallas_call(
        _band_kernel,
        out_shape=jax.ShapeDtypeStruct((b * s, wa), BF16),
        grid=(b, nq),
        in_specs=[
            wspec(PB_QA, 0),
            wspec(PB_KA, -2), wspec(PB_KA, -1), wspec(PB_KA, 0),
            vspec(-2), vspec(-1), vspec(0),
            pl.BlockSpec((None, H_A, 3 * tq, tq), lambda bb, i: (jnp.minimum(i, 2), 0, 0, 0)),
        ],
        out_specs=pl.BlockSpec((tq, wa), lambda bb, i: (bb * nq + i, 0)),
        scratch_shapes=[
            pltpu.VMEM((wa, tq), F32),
            pltpu.VMEM((H_A, 3 * tq, tq), F32),
            pltpu.VMEM((H_A, 3 * tq, tq), BF16),
        ],
        compiler_params=_params(("arbitrary", "arbitrary"), VMEM_LIMIT),
        name="band_attn",
    )(proj, proj, proj, proj, vat, vat, vat, bias)


def _diff_kernel(q_ref, k_ref, vt_ref, lv_ref, sub_ref, o_ref, acc_ref, s_ref, p_ref, *,
                 lam_init):
    i = pl.program_id(1)
    tq = q_ref.shape[0]
    tk = vt_ref.shape[2]
    n_maps = 2 * H_B
    q = q_ref[...]
    qms = ()
    for h in range(H_B):
        qms += _head_halves(q[:, h * LANES:(h + 1) * LANES])
    acc_ref[...] = jnp.zeros(acc_ref.shape, F32)
    ck = lax.broadcasted_iota(I32, (tk, tq), 0) // CHUNK
    cq = lax.broadcasted_iota(I32, (tk, tq), 1) // CHUNK
    diag_ok = ck <= cq

    ones = jnp.ones((2 * SUBLANES, tk), BF16)

    def block(j, stats, mask):
        start = pl.multiple_of(j * tk, tk)
        mxs = []
        for c in range(n_maps):
            h = c // 2
            s = _dot_nt(k_ref[pl.ds(start, tk), h * LANES:(h + 1) * LANES], qms[c])
            if mask is not None:
                s = jnp.where(mask, s, NEG_MASK)
            s_ref[c] = s
            mxs.append(jnp.max(_row_groups(s), axis=0))
        ms, ls = stats
        new_m, new_l = [], []
        for c in range(n_maps):
            m_new = jnp.maximum(ms[c], jnp.max(mxs[c], axis=0, keepdims=True))
            for r in range(tk // SLAB):
                p = jnp.exp2(s_ref[c, r * SLAB:(r + 1) * SLAB, :] - m_new)
                p_ref[c, r * SLAB:(r + 1) * SLAB, :] = p.astype(BF16)
            alpha = jnp.exp2(ms[c] - m_new)
            new_m.append(m_new)
            vt = jnp.concatenate([vt_ref[j, (c // 2) * LANES:(c // 2 + 1) * LANES, :], ones], axis=0)
            pv = _dot(vt, p_ref[c])
            acc_ref[c] = alpha * acc_ref[c] + pv[:LANES]
            new_l.append(alpha * ls[c] + pv[LANES:LANES + 1])
        return tuple(new_m), tuple(new_l)

    m0 = jnp.full((1, tq), NEG_INIT, F32)
    l0 = jnp.zeros((1, tq), F32)
    stats = lax.fori_loop(0, i, lambda j, st: block(j, st, None),
                          ((m0,) * n_maps, (l0,) * n_maps))
    _, ls = block(i, stats, diag_ok)

    lv = lv_ref[...]
    lam = (jnp.exp(jnp.sum(lv[0:1] * lv[1:2], axis=-1, keepdims=True))
           - jnp.exp(jnp.sum(lv[2:3] * lv[3:4], axis=-1, keepdims=True)) + lam_init)
    for h in range(H_B):
        o = acc_ref[2 * h] / ls[2 * h] - lam * (acc_ref[2 * h + 1] / ls[2 * h + 1])
        ms = jnp.mean(o * o, axis=0, keepdims=True)
        o = (o * lax.rsqrt(ms + EPS)) * sub_ref[...] * (1.0 - lam_init)
        o_ref[:, h * LANES:(h + 1) * LANES] = o.T.astype(o_ref.dtype)


def _diff_attention(proj, vbt, lambda_vec, subln_col, lam_init, b, s):
    tq = TQ_B
    nkb = s // tq
    wb = H_B * 2 * HEAD_DIM
    kern = functools.partial(_diff_kernel, lam_init=lam_init)
    return pl.pallas_call(
        kern,
        out_shape=jax.ShapeDtypeStruct((b * s, wb), BF16),
        grid=(b, nkb),
        in_specs=[
            pl.BlockSpec((None, tq, wb), lambda bb, i: (PB_QB, bb * nkb + i, 0)),
            pl.BlockSpec((None, s, wb), lambda bb, i: (PB_KB, bb, 0)),
            pl.BlockSpec((nkb, wb, tq), lambda bb, i: (bb, 0, 0)),
            pl.BlockSpec((4, HEAD_DIM), lambda bb, i: (0, 0)),
            pl.BlockSpec((LANES, 1), lambda bb, i: (0, 0)),
        ],
        out_specs=pl.BlockSpec((tq, wb), lambda bb, i: (bb * nkb + i, 0)),
        scratch_shapes=[
            pltpu.VMEM((2 * H_B, LANES, tq), F32),
            pltpu.VMEM((2 * H_B, tq, tq), F32),
            pltpu.VMEM((2 * H_B, tq, tq), BF16),
        ],
        compiler_params=_params(("arbitrary", "arbitrary"), VMEM_LIMIT),
        name="diff_attn",
    )(proj, proj, vbt, lambda_vec, subln_col)


def _sparse_kernel(q_ref, k_ref, vt_ref, tail_ref, ki_ref, o_ref,
                   keys_ref, hi_ref, lo_ref, acc_ref, s_ref, p_ref, b_ref, *, topk):
    i = pl.program_id(1)
    tq = q_ref.shape[0]
    tk = vt_ref.shape[2]

    wi = tail_ref[:, TAIL_WI:TAIL_WI + LANES]
    wt = wi.astype(F32).T * (H_IDX ** -0.5)
    w_rows = [wt[h:h + 1, :] for h in range(H_IDX)]
    qi_heads = (_head_halves(tail_ref[:, TAIL_QI:TAIL_QI + LANES])
                + _head_halves(tail_ref[:, TAIL_QI + LANES:TAIL_QI + 2 * LANES]))
    n_pairs = (i + 2) // 2

    def score_pair(jj, masked):
        start = pl.multiple_of(jj * 2 * tk, 2 * tk)
        kk = ki_ref[pl.ds(start, 2 * tk), :]
        for h in range(H_IDX):
            s_ref[2 * h:2 * h + 2] = _dot_nt(kk, qi_heads[h]).reshape(2, tk, tq)
        sc = jnp.zeros((2 * tk, tq), F32)
        for h in range(H_IDX):
            logits = s_ref[2 * h:2 * h + 2].reshape(2 * tk, tq)
            sc = sc + jnp.maximum(logits, 0.0) * w_rows[h]
        bits = pltpu.bitcast(sc, I32)
        key = bits ^ ((bits >> 31) & 0x7FFFFFFF)
        key = jnp.where(sc == 0.0, 0, key)
        if masked:
            ck = (lax.broadcasted_iota(I32, (2 * tk, 1), 0) + start) // CHUNK
            cq = (lax.broadcasted_iota(I32, (1, tq), 1) + i * tq) // CHUNK
            key = jnp.where(ck <= cq, key, INT_MIN)
        for half in range(2):
            part = key[half * tk:(half + 1) * tk]
            keys_ref[2 * jj + half] = part
            hi_ref[2 * jj + half] = (part >> 16).astype(I16)

    def score_body(jj, carry):
        score_pair(jj, False)
        return carry

    lax.fori_loop(0, n_pairs - 1, score_body, 0)
    score_pair(n_pairs - 1, True)

    n_acc = 4
    grp = 2 * SUBLANES

    def count16(ref, pred_fn):
        def inner(j, accs):
            accs = list(accs)
            for g in range(tk // grp):
                rows = ref[j, g * grp:(g + 1) * grp, :]
                a = accs[g % n_acc]
                accs[g % n_acc] = jnp.where(pred_fn(rows), a + 1, a)
            return tuple(accs)
        accs = lax.fori_loop(0, i + 1, inner, (jnp.zeros((grp, tq), I16),) * n_acc)
        acc = (accs[0] + accs[1]) + (accs[2] + accs[3])
        return jnp.sum(acc.astype(F32), axis=0, keepdims=True)

    def digit16(x):
        return jnp.broadcast_to(x, (grp, tq)).astype(I16)

    def radix16(ref, want):
        def body(it, prefix):
            cand_u = prefix | lax.shift_left(jnp.int32(1), 15 - it)
            cand = digit16(cand_u - 32768)
            cnt = count16(ref, lambda rows: rows >= cand)
            return jnp.where(cnt >= want, cand_u, prefix)
        return lax.fori_loop(0, 16, body, jnp.zeros((1, tq), I32)) - 32768

    t_hi = radix16(hi_ref, topk)
    t_hi16 = digit16(t_hi)
    n_above = count16(hi_ref, lambda rows: rows > t_hi16)

    def low_body(j, carry):
        key = keys_ref[j]
        lo = ((key ^ 0x8000) << 16) >> 16
        lo_ref[j] = jnp.where((key >> 16) == t_hi, lo, -32768).astype(I16)
        return carry

    lax.fori_loop(0, i + 1, low_body, 0)
    t_lo = radix16(lo_ref, topk - n_above)
    thr = (t_hi << 16) | (t_lo + 32768)

    def count(pred_fn):
        def inner(j, accs):
            accs = list(accs)
            for g in range(tk // SUBLANES):
                rows = keys_ref[j, g * SUBLANES:(g + 1) * SUBLANES, :]
                a = accs[g % n_acc]
                accs[g % n_acc] = jnp.where(pred_fn(rows), a + 1.0, a)
            return tuple(accs)
        accs = lax.fori_loop(0, i + 1, inner, (jnp.zeros((SUBLANES, tq), F32),) * n_acc)
        acc = (accs[0] + accs[1]) + (accs[2] + accs[3])
        return jnp.sum(acc, axis=0, keepdims=True)

    n_gt = count(lambda blk: blk > thr)
    need = jnp.where(thr == INT_MIN, 0.0, topk - n_gt)

    acc_ref[...] = jnp.zeros(acc_ref.shape, F32)
    q = q_ref[...]
    q_heads = ()
    for hp in range(H_C // 2):
        q_heads += _head_halves(q[:, hp * LANES:(hp + 1) * LANES])
    earlier = (lax.broadcasted_iota(I32, (tk, tk), 0)
               > lax.broadcasted_iota(I32, (tk, tk), 1))
    earlier = jnp.where(earlier, 1.0, 0.0).astype(BF16)

    ones = jnp.ones((2 * SUBLANES, tk), BF16)

    def attn_body(j, carry):
        tie_seen, ms, ls = carry
        start = pl.multiple_of(j * tk, tk)
        kblk = keys_ref[j]
        tie = kblk == thr
        tie_f = jnp.where(tie, 1.0, 0.0)
        rank = _dot(earlier, tie_f.astype(BF16)) + tie_seen
        sel = (kblk > thr) | (tie & (rank < need))
        b_ref[...] = jnp.where(sel, 0.0, NEG_MASK)
        mxs = []
        for h in range(H_C):
            hp = h // 2
            kb = k_ref[pl.ds(start, tk), hp * LANES:(hp + 1) * LANES]
            s = _dot_nt(kb, q_heads[h]) + b_ref[...]
            s_ref[h] = s
            mxs.append(jnp.max(_row_groups(s), axis=0))
        new_m, new_l, alphas = [], [], []
        for h in range(H_C):
            m_new = jnp.maximum(ms[h], jnp.max(mxs[h], axis=0, keepdims=True))
            for r in range(tk // SLAB):
                p = jnp.exp2(s_ref[h, r * SLAB:(r + 1) * SLAB, :] - m_new)
                p_ref[h, r * SLAB:(r + 1) * SLAB, :] = p.astype(BF16)
            new_m.append(m_new)
            alphas.append(jnp.exp2(ms[h] - m_new))
        for h in range(H_C):
            rows = slice(h * HEAD_DIM, (h + 1) * HEAD_DIM)
            pv = _dot(jnp.concatenate([vt_ref[j, rows, :], ones], axis=0), p_ref[h])
            acc_ref[rows, :] = alphas[h] * acc_ref[rows, :] + pv[:HEAD_DIM]
            new_l.append(alphas[h] * ls[h] + pv[HEAD_DIM:HEAD_DIM + 1])
        return (tie_seen + jnp.sum(tie_f, axis=0, keepdims=True), tuple(new_m), tuple(new_l))

    m0 = jnp.full((1, tq), NEG_INIT, F32)
    l0 = jnp.zeros((1, tq), F32)
    _, _, ls = lax.fori_loop(0, i + 1, attn_body, (l0, (m0,) * H_C, (l0,) * H_C))

    for h in range(H_C):
        rows = slice(h * HEAD_DIM, (h + 1) * HEAD_DIM)
        acc_ref[rows, :] = acc_ref[rows, :] / ls[h]
    o_ref[...] = acc_ref[...].T.astype(o_ref.dtype)


def _sparse_attention(proj, vct, topk, b, s):
    tq = TQ
    wc = H_C * HEAD_DIM
    nkb = s // TK
    nq = s // tq
    kern = functools.partial(_sparse_kernel, topk=float(topk))
    return pl.pallas_call(
        kern,
        out_shape=jax.ShapeDtypeStruct((b * s, wc), BF16),
        grid=(b, nq),
        in_specs=[
            pl.BlockSpec((None, tq, wc), lambda bb, i: (PB_QC, bb * nq + i, 0)),
            pl.BlockSpec((None, s, wc), lambda bb, i: (PB_KC, bb, 0)),
            pl.BlockSpec((nkb, wc, TK), lambda bb, i: (bb, 0, 0)),
            pl.BlockSpec((None, tq, PROJ_TN), lambda bb, i: (PB_TAIL, bb * nq + i, 0)),
            pl.BlockSpec((None, s, LANES), lambda bb, i: (PB_TAIL, bb, TAIL_KI // LANES)),
        ],
        out_specs=pl.BlockSpec((tq, wc), lambda bb, i: (bb * nq + i, 0)),
        scratch_shapes=[
            pltpu.VMEM((nkb, TK, tq), I32),
            pltpu.VMEM((nkb, TK, tq), I16),
            pltpu.VMEM((nkb, TK, tq), I16),
            pltpu.VMEM((wc, tq), F32),
            pltpu.VMEM((H_C, TK, tq), F32),
            pltpu.VMEM((H_C, TK, tq), BF16),
            pltpu.VMEM((TK, tq), F32),
        ],
        compiler_params=_params(("arbitrary", "arbitrary"), VMEM_LIMIT),
        name="sparse_attn",
    )(proj, proj, vct, proj, proj)


def _mix_kernel(x_ref, oa_ref, ob_ref, oc_ref, ga_ref, gb_ref, gc_ref,
                ua_ref, ub_ref, uc_ref, wo_ref, o_ref):
    def gate(g_ref):
        return jnp.concatenate([g_ref[0], g_ref[1]], axis=1).astype(F32)

    y = gate(ga_ref) * _dot(oa_ref[...], ua_ref[...])
    y = y + gate(gb_ref) * _dot(ob_ref[...], ub_ref[...])
    y = y + gate(gc_ref) * _dot(oc_ref[...], uc_ref[...])
    o_ref[...] = x_ref[...] + _dot(y.astype(BF16), wo_ref[...])


def _mix_out(x2, oa, ob, oc, proj, ua, ub, uc, wo):
    t, d = x2.shape
    tm = min(1024, t)
    per_gate = d // PROJ_TN
    assert PB_GATES % per_gate == 0
    gate_blk = PB_GATES // per_gate
    w = oa.shape[1]

    def full(shape):
        return pl.BlockSpec(shape, lambda i: (0, 0), pipeline_mode=pl.Buffered(1))

    return pl.pallas_call(
        _mix_kernel,
        out_shape=jax.ShapeDtypeStruct((t, d), F32),
        grid=(t // tm,),
        in_specs=[
            pl.BlockSpec((tm, d), lambda i: (i, 0)),
            pl.BlockSpec((tm, w), lambda i: (i, 0)),
            pl.BlockSpec((tm, w), lambda i: (i, 0)),
            pl.BlockSpec((tm, w), lambda i: (i, 0)),
            pl.BlockSpec((per_gate, tm, PROJ_TN), lambda i: (gate_blk, i, 0)),
            pl.BlockSpec((per_gate, tm, PROJ_TN), lambda i: (gate_blk + 1, i, 0)),
            pl.BlockSpec((per_gate, tm, PROJ_TN), lambda i: (gate_blk + 2, i, 0)),
            full((w, d)), full((w, d)), full((w, d)), full((d, d)),
        ],
        out_specs=pl.BlockSpec((tm, d), lambda i: (i, 0)),
        compiler_params=_params(("arbitrary",), VMEM_LIMIT),
        name="mix_out",
    )(x2, oa, ob, oc, proj, proj, proj, ua, ub, uc, wo)


def _mem_kv_kernel(mem_ref, g_ref, wk_ref, wvt_ref, k_ref, vt_ref):
    mem_n = _rms(mem_ref[...], g_ref[...]).astype(BF16)
    k_ref[...] = _dot(mem_n, wk_ref[...]).astype(k_ref.dtype)
    vt = _dot_nt(wvt_ref[...], mem_n).astype(vt_ref.dtype)
    n_mem = vt_ref.shape[2]
    for c in range(vt_ref.shape[0]):
        vt_ref[c] = vt[:, c * n_mem:(c + 1) * n_mem]


def _mem_kv(mem2, mem_norm, w_k, w_vt, n_mem):
    r, d = mem2.shape
    depth, _, n = w_k.shape
    tm = min(1024, r)
    return pl.pallas_call(
        _mem_kv_kernel,
        out_shape=(jax.ShapeDtypeStruct((depth, r, n), BF16),
                   jax.ShapeDtypeStruct((depth, r // n_mem, n, n_mem), BF16)),
        grid=(depth, r // tm),
        in_specs=[
            pl.BlockSpec((tm, d), lambda l, i: (i, 0)),
            pl.BlockSpec((1, d), lambda l, i: (0, 0)),
            pl.BlockSpec((None, d, n), lambda l, i: (l, 0, 0)),
            pl.BlockSpec((None, n, d), lambda l, i: (l, 0, 0)),
        ],
        out_specs=(pl.BlockSpec((None, tm, n), lambda l, i: (l, i, 0)),
                   pl.BlockSpec((None, tm // n_mem, n, n_mem), lambda l, i: (l, i, 0, 0))),
        compiler_params=_params(("arbitrary", "arbitrary"), VMEM_LIMIT),
        name="mem_kv",
    )(mem2, mem_norm, w_k, w_vt)


def _cross_kernel(x_ref, g_ref, wq_ref, k_ref, vt_ref, wo_ref, o_ref, acc_ref, s_ref, p_ref):
    n_mem = k_ref.shape[0]
    sub = acc_ref.shape[2]
    ones = jnp.ones((2 * SUBLANES, n_mem), BF16)
    for u in range(acc_ref.shape[0]):
        x = x_ref[u * sub:(u + 1) * sub, :]
        hn = _rms(x, g_ref[...]).astype(BF16)
        q = _dot(hn, wq_ref[...]).astype(BF16)
        ms = []
        for h in range(H_X):
            sl = slice((h // 2) * LANES, (h // 2 + 1) * LANES)
            s = _dot_nt(k_ref[:, sl], _head_halves(q[:, sl])[h % 2])
            s_ref[u, h] = s
            ms.append(jnp.max(s, axis=0, keepdims=True))
        for h in range(H_X):
            p_ref[u, h] = jnp.exp2(s_ref[u, h] - ms[h]).astype(BF16)
        for h in range(H_X):
            rows = slice(h * HEAD_DIM, (h + 1) * HEAD_DIM)
            pv = _dot(jnp.concatenate([vt_ref[rows, :], ones], axis=0), p_ref[u, h])
            acc_ref[u, rows, :] = pv[:HEAD_DIM] / pv[HEAD_DIM:HEAD_DIM + 1]
        o_ref[u * sub:(u + 1) * sub, :] = x + _dot(acc_ref[u].T.astype(BF16), wo_ref[...])


def _cross_attention(x2, gain, wq, k_mem, vt_mem, wo, seq):
    t, d = x2.shape
    sub = min(512, seq)
    tm = min(2 * sub, seq)
    per_batch = seq // tm
    n_mem, wk = k_mem.shape[0] // vt_mem.shape[0], k_mem.shape[1]
    return pl.pallas_call(
        _cross_kernel,
        out_shape=jax.ShapeDtypeStruct((t, d), F32),
        grid=(t // tm,),
        in_specs=[
            pl.BlockSpec((tm, d), lambda i: (i, 0)),
            pl.BlockSpec((1, d), lambda i: (0, 0)),
            pl.BlockSpec((d, wk), lambda i: (0, 0)),
            pl.BlockSpec((n_mem, wk), lambda i: (i // per_batch, 0)),
            pl.BlockSpec((None, wk, n_mem), lambda i: (i // per_batch, 0, 0)),
            pl.BlockSpec((wk, d), lambda i: (0, 0)),
        ],
        out_specs=pl.BlockSpec((tm, d), lambda i: (i, 0)),
        scratch_shapes=[
            pltpu.VMEM((tm // sub, wk, sub), F32),
            pltpu.VMEM((tm // sub, H_X, n_mem, sub), F32),
            pltpu.VMEM((tm // sub, H_X, n_mem, sub), BF16),
        ],
        compiler_params=_params(("arbitrary",), VMEM_LIMIT),
        name="cross_attn",
    )(x2, gain, wq, k_mem, vt_mem, wo)


def _ffn_kernel(x_ref, g_ref, wgu_ref, wd_ref, fg_ref, o_ref, acc_ref, *, final_norm, tf):
    x = x_ref[...]
    hn = _rms(x, g_ref[...]).astype(BF16)
    d_ff = wd_ref.shape[0]
    for c in range(d_ff // tf):
        gate = _dot(hn, wgu_ref[:, c * tf:(c + 1) * tf])
        up = _dot(hn, wgu_ref[:, d_ff + c * tf:d_ff + (c + 1) * tf])
        h = (gate / (1.0 + jnp.exp(-gate))) * up
        part = _dot(h.astype(BF16), wd_ref[c * tf:(c + 1) * tf, :])
        if c == 0:
            acc_ref[...] = part
        else:
            acc_ref[...] += part
    y = x + acc_ref[...]
    if final_norm:
        y = _rms(y, fg_ref[...])
    o_ref[...] = y


def _ffn(x2, gain, w_gu, w_down, final_gain, final_norm):
    t, d = x2.shape
    d_ff = w_down.shape[0]
    tm = min(1024, t)
    tf = 2 * LANES
    assert d_ff % tf == 0
    kern = functools.partial(_ffn_kernel, final_norm=final_norm, tf=tf)
    resident = pl.Buffered(1)
    return pl.pallas_call(
        kern,
        out_shape=jax.ShapeDtypeStruct((t, d), F32),
        grid=(t // tm,),
        in_specs=[
            pl.BlockSpec((tm, d), lambda i: (i, 0)),
            pl.BlockSpec((1, d), lambda i: (0, 0)),
            pl.BlockSpec((d, 2 * d_ff), lambda i: (0, 0), pipeline_mode=resident),
            pl.BlockSpec((d_ff, d), lambda i: (0, 0), pipeline_mode=resident),
            pl.BlockSpec((1, d), lambda i: (0, 0)),
        ],
        out_specs=pl.BlockSpec((tm, d), lambda i: (i, 0)),
        scratch_shapes=[pltpu.VMEM((tm, d), F32)],
        compiler_params=_params(("arbitrary",), VMEM_LIMIT),
        name="ffn",
    )(x2, gain, w_gu, w_down, final_gain)


def _layout_in_weights(w_in):
    depth, d, _ = w_in.shape
    w_a, w_b, w_c = H_A * HEAD_DIM, H_B * 2 * HEAD_DIM, H_C * HEAD_DIM
    splits = (w_a, w_a, w_a, w_b, w_b, w_b, w_c, w_c, w_c, H_IDX * D_IDX, D_IDX, H_IDX, 3 * d)
    offs = [0] + [int(o) for o in np.cumsum(splits)]
    scale = HEAD_DIM ** -0.5
    col_scale = np.ones((offs[-1],), np.float32)
    for part in (0, 3, 6):
        col_scale[offs[part]:offs[part + 1]] = scale * math.log2(math.e)
    col_scale[offs[9]:offs[10]] = D_IDX ** -0.5
    w = (w_in * col_scale).astype(BF16)
    qa, ka, va, qb, kb, vb, qc, kc, vc, qi, ki, wi, gates = (
        w[:, :, offs[n]:offs[n + 1]] for n in range(len(splits)))
    pad = jnp.zeros((depth, d, LANES - H_IDX), BF16)
    w_main = jnp.concatenate([gates, qb, kb, qc, kc, qi, ki, ki, wi, pad, qa, ka], axis=2)
    assert w_main.shape[2] == PB_COUNT * PROJ_TN
    return w_main, va.swapaxes(1, 2), vb.swapaxes(1, 2), vc.swapaxes(1, 2)


def _rope_tables(seq):
    pos = jnp.arange(seq, dtype=F32)
    inv = ROPE_THETA ** (-jnp.arange(0, HEAD_DIM, 2, dtype=F32) / HEAD_DIM)
    ang = pos[:, None] * inv[None, :]
    cos, sin = jnp.cos(ang), jnp.sin(ang)
    zero = jnp.zeros_like(sin)
    reps = LANES // HEAD_DIM
    cos_t = jnp.tile(jnp.concatenate([cos, cos], axis=1), (1, reps))
    sina_t = jnp.tile(jnp.concatenate([-sin, zero], axis=1), (1, reps))
    sinb_t = jnp.tile(jnp.concatenate([zero, sin], axis=1), (1, reps))
    return cos_t, sina_t, sinb_t


def kernel(x, mem, norm_mix, w_in, rel_bias_a, lambda_vecs, subln_b, w_up_a, w_up_b, w_up_c,
           w_out, norm_cross, w_q_x, w_kv_x, w_o_x, norm_ffn, w_gu, w_down, mem_norm, final_norm):
    b, s, d = x.shape
    depth = w_in.shape[0]
    assert s % TQ_B == 0 and LEFT_CHUNKS * CHUNK == 2 * TQ and TQ == TK
    topk = min(TOPK_MAX, s // 4)
    cos_t, sina_t, sinb_t = _rope_tables(s)
    x2 = x.reshape(b * s, d)
    mem2 = mem.reshape(b * mem.shape[1], d)
    wkv = H_X * HEAD_DIM
    k_mem, vt_mem = _mem_kv(mem2, mem_norm.reshape(1, d), w_kv_x[:, :, :wkv].astype(BF16),
                            w_kv_x[:, :, wkv:].swapaxes(1, 2).astype(BF16), mem.shape[1])
    w_main, w_vat, w_vbt, w_vct = _layout_in_weights(w_in)
    ua, ub, uc, wo = (w.astype(BF16) for w in (w_up_a, w_up_b, w_up_c, w_out))
    wq_x = (w_q_x * (HEAD_DIM ** -0.5 * math.log2(math.e))).astype(BF16)
    wo_x = w_o_x.astype(BF16)
    wgu, wdn = w_gu.astype(BF16), w_down.astype(BF16)
    for l in range(depth):
        proj, vat, vbt, vct = _in_proj(x2, norm_mix[l].reshape(1, d), w_main[l], cos_t, sina_t,
                                       sinb_t, w_vat[l], w_vbt[l], w_vct[l], s)
        lam_init = 0.8 - 0.6 * math.exp(-0.3 * l)
        o_a = _band_attention(proj, vat, _band_bias(rel_bias_a[l], TQ), b, s)
        o_b = _diff_attention(proj, vbt, lambda_vecs[l].astype(F32),
                              subln_b[l].reshape(LANES, 1), lam_init, b, s)
        o_c = _sparse_attention(proj, vct, topk, b, s)
        x2 = _mix_out(x2, o_a, o_b, o_c, proj, ua[l], ub[l], uc[l], wo[l])
        x2 = _cross_attention(x2, norm_cross[l].reshape(1, d), wq_x[l], k_mem[l], vt_mem[l],
                              wo_x[l], s)
        x2 = _ffn(x2, norm_ffn[l].reshape(1, d), wgu[l], wdn[l],
                  final_norm.reshape(1, d), final_norm=(l == depth - 1))
    return x2.reshape(b, s, d)
```
